```python
import jax, jax.numpy as jnp
from jax import lax
import numpy as np

D_MODEL = 2048
BATCH = 8
SEQ = 2048
DEPTH = 1

NORM_EPS = 1e-5
SWA_HEADS = 32
SWA_KV_HEADS = 4
SWA_HEAD_DIM = 64
SWA_WINDOW = 128
SWA_BLOCK = 128
GLA_HEADS = 4
GLA_DK = D_MODEL // (2 * GLA_HEADS)
GLA_DV = D_MODEL // GLA_HEADS
GLA_GATE_RANK = 16
GLA_TAU = 16.0
GLA_CHUNK = 64
N_EXPERTS = 32
TOP_K = 4
D_FF = D_MODEL
SWIGLU_LIMIT = 7.0
SWIGLU_ALPHA = 1.702
MOE_BLOCK = 128

SWA_Q = SWA_HEADS * SWA_HEAD_DIM
SWA_KV = SWA_KV_HEADS * SWA_HEAD_DIM
GLA_K = GLA_HEADS * GLA_DK
GLA_V = GLA_HEADS * GLA_DV
IN_SPLITS = (SWA_Q, SWA_KV, SWA_KV, GLA_K, GLA_K, GLA_V, GLA_V, GLA_GATE_RANK, D_MODEL, D_MODEL)
D_IN = SWA_Q + 2 * SWA_KV + 2 * GLA_K + 2 * GLA_V + GLA_GATE_RANK + 2 * D_MODEL

kernel_name = "hybrid_swa_gla_moe_block"


def rms_norm(x, g):
    xf = x.astype(jnp.float32)
    y = xf * lax.rsqrt(jnp.mean(xf * xf, axis=-1, keepdims=True) + NORM_EPS)
    return (y * g.astype(jnp.float32)).astype(x.dtype)


def alibi_slopes(n):
    return jnp.exp2(-8.0 * jnp.arange(1, n + 1, dtype=jnp.float32) / n)


def swa_attention(q, k, v, sinks):
    B, S = q.shape[0], q.shape[1]
    nb = S // SWA_BLOCK
    G = SWA_HEADS // SWA_KV_HEADS
    qb = q.reshape(B, nb, SWA_BLOCK, SWA_KV_HEADS, G, SWA_HEAD_DIM)

    def band(t):
        tb = t.reshape(B, nb, SWA_BLOCK, SWA_KV_HEADS, SWA_HEAD_DIM)
        prev = jnp.concatenate([jnp.zeros_like(tb[:, :1]), tb[:, :-1]], axis=1)
        return jnp.concatenate([prev, tb], axis=2)

    kb, vb = band(k), band(v)
    scores = jnp.einsum("bnqkgd,bnskd->bnkgqs", qb, kb).astype(jnp.float32) * (SWA_HEAD_DIM ** -0.5)
    r = jnp.arange(SWA_BLOCK)[:, None]
    j = jnp.arange(2 * SWA_BLOCK)[None, :]
    dist = r - j + SWA_BLOCK
    blk = jnp.arange(nb)[:, None, None]
    valid = (dist >= 0) & (dist < SWA_WINDOW) & (blk * SWA_BLOCK + j - SWA_BLOCK >= 0)
    slopes = alibi_slopes(SWA_HEADS).reshape(SWA_KV_HEADS, G)
    scores = scores - slopes[:, :, None, None] * dist.astype(jnp.float32)
    scores = jnp.where(valid[None, :, None, None], scores, -jnp.inf)
    sink = sinks.astype(jnp.float32).reshape(SWA_KV_HEADS, G)[:, :, None, None]
    m = jnp.maximum(jnp.max(scores, axis=-1, keepdims=True), sink)
    p = jnp.exp(scores - m)
    p = p / (jnp.sum(p, axis=-1, keepdims=True) + jnp.exp(sink - m))
    o = jnp.einsum("bnkgqs,bnskd->bnqkgd", p.astype(v.dtype), vb)
    return o.reshape(B, S, SWA_Q)


def gla_attention(q, k, v, log_a):
    B, S = q.shape[0], q.shape[1]
    nc = S // GLA_CHUNK
    f32 = jnp.float32

    def chunks(t):
        return t.astype(f32).reshape(B, nc, GLA_CHUNK, GLA_HEADS, t.shape[-1]).transpose(1, 0, 3, 2, 4)

    qc = chunks(q) * (GLA_DK ** -0.5)
    kc, vc = chunks(k), chunks(v)
    bc = jnp.cumsum(chunks(log_a), axis=3)
    causal = jnp.tril(jnp.ones((GLA_CHUNK, GLA_CHUNK), dtype=bool))

    def step(state, inp):
        q_, k_, v_, b_ = inp
        inter = jnp.einsum("bhtd,bhde->bhte", q_ * jnp.exp(b_), state)
        diff = jnp.where(causal[:, :, None], b_[:, :, :, None, :] - b_[:, :, None, :, :], -jnp.inf)
        att = jnp.einsum("bhtd,bhsd,bhtsd->bhts", q_, k_, jnp.exp(diff))
        intra = jnp.einsum("bhts,bhse->bhte", att, v_)
        b_last = b_[:, :, -1:, :]
        new_state = jnp.exp(b_last[:, :, 0, :, None]) * state + jnp.einsum(
            "bhsd,bhse->bhde", k_ * jnp.exp(b_last - b_), v_)
        return new_state, inter + intra

    state0 = jnp.zeros((B, GLA_HEADS, GLA_DK, GLA_DV), f32)
    _, o = lax.scan(step, state0, (qc, kc, vc, bc))
    return o.transpose(1, 0, 3, 2, 4).reshape(B, S, GLA_HEADS, GLA_DV).astype(v.dtype)


def moe_ffn(h, w_router, b_router, w_gate_up, b_gate_up, w_down, b_down):
    B, S, D = h.shape
    N = B * S
    A = N * TOP_K
    xf = h.reshape(N, D)
    logits = (xf @ w_router + b_router).astype(jnp.float32)
    top_val, top_idx = lax.top_k(logits, TOP_K)
    gates = jax.nn.softmax(top_val, axis=-1)
    eid = top_idx.reshape(A)
    order = jnp.argsort(eid)
    sorted_e = eid[order]
    token_of = order // TOP_K
    counts = jnp.bincount(eid, length=N_EXPERTS)
    padded = (counts + MOE_BLOCK - 1) // MOE_BLOCK * MOE_BLOCK
    offsets = jnp.cumsum(counts) - counts
    padded_ends = jnp.cumsum(padded)
    padded_offsets = padded_ends - padded
    pos = padded_offsets[sorted_e] + jnp.arange(A) - offsets[sorted_e]
    n_blocks = (A + N_EXPERTS * (MOE_BLOCK - 1) + MOE_BLOCK - 1) // MOE_BLOCK
    P = n_blocks * MOE_BLOCK
    x_pad = jnp.zeros((P, D), h.dtype).at[pos].set(xf[token_of])
    block_start = jnp.arange(n_blocks) * MOE_BLOCK
    block_e = jnp.minimum(jnp.sum(padded_ends[None, :] <= block_start[:, None], axis=1), N_EXPERTS - 1)

    def expert_block(args):
        xb, e = args
        gu = xb @ w_gate_up[e] + b_gate_up[e]
        glu, lin = gu[:, :D_FF], gu[:, D_FF:]
        glu = jnp.minimum(glu, SWIGLU_LIMIT)
        lin = jnp.clip(lin, -SWIGLU_LIMIT, SWIGLU_LIMIT)
        act = glu * jax.nn.sigmoid(SWIGLU_ALPHA * glu) * (lin + 1.0)
        return act @ w_down[e] + b_down[e]

    y_pad = lax.map(expert_block, (x_pad.reshape(n_blocks, MOE_BLOCK, D), block_e))
    y_sorted = y_pad.reshape(P, D)[pos] * gates.reshape(A)[order][:, None].astype(h.dtype)
    return jax.ops.segment_sum(y_sorted, token_of, num_segments=N).reshape(B, S, D)


def setup_inputs(seed: int = 0) -> dict:
    key = jax.random.key(seed)
    ks = jax.random.split(key, 17)
    L = DEPTH

    def nrm(k, shape, scale):
        return jax.random.normal(k, shape, jnp.float32) * scale

    return {
        "x": nrm(ks[0], (BATCH, SEQ, D_MODEL), 1.0),
        "attn_norm_g": 1.0 + nrm(ks[1], (L, D_MODEL), 0.02),
        "w_in": nrm(ks[2], (L, D_MODEL, D_IN), D_MODEL ** -0.5),
        "swa_sinks": nrm(ks[3], (L, SWA_HEADS), 0.5),
        "w_alpha2": nrm(ks[4], (L, GLA_GATE_RANK, GLA_K), GLA_GATE_RANK ** -0.5),
        "b_alpha": nrm(ks[5], (L, GLA_K), 0.5),
        "gla_norm_g": 1.0 + nrm(ks[6], (L, GLA_DV), 0.02),
        "w_out": nrm(ks[7], (L, D_MODEL, D_MODEL), D_MODEL ** -0.5),
        "ffn_norm_g": 1.0 + nrm(ks[8], (L, D_MODEL), 0.02),
        "w_router": nrm(ks[9], (L, D_MODEL, N_EXPERTS), D_MODEL ** -0.5),
        "b_router": nrm(ks[10], (L, N_EXPERTS), 0.01),
        "w_gate_up": nrm(ks[11], (L, N_EXPERTS, D_MODEL, 2 * D_FF), D_MODEL ** -0.5),
        "b_gate_up": nrm(ks[12], (L, N_EXPERTS, 2 * D_FF), 0.01),
        "w_down": nrm(ks[13], (L, N_EXPERTS, D_FF, D_MODEL), D_FF ** -0.5),
        "b_down": nrm(ks[14], (L, N_EXPERTS, D_MODEL), 0.01),
        "final_norm_g": 1.0 + nrm(ks[15], (D_MODEL,), 0.02),
    }


def reference(x, attn_norm_g, w_in, swa_sinks, w_alpha2, b_alpha, gla_norm_g, w_out,
              ffn_norm_g, w_router, b_router, w_gate_up, b_gate_up, w_down, b_down, final_norm_g):
    B, S = x.shape[0], x.shape[1]
    split_points = [int(p) for p in np.cumsum(IN_SPLITS)[:-1]]
    for l in range(DEPTH):
        h = rms_norm(x, attn_norm_g[l])
        q_a, k_a, v_a, q_b, k_b, v_b, r_b, a_lr, g_a, g_b = jnp.split(h @ w_in[l], split_points, axis=-1)
        o_a = swa_attention(q_a.reshape(B, S, SWA_HEADS, SWA_HEAD_DIM),
                            k_a.reshape(B, S, SWA_KV_HEADS, SWA_HEAD_DIM),
                            v_a.reshape(B, S, SWA_KV_HEADS, SWA_HEAD_DIM), swa_sinks[l])
        log_a = jax.nn.log_sigmoid((a_lr @ w_alpha2[l] + b_alpha[l]).astype(jnp.float32)) / GLA_TAU
        o_b = gla_attention(q_b.reshape(B, S, GLA_HEADS, GLA_DK),
                            k_b.reshape(B, S, GLA_HEADS, GLA_DK),
                            v_b.reshape(B, S, GLA_HEADS, GLA_DV),
                            log_a.reshape(B, S, GLA_HEADS, GLA_DK))
        o_b = rms_norm(o_b, gla_norm_g[l]).reshape(B, S, GLA_V) * jax.nn.silu(r_b)
        mix = (jax.nn.sigmoid(g_a) * o_a + jax.nn.sigmoid(g_b) * o_b) @ w_out[l]
        x = x + mix
        x = x + moe_ffn(rms_norm(x, ffn_norm_g[l]), w_router[l], b_router[l],
                        w_gate_up[l], b_gate_up[l], w_down[l], b_down[l])
    return rms_norm(x, final_norm_g)
```

```python
import functools

import jax
import jax.numpy as jnp
from jax import lax
from jax.experimental import pallas as pl
from jax.experimental.pallas import tpu as pltpu

F32 = jnp.float32
BF16 = jnp.bfloat16
I32 = jnp.int32

D_MODEL = 2048
NORM_EPS = 1e-5
SWA_HEADS = 32
SWA_KV_HEADS = 4
SWA_HEAD_DIM = 64
SWA_WINDOW = 128
SWA_GROUP = SWA_HEADS // SWA_KV_HEADS
GLA_HEADS = 4
GLA_DK = 256
GLA_DV = 512
GLA_GATE_RANK = 16
GLA_TAU = 16.0
GLA_CHUNK = 64
N_EXPERTS = 32
TOP_K = 4
D_FF = 2048
SWIGLU_LIMIT = 7.0
SWIGLU_ALPHA = 1.702

LANES = 128
MASK_NEG = -1e30

COL_QA = 0
COL_GA = 2048
COL_GB = 4096
COL_RB = 6144
COL_VB = 8192
COL_QB = 10240
COL_KB = 11264
COL_KA = 12288
COL_VA = 12544
P_COLS = 12800

VMEM_LIMIT = 56 * 1024 * 1024

INPROJ_TM = 1024
INPROJ_TN = 512
MERGE_TM = 256
SCATTER_TS = 256
EXPERT_TM = 512
EXPERT_TF = 512
COMBINE_TC = 128


def _cparams(sem):
    return pltpu.CompilerParams(dimension_semantics=sem, vmem_limit_bytes=VMEM_LIMIT)


def _sigmoid(x):
    return 1.0 / (1.0 + jnp.exp(-x))


def _inproj_kernel(x_ref, g_ref, w_ref, wa_ref, p_ref, a_ref, h_scr):
    @pl.when(pl.program_id(1) == 0)
    def _():
        x = x_ref[...]
        ms = jnp.mean(x * x, axis=-1, keepdims=True)
        h = (x * lax.rsqrt(ms + NORM_EPS) * g_ref[...]).astype(BF16)
        h_scr[...] = h
        a_ref[...] = jnp.dot(h, wa_ref[...], preferred_element_type=F32)

    p_ref[...] = jnp.dot(h_scr[...], w_ref[...], preferred_element_type=F32).astype(BF16)


def _inproj(x2d, g, w_main, w_a):
    n = x2d.shape[0]
    tm = min(INPROJ_TM, n)
    return pl.pallas_call(
        _inproj_kernel,
        out_shape=(jax.ShapeDtypeStruct((n, P_COLS), BF16),
                   jax.ShapeDtypeStruct((n, LANES), F32)),
        grid=(n // tm, P_COLS // INPROJ_TN),
        in_specs=[
            pl.BlockSpec((tm, D_MODEL), lambda i, j: (i, 0)),
            pl.BlockSpec((1, D_MODEL), lambda i, j: (0, 0)),
            pl.BlockSpec((D_MODEL, INPROJ_TN), lambda i, j: (0, j)),
            pl.BlockSpec((D_MODEL, LANES), lambda i, j: (0, 0)),
        ],
        out_specs=(
            pl.BlockSpec((tm, INPROJ_TN), lambda i, j: (i, j)),
            pl.BlockSpec((tm, LANES), lambda i, j: (i, 0)),
        ),
        scratch_shapes=[pltpu.VMEM((tm, D_MODEL), BF16)],
        compiler_params=_cparams(("arbitrary", "arbitrary")),
        name="inproj",
    )(x2d, g, w_main, w_a)


def _swa_kernel(slopes_ref, sinks_ref, q_ref, kp_ref, kc_ref, vp_ref, vc_ref, o_ref):
    n = pl.program_id(1)
    w = SWA_WINDOW
    kband = jnp.concatenate([kp_ref[...], kc_ref[...]], axis=0)
    vband = jnp.concatenate([vp_ref[...], vc_ref[...]], axis=0)
    vband_t = vband.astype(F32).T.astype(BF16)
    key = lax.broadcasted_iota(I32, (2 * w, w), 0)
    qry = lax.broadcasted_iota(I32, (2 * w, w), 1)
    dist = qry - key + w
    valid = (dist >= 0) & (dist < w) & ((key >= w) | (n > 0))
    negd = jnp.where(valid, -dist.astype(F32), MASK_NEG)
    q = q_ref[...] * jnp.asarray(SWA_HEAD_DIM ** -0.5, BF16)
    outs = []
    for kv in range(SWA_KV_HEADS):
        k_h = kband[:, kv * SWA_HEAD_DIM:(kv + 1) * SWA_HEAD_DIM]
        vt_h = vband_t[kv * SWA_HEAD_DIM:(kv + 1) * SWA_HEAD_DIM, :]
        for g in range(SWA_GROUP):
            h = kv * SWA_GROUP + g
            q_h = q[:, h * SWA_HEAD_DIM:(h + 1) * SWA_HEAD_DIM]
            st = lax.dot_general(k_h, q_h, (((1,), (1,)), ((), ())),
                                 preferred_element_type=F32)
            st = st + slopes_ref[h] * negd
            sink = sinks_ref[h]
            m = jnp.maximum(jnp.max(st, axis=0, keepdims=True), sink)
            p = jnp.exp(st - m)
            den = jnp.sum(p, axis=0, keepdims=True) + jnp.exp(sink - m)
            ot = jnp.dot(vt_h, p.astype(BF16), preferred_element_type=F32)
            outs.append(ot / den)
    o_t = jnp.concatenate(outs, axis=0)
    o_ref[...] = o_t.T.astype(BF16)


def _swa(p_act, slopes, sinks, batch, seq):
    n = batch * seq
    w = SWA_WINDOW
    nb = seq // w
    kvw = SWA_KV_HEADS * SWA_HEAD_DIM

    def cur(col):
        return lambda b, i: (b * nb + i, col)

    def prev(col):
        return lambda b, i: (b * nb + jnp.maximum(i - 1, 0), col)

    smem = pl.BlockSpec(memory_space=pltpu.SMEM)
    return pl.pallas_call(
        _swa_kernel,
        out_shape=jax.ShapeDtypeStruct((n, D_MODEL), BF16),
        grid=(batch, nb),
        in_specs=[
            smem, smem,
            pl.BlockSpec((w, D_MODEL), cur(COL_QA // D_MODEL)),
            pl.BlockSpec((w, kvw), prev(COL_KA // kvw)),
            pl.BlockSpec((w, kvw), cur(COL_KA // kvw)),
            pl.BlockSpec((w, kvw), prev(COL_VA // kvw)),
            pl.BlockSpec((w, kvw), cur(COL_VA // kvw)),
        ],
        out_specs=pl.BlockSpec((w, D_MODEL), lambda b, i: (b * nb + i, 0)),
        compiler_params=_cparams(("arbitrary", "arbitrary")),
        name="swa",
    )(slopes, sinks, p_act, p_act, p_act, p_act, p_act)


def _gla_kernel(q_ref, k_ref, v_ref, a_ref, w2_ref, ba_ref, gn_ref, o_ref, st_scr):
    c = GLA_CHUNK
    nchunks = q_ref.shape[0] // c
    st_scr[...] = jnp.zeros_like(st_scr)
    row = lax.broadcasted_iota(I32, (c, c), 0)
    col = lax.broadcasted_iota(I32, (c, c), 1)
    causal = row >= col
    tri = causal.astype(F32)
    scale = jnp.asarray(GLA_DK ** -0.5, F32)

    def chunk(ci, carry):
        r0 = pl.multiple_of(ci * c, c)
        z = jnp.dot(a_ref[pl.ds(r0, c), :], w2_ref[...], preferred_element_type=F32,
                    precision=lax.Precision.HIGHEST) + ba_ref[...]
        log_a = (jnp.minimum(z, 0.0) - jnp.log1p(jnp.exp(-jnp.abs(z)))) * (1.0 / GLA_TAU)
        b = jnp.dot(tri, log_a, preferred_element_type=F32,
                    precision=lax.Precision.HIGHEST)
        b_last = b[c - 1:c, :]
        b_mid = b[c // 2 - 1:c // 2, :]
        qf = q_ref[pl.ds(r0, c), :].astype(F32) * scale
        kf = k_ref[pl.ds(r0, c), :].astype(F32)
        v = v_ref[pl.ds(r0, c), :]
        q_inter = (qf * jnp.exp(b)).astype(BF16)
        q_intra = (qf * jnp.exp(b - b_mid)).astype(BF16)
        k_intra = (kf * jnp.exp(b_mid - b)).astype(BF16)
        k_state = (kf * jnp.exp(b_last - b)).astype(BF16)
        st = st_scr[...]
        inter = lax.dot_general(q_inter, st.astype(BF16), (((1,), (1,)), ((), ())),
                                preferred_element_type=F32)
        att = lax.dot_general(q_intra, k_intra, (((1,), (1,)), ((), ())),
                              preferred_element_type=F32)
        att = jnp.where(causal, att, 0.0).astype(BF16)
        intra = jnp.dot(att, v, preferred_element_type=F32)
        upd = lax.dot_general(v, k_state, (((0,), (0,)), ((), ())),
                              preferred_element_type=F32)
        st_scr[...] = st * jnp.exp(b_last) + upd
        o = inter + intra
        ms = jnp.mean(o * o, axis=-1, keepdims=True)
        o_ref[pl.ds(r0, c), :] = (o * lax.rsqrt(ms + NORM_EPS) * gn_ref[...]).astype(BF16)
        return carry

    lax.fori_loop(0, nchunks, chunk, 0)


def _gla(p_act, a_lr, w2_pad, b_alpha, gla_norm_g, batch, seq):
    n = batch * seq
    return pl.pallas_call(
        _gla_kernel,
        out_shape=jax.ShapeDtypeStruct((n, GLA_HEADS * GLA_DV), BF16),
        grid=(batch, GLA_HEADS),
        in_specs=[
            pl.BlockSpec((seq, GLA_DK), lambda b, h: (b, COL_QB // GLA_DK + h)),
            pl.BlockSpec((seq, GLA_DK), lambda b, h: (b, COL_KB // GLA_DK + h)),
            pl.BlockSpec((seq, GLA_DV), lambda b, h: (b, COL_VB // GLA_DV + h)),
            pl.BlockSpec((seq, LANES), lambda b, h: (b, 0)),
            pl.BlockSpec((LANES, GLA_DK), lambda b, h: (0, h)),
            pl.BlockSpec((1, GLA_DK), lambda b, h: (0, h)),
            pl.BlockSpec((1, GLA_DV), lambda b, h: (0, 0)),
        ],
        out_specs=pl.BlockSpec((seq, GLA_DV), lambda b, h: (b, h)),
        scratch_shapes=[pltpu.VMEM((GLA_DV, GLA_DK), F32)],
        compiler_params=_cparams(("arbitrary", "arbitrary")),
        name="gla",
    )(p_act, p_act, p_act, a_lr, w2_pad, b_alpha, gla_norm_g)


def _merge_kernel(ga_ref, gb_ref, rb_ref, oa_ref, ob_ref, x_ref, wout_ref, gffn_ref, wr_ref, br_ref,
                  x2_ref, h2_ref, eid_ref, gate_ref, rank_ref, cnt_ref, cnt_scr):
    tm = x_ref.shape[0]

    @pl.when(pl.program_id(0) == 0)
    def _():
        cnt_scr[...] = jnp.zeros_like(cnt_scr)

    ga = ga_ref[...].astype(F32)
    gb = gb_ref[...].astype(F32)
    rb = rb_ref[...].astype(F32)
    ob = ob_ref[...].astype(F32) * (rb * _sigmoid(rb))
    mix = _sigmoid(ga) * oa_ref[...].astype(F32) + _sigmoid(gb) * ob
    x2 = x_ref[...] + jnp.dot(mix.astype(BF16), wout_ref[...], preferred_element_type=F32)
    x2_ref[...] = x2
    ms = jnp.mean(x2 * x2, axis=-1, keepdims=True)
    h2 = x2 * lax.rsqrt(ms + NORM_EPS) * gffn_ref[...]
    h2_ref[...] = h2

    logits = jnp.dot(h2, wr_ref[...], preferred_element_type=F32,
                     precision=lax.Precision.HIGHEST) + br_ref[...]
    lane = lax.broadcasted_iota(I32, (tm, LANES), 1)
    neg_inf = jnp.asarray(-jnp.inf, F32)
    work = jnp.where(lane < N_EXPERTS, logits, neg_inf)
    vals, idxs = [], []
    for _ in range(TOP_K):
        m = jnp.max(work, axis=-1, keepdims=True)
        idx = jnp.min(jnp.where(work == m, lane, LANES), axis=-1, keepdims=True)
        vals.append(m)
        idxs.append(idx)
        work = jnp.where(lane == idx, neg_inf, work)
    exps = [jnp.exp(v - vals[0]) for v in vals]
    denom = exps[0] + exps[1] + exps[2] + exps[3]

    r_i = lax.broadcasted_iota(I32, (tm, tm), 0)
    c_i = lax.broadcasted_iota(I32, (tm, tm), 1)
    lower = (r_i > c_i).astype(BF16)
    base = cnt_scr[...]
    eid_out = jnp.zeros((tm, LANES), I32)
    gate_out = jnp.zeros((tm, LANES), F32)
    rank_out = jnp.zeros((tm, LANES), F32)
    for k in range(TOP_K):
        onehot = lane == idxs[k]
        oh_f = onehot.astype(F32)
        before = jnp.dot(lower, onehot.astype(BF16), preferred_element_type=F32)
        rank_k = jnp.sum(jnp.where(onehot, before + base, 0.0), axis=-1, keepdims=True)
        base = base + jnp.sum(oh_f, axis=0, keepdims=True)
        eid_out = jnp.where(lane == k, idxs[k], eid_out)
        gate_out = jnp.where(lane == k, exps[k] / denom, gate_out)
        rank_out = jnp.where(lane == k, rank_k, rank_out)
    cnt_scr[...] = base
    eid_ref[...] = eid_out
    gate_ref[...] = gate_out
    rank_ref[...] = rank_out.astype(I32)
    cnt_ref[...] = base


def _merge(p_act, o_a, o_b, x2d, w_out, g_ffn, w_router, b_router):
    n = x2d.shape[0]
    tm = min(MERGE_TM, n)
    row = lambda c: pl.BlockSpec((tm, D_MODEL), lambda i: (i, c))
    const = lambda shape: pl.BlockSpec(shape, lambda i: (0, 0))
    aux = pl.BlockSpec((tm, LANES), lambda i: (i, 0))
    return pl.pallas_call(
        _merge_kernel,
        out_shape=(
            jax.ShapeDtypeStruct((n, D_MODEL), F32),
            jax.ShapeDtypeStruct((n, D_MODEL), F32),
            jax.ShapeDtypeStruct((n, LANES), I32),
            jax.ShapeDtypeStruct((n, LANES), F32),
            jax.ShapeDtypeStruct((n, LANES), I32),
            jax.ShapeDtypeStruct((1, LANES), F32),
        ),
        grid=(n // tm,),
        in_specs=[
            row(COL_GA // D_MODEL), row(COL_GB // D_MODEL), row(COL_RB // D_MODEL),
            row(0), row(0), row(0),
            const((D_MODEL, D_MODEL)), const((1, D_MODEL)),
            const((D_MODEL, LANES)), const((1, LANES)),
        ],
        out_specs=(row(0), row(0), aux, aux, aux, const((1, LANES))),
        scratch_shapes=[pltpu.VMEM((1, LANES), F32)],
        compiler_params=_cparams(("arbitrary",)),
        name="merge",
    )(p_act, p_act, p_act, o_a, o_b, x2d, w_out, g_ffn, w_router, b_router)


def _scatter_kernel(pos_ref, h_ref, zeros_ref, xpad_ref, sem):
    del zeros_ref
    n_assign = pos_ref.shape[2]

    def row_copy(a):
        t = lax.shift_right_logical(a, 2)
        return pltpu.make_async_copy(h_ref.at[pl.ds(t, 1), :],
                                     xpad_ref.at[pl.ds(pos_ref[0, 0, a], 1), :], sem)

    def start(a, carry):
        row_copy(a).start()
        return carry

    def wait(a, carry):
        row_copy(a).wait()
        return carry

    lax.fori_loop(0, n_assign, start, 0)
    lax.fori_loop(0, n_assign, wait, 0)


def _scatter(h2, pos, p_rows):
    n = h2.shape[0]
    ts = min(SCATTER_TS, n)
    pos3 = pos.reshape(n // ts, 1, ts * TOP_K)
    zeros = jnp.zeros((p_rows, D_MODEL), F32)
    return pl.pallas_call(
        _scatter_kernel,
        out_shape=jax.ShapeDtypeStruct((p_rows, D_MODEL), F32),
        grid=(n // ts,),
        in_specs=[
            pl.BlockSpec((1, 1, ts * TOP_K), lambda i: (i, 0, 0), memory_space=pltpu.SMEM),
            pl.BlockSpec((ts, D_MODEL), lambda i: (i, 0)),
            pl.BlockSpec(memory_space=pl.ANY),
        ],
        out_specs=pl.BlockSpec(memory_space=pl.ANY),
        scratch_shapes=[pltpu.SemaphoreType.DMA],
        input_output_aliases={2: 0},
        compiler_params=_cparams(("arbitrary",)),
        name="scatter",
    )(pos3, h2, zeros)


def _experts_kernel(be_ref, nv_ref, x_ref, wg_ref, wl_ref, bg_ref, bl_ref, wd_ref, bd_ref,
                    y_ref, xb_scr, act_scr):
    b = pl.program_id(0)
    s = pl.program_id(1)
    nf = D_FF // EXPERT_TF
    valid = b < nv_ref[0]

    @pl.when(valid & (s == 0))
    def _():
        xb_scr[...] = x_ref[...].astype(BF16)

    @pl.when(valid & (s < nf))
    def _():
        xb = xb_scr[...]
        glu = jnp.dot(xb, wg_ref[...], preferred_element_type=F32) + bg_ref[...]
        lin = jnp.dot(xb, wl_ref[...], preferred_element_type=F32) + bl_ref[...]
        glu = jnp.minimum(glu, SWIGLU_LIMIT)
        lin = jnp.clip(lin, -SWIGLU_LIMIT, SWIGLU_LIMIT)
        act = glu * _sigmoid(SWIGLU_ALPHA * glu) * (lin + 1.0)
        act_scr[jnp.minimum(s, nf - 1)] = act.astype(BF16)

    @pl.when(valid & (s >= nf))
    def _():
        acc = jnp.dot(act_scr[0], wd_ref[0:EXPERT_TF, :], preferred_element_type=F32)
        for t in range(1, nf):
            acc += jnp.dot(act_scr[t], wd_ref[t * EXPERT_TF:(t + 1) * EXPERT_TF, :],
                           preferred_element_type=F32)
        y_ref[...] = acc + bd_ref[...]

    @pl.when(jnp.logical_not(valid) & (s >= nf))
    def _():
        y_ref[...] = jnp.zeros_like(y_ref)


def _experts(x_pad, block_e, n_valid, w_gu, b_gu, w_down, b_down):
    p_rows = x_pad.shape[0]
    nb = p_rows // EXPERT_TM
    nf = D_FF // EXPERT_TF
    nfl = nf - 1

    def xi(b, s, be, nv):
        return (jnp.minimum(b, nv[0] - 1), 0)

    def fa(b, s, be, nv):
        return jnp.where(b < nv[0], jnp.minimum(s, nfl), nfl)

    def fb(b, s, be, nv):
        return jnp.where(b < nv[0], jnp.maximum(s - nf, 0), nfl)

    grid_spec = pltpu.PrefetchScalarGridSpec(
        num_scalar_prefetch=2,
        grid=(nb, 2 * nf),
        in_specs=[
            pl.BlockSpec((EXPERT_TM, D_MODEL), xi),
            pl.BlockSpec((None, D_MODEL, EXPERT_TF), lambda b, s, be, nv: (be[b], 0, fa(b, s, be, nv))),
            pl.BlockSpec((None, D_MODEL, EXPERT_TF), lambda b, s, be, nv: (be[b], 0, nf + fa(b, s, be, nv))),
            pl.BlockSpec((None, 1, EXPERT_TF), lambda b, s, be, nv: (be[b], 0, fa(b, s, be, nv))),
            pl.BlockSpec((None, 1, EXPERT_TF), lambda b, s, be, nv: (be[b], 0, nf + fa(b, s, be, nv))),
            pl.BlockSpec((None, D_FF, EXPERT_TF), lambda b, s, be, nv: (be[b], 0, fb(b, s, be, nv))),
            pl.BlockSpec((None, 1, EXPERT_TF), lambda b, s, be, nv: (be[b], 0, fb(b, s, be, nv))),
        ],
        out_specs=pl.BlockSpec((EXPERT_TM, EXPERT_TF), lambda b, s, be, nv: (b, jnp.maximum(s - nf, 0))),
        scratch_shapes=[pltpu.VMEM((EXPERT_TM, D_MODEL), BF16),
                        pltpu.VMEM((nf, EXPERT_TM, EXPERT_TF), BF16)],
    )
    return pl.pallas_call(
        _experts_kernel,
        out_shape=jax.ShapeDtypeStruct((p_rows, D_MODEL), F32),
        grid_spec=grid_spec,
        compiler_params=_cparams(("arbitrary", "arbitrary")),
        name="experts",
    )(block_e, n_valid, x_pad, w_gu, w_gu, b_gu, b_gu, w_down, b_down)


def _combine_kernel(pos_ref, x2_ref, gate_ref, gfin_ref, ypad_ref, o_ref, ybuf, sem):
    n_assign = pos_ref.shape[2]

    def row_copy(a):
        t = lax.shift_right_logical(a, 2)
        k = lax.bitwise_and(a, TOP_K - 1)
        return pltpu.make_async_copy(ypad_ref.at[pl.ds(pos_ref[0, 0, a], 1), :],
                                     ybuf.at[k, pl.ds(t, 1), :], sem)

    def start(a, carry):
        row_copy(a).start()
        return carry

    def wait(a, carry):
        row_copy(a).wait()
        return carry

    lax.fori_loop(0, n_assign, start, 0)
    lax.fori_loop(0, n_assign, wait, 0)
    gates = gate_ref[...]
    x = x2_ref[...]
    for k in range(TOP_K):
        x = x + gates[:, k:k + 1] * ybuf[k]
    ms = jnp.mean(x * x, axis=-1, keepdims=True)
    o_ref[...] = x * lax.rsqrt(ms + NORM_EPS) * gfin_ref[...]


def _combine(x2, gates, pos, y_pad, g_final):
    n = x2.shape[0]
    tc = min(COMBINE_TC, n)
    pos3 = pos.reshape(n // tc, 1, tc * TOP_K)
    return pl.pallas_call(
        _combine_kernel,
        out_shape=jax.ShapeDtypeStruct((n, D_MODEL), F32),
        grid=(n // tc,),
        in_specs=[
            pl.BlockSpec((1, 1, tc * TOP_K), lambda i: (i, 0, 0), memory_space=pltpu.SMEM),
            pl.BlockSpec((tc, D_MODEL), lambda i: (i, 0)),
            pl.BlockSpec((tc, LANES), lambda i: (i, 0)),
            pl.BlockSpec((1, D_MODEL), lambda i: (0, 0)),
            pl.BlockSpec(memory_space=pl.ANY),
        ],
        out_specs=pl.BlockSpec((tc, D_MODEL), lambda i: (i, 0)),
        scratch_shapes=[pltpu.VMEM((TOP_K, tc, D_MODEL), F32), pltpu.SemaphoreType.DMA],
        compiler_params=_cparams(("arbitrary",)),
        name="combine",
    )(pos3, x2, gates, g_final, y_pad)


def _layer(x2d, batch, seq, attn_norm_g, w_in, swa_sinks, w_alpha2, b_alpha, gla_norm_g, w_out,
           ffn_norm_g, w_router, b_router, w_gate_up, b_gate_up, w_down, b_down):
    n = x2d.shape[0]
    o_qa, o_ka, o_va = 0, 2048, 2304
    o_qb, o_kb, o_vb, o_rb, o_al, o_ga, o_gb = 2560, 3584, 4608, 6656, 8704, 8720, 10768
    seg = lambda s, w: w_in[:, s:s + w]
    w_main = jnp.concatenate([
        seg(o_qa, 2048), seg(o_ga, 2048), seg(o_gb, 2048), seg(o_rb, 2048), seg(o_vb, 2048),
        seg(o_qb, 1024), seg(o_kb, 1024), seg(o_ka, 256), seg(o_va, 256)], axis=1).astype(BF16)
    w_a = jnp.pad(seg(o_al, GLA_GATE_RANK), ((0, 0), (0, LANES - GLA_GATE_RANK))).astype(BF16)
    w2_pad = jnp.pad(w_alpha2, ((0, LANES - GLA_GATE_RANK), (0, 0)))
    w_r = jnp.pad(w_router, ((0, 0), (0, LANES - N_EXPERTS)))
    b_r = jnp.pad(b_router, (0, LANES - N_EXPERTS)).reshape(1, LANES)
    slopes = jnp.exp2(-8.0 * jnp.arange(1, SWA_HEADS + 1, dtype=F32) / SWA_HEADS)

    p_act, a_lr = _inproj(x2d, attn_norm_g.reshape(1, D_MODEL), w_main, w_a)
    o_a = _swa(p_act, slopes, swa_sinks.astype(F32), batch, seq)
    o_b = _gla(p_act, a_lr, w2_pad, b_alpha.reshape(1, -1), gla_norm_g.reshape(1, -1), batch, seq)
    x2, h2, eid, gates, rank, cnt = _merge(
        p_act, o_a, o_b, x2d, w_out.astype(BF16), ffn_norm_g.reshape(1, D_MODEL), w_r, b_r)

    counts = cnt[0, :N_EXPERTS].astype(I32)
    padded = (counts + EXPERT_TM - 1) // EXPERT_TM * EXPERT_TM
    pad_ends = jnp.cumsum(padded)
    pad_off = pad_ends - padded
    pos = pad_off[eid[:, :TOP_K]] + rank[:, :TOP_K]
    nb = (n * TOP_K) // EXPERT_TM + N_EXPERTS
    block_start = jnp.arange(nb, dtype=I32) * EXPERT_TM
    block_e = jnp.minimum(jnp.sum(pad_ends[None, :] <= block_start[:, None], axis=1),
                          N_EXPERTS - 1).astype(I32)
    n_valid = (pad_ends[-1] // EXPERT_TM).astype(I32).reshape(1)

    x_pad = _scatter(h2, pos, nb * EXPERT_TM)
    y_pad = _experts(x_pad, block_e, n_valid, w_gate_up.astype(BF16),
                     b_gate_up.reshape(N_EXPERTS, 1, 2 * D_FF), w_down.astype(BF16),
                     b_down.reshape(N_EXPERTS, 1, D_MODEL))
    return x2, gates, pos, y_pad


def kernel(x, attn_norm_g, w_in, swa_sinks, w_alpha2, b_alpha, gla_norm_g, w_out, ffn_norm_g,
           w_router, b_router, w_gate_up, b_gate_up, w_down, b_down, final_norm_g):
    batch, seq, _ = x.shape
    assert w_in.shape[0] == 1, "single-layer block: the final norm is fused into the MoE combine"
    x2d = x.reshape(batch * seq, D_MODEL)
    x2, gates, pos, y_pad = _layer(
        x2d, batch, seq, attn_norm_g[0], w_in[0], swa_sinks[0], w_alpha2[0], b_alpha[0],
        gla_norm_g[0], w_out[0], ffn_norm_g[0], w_router[0], b_router[0], w_gate_up[0],
        b_gate_up[0], w_down[0], b_down[0])
    out = _combine(x2, gates, pos, y_pad, final_norm_g.reshape(1, D_MODEL))
    return out.reshape(batch, seq, D_MODEL)
```

```python
import functools

import jax
import jax.numpy as jnp
from jax import lax
from jax.experimental import pallas as pl
from jax.experimental.pallas import tpu as pltpu

F32 = jnp.float32
BF16 = jnp.bfloat16
I32 = jnp.int32
U32 = jnp.uint32

D_MODEL = 2048
D_HALF = D_MODEL // 2
NORM_EPS = 1e-5
SWA_HEADS = 32
SWA_KV_HEADS = 4
SWA_HEAD_DIM = 64
SWA_WINDOW = 128
SWA_GROUP = SWA_HEADS // SWA_KV_HEADS
GLA_HEADS = 4
GLA_DK = 256
GLA_DV = 512
GLA_GATE_RANK = 16
GLA_TAU = 16.0
GLA_CHUNK = 64
N_EXPERTS = 32
TOP_K = 4
D_FF = 2048
SWIGLU_LIMIT = 7.0
SWIGLU_ALPHA = 1.702

LANES = 128
SUBLANES = 8
MASK_NEG = -1e30

COL_QA = 0
COL_GA = 2048
COL_GB = 4096
COL_RB = 6144
COL_VB = 8192
COL_QB = 10240
COL_KB = 11264
COL_KA = 12288
COL_VA = 12544
P_COLS = 12800

VMEM_LIMIT = 56 * 1024 * 1024

INPROJ_TM = 1024
INPROJ_TN = 512
MERGE_TM = 256
MERGE_SLOTS = MERGE_TM * TOP_K + N_EXPERTS * SUBLANES
SEG_BITS = (MERGE_TM // SUBLANES).bit_length()
EXPERT_TM = 512
EXPERT_TF = 512
EXPERT_TN = 256


def _cparams(sem):
    return pltpu.CompilerParams(dimension_semantics=sem, vmem_limit_bytes=VMEM_LIMIT)


def _sigmoid(x):
    return 1.0 / (1.0 + jnp.exp(-x))


def _pack_halves(lo, hi):
    return pltpu.pack_elementwise([lo, hi], packed_dtype=BF16)


def _unpack_half(words, index):
    return pltpu.unpack_elementwise(words, index=index, packed_dtype=BF16, unpacked_dtype=F32)


def _inproj_kernel(x_ref, g_ref, w_ref, wa_ref, p_ref, a_ref, h_scr):
    @pl.when(pl.program_id(1) == 0)
    def _():
        x = x_ref[...]
        ms = jnp.mean(x * x, axis=-1, keepdims=True)
        h = (x * lax.rsqrt(ms + NORM_EPS) * g_ref[...]).astype(BF16)
        h_scr[...] = h
        a_ref[...] = jnp.dot(h, wa_ref[...], preferred_element_type=F32)

    p_ref[...] = jnp.dot(h_scr[...], w_ref[...], preferred_element_type=F32).astype(BF16)


def _inproj(x2d, g, w_main, w_a):
    n = x2d.shape[0]
    tm = min(INPROJ_TM, n)
    return pl.pallas_call(
        _inproj_kernel,
        out_shape=(jax.ShapeDtypeStruct((n, P_COLS), BF16),
                   jax.ShapeDtypeStruct((n, LANES), F32)),
        grid=(n // tm, P_COLS // INPROJ_TN),
        in_specs=[
            pl.BlockSpec((tm, D_MODEL), lambda i, j: (i, 0)),
            pl.BlockSpec((1, D_MODEL), lambda i, j: (0, 0)),
            pl.BlockSpec((D_MODEL, INPROJ_TN), lambda i, j: (0, j)),
            pl.BlockSpec((D_MODEL, LANES), lambda i, j: (0, 0)),
        ],
        out_specs=(
            pl.BlockSpec((tm, INPROJ_TN), lambda i, j: (i, j)),
            pl.BlockSpec((tm, LANES), lambda i, j: (i, 0)),
        ),
        scratch_shapes=[pltpu.VMEM((tm, D_MODEL), BF16)],
        compiler_params=_cparams(("arbitrary", "arbitrary")),
        name="inproj",
    )(x2d, g, w_main, w_a)


def _swa_kernel(slopes_ref, sinks_ref, q_ref, kp_ref, kc_ref, vp_ref, vc_ref, o_ref):
    n = pl.program_id(1)
    w = SWA_WINDOW
    kband = jnp.concatenate([kp_ref[...], kc_ref[...]], axis=0)
    vband = jnp.concatenate([vp_ref[...], vc_ref[...]], axis=0)
    vband_t = vband.astype(F32).T.astype(BF16)
    key = lax.broadcasted_iota(I32, (2 * w, w), 0)
    qry = lax.broadcasted_iota(I32, (2 * w, w), 1)
    dist = qry - key + w
    valid = (dist >= 0) & (dist < w) & ((key >= w) | (n > 0))
    negd = jnp.where(valid, -dist.astype(F32), MASK_NEG)
    q = q_ref[...] * jnp.asarray(SWA_HEAD_DIM ** -0.5, BF16)
    outs = []
    for kv in range(SWA_KV_HEADS):
        k_h = kband[:, kv * SWA_HEAD_DIM:(kv + 1) * SWA_HEAD_DIM]
        vt_h = vband_t[kv * SWA_HEAD_DIM:(kv + 1) * SWA_HEAD_DIM, :]
        for g in range(SWA_GROUP):
            h = kv * SWA_GROUP + g
            q_h = q[:, h * SWA_HEAD_DIM:(h + 1) * SWA_HEAD_DIM]
            st = lax.dot_general(k_h, q_h, (((1,), (1,)), ((), ())),
                                 preferred_element_type=F32)
            st = st + slopes_ref[h] * negd
            sink = sinks_ref[h]
            m = jnp.maximum(jnp.max(st, axis=0, keepdims=True), sink)
            p = jnp.exp(st - m)
            den = jnp.sum(p, axis=0, keepdims=True) + jnp.exp(sink - m)
            ot = jnp.dot(vt_h, p.astype(BF16), preferred_element_type=F32)
            outs.append(ot / den)
    o_t = jnp.concatenate(outs, axis=0)
    o_ref[...] = o_t.T.astype(BF16)


def _swa(p_act, slopes, sinks, batch, seq):
    n = batch * seq
    w = SWA_WINDOW
    nb = seq // w
    kvw = SWA_KV_HEADS * SWA_HEAD_DIM

    def cur(col):
        return lambda b, i: (b * nb + i, col)

    def prev(col):
        return lambda b, i: (b * nb + jnp.maximum(i - 1, 0), col)

    smem = pl.BlockSpec(memory_space=pltpu.SMEM)
    return pl.pallas_call(
        _swa_kernel,
        out_shape=jax.ShapeDtypeStruct((n, D_MODEL), BF16),
        grid=(batch, nb),
        in_specs=[
            smem, smem,
            pl.BlockSpec((w, D_MODEL), cur(COL_QA // D_MODEL)),
            pl.BlockSpec((w, kvw), prev(COL_KA // kvw)),
            pl.BlockSpec((w, kvw), cur(COL_KA // kvw)),
            pl.BlockSpec((w, kvw), prev(COL_VA // kvw)),
            pl.BlockSpec((w, kvw), cur(COL_VA // kvw)),
        ],
        out_specs=pl.BlockSpec((w, D_MODEL), lambda b, i: (b * nb + i, 0)),
        compiler_params=_cparams(("arbitrary", "arbitrary")),
        name="swa",
    )(slopes, sinks, p_act, p_act, p_act, p_act, p_act)


def _gla_kernel(q_ref, k_ref, v_ref, a_ref, w2_ref, ba_ref, gn_ref, o_ref, st_scr):
    c = GLA_CHUNK
    nchunks = q_ref.shape[0] // c
    st_scr[...] = jnp.zeros_like(st_scr)
    row = lax.broadcasted_iota(I32, (c, c), 0)
    col = lax.broadcasted_iota(I32, (c, c), 1)
    causal = row >= col
    tri = causal.astype(F32)
    scale = jnp.asarray(GLA_DK ** -0.5, F32)

    def chunk(ci, carry):
        r0 = pl.multiple_of(ci * c, c)
        z = jnp.dot(a_ref[pl.ds(r0, c), :], w2_ref[...], preferred_element_type=F32,
                    precision=lax.Precision.HIGHEST) + ba_ref[...]
        log_a = (jnp.minimum(z, 0.0) - jnp.log1p(jnp.exp(-jnp.abs(z)))) * (1.0 / GLA_TAU)
        b = jnp.dot(tri, log_a, preferred_element_type=F32,
                    precision=lax.Precision.HIGHEST)
        b_last = b[c - 1:c, :]
        b_mid = b[c // 2 - 1:c // 2, :]
        qf = q_ref[pl.ds(r0, c), :].astype(F32) * scale
        kf = k_ref[pl.ds(r0, c), :].astype(F32)
        v = v_ref[pl.ds(r0, c), :]
        q_inter = (qf * jnp.exp(b)).astype(BF16)
        q_intra = (qf * jnp.exp(b - b_mid)).astype(BF16)
        k_intra = (kf * jnp.exp(b_mid - b)).astype(BF16)
        k_state = (kf * jnp.exp(b_last - b)).astype(BF16)
        st = st_scr[...]
        inter = lax.dot_general(q_inter, st.astype(BF16), (((1,), (1,)), ((), ())),
                                preferred_element_type=F32)
        att = lax.dot_general(q_intra, k_intra, (((1,), (1,)), ((), ())),
                              preferred_element_type=F32)
        att = jnp.where(causal, att, 0.0).astype(BF16)
        intra = jnp.dot(att, v, preferred_element_type=F32)
        upd = lax.dot_general(v, k_state, (((0,), (0,)), ((), ())),
                              preferred_element_type=F32)
        st_scr[...] = st * jnp.exp(b_last) + upd
        o = inter + intra
        ms = jnp.mean(o * o, axis=-1, keepdims=True)
        o_ref[pl.ds(r0, c), :] = (o * lax.rsqrt(ms + NORM_EPS) * gn_ref[...]).astype(BF16)
        return carry

    lax.fori_loop(0, nchunks, chunk, 0)


def _gla(p_act, a_lr, w2_pad, b_alpha, gla_norm_g, batch, seq):
    n = batch * seq
    return pl.pallas_call(
        _gla_kernel,
        out_shape=jax.ShapeDtypeStruct((n, GLA_HEADS * GLA_DV), BF16),
        grid=(batch, GLA_HEADS),
        in_specs=[
            pl.BlockSpec((seq, GLA_DK), lambda b, h: (b, COL_QB // GLA_DK + h)),
            pl.BlockSpec((seq, GLA_DK), lambda b, h: (b, COL_KB // GLA_DK + h)),
            pl.BlockSpec((seq, GLA_DV), lambda b, h: (b, COL_VB // GLA_DV + h)),
            pl.BlockSpec((seq, LANES), lambda b, h: (b, 0)),
            pl.BlockSpec((LANES, GLA_DK), lambda b, h: (0, h)),
            pl.BlockSpec((1, GLA_DK), lambda b, h: (0, h)),
            pl.BlockSpec((1, GLA_DV), lambda b, h: (0, 0)),
        ],
        out_specs=pl.BlockSpec((seq, GLA_DV), lambda b, h: (b, h)),
        scratch_shapes=[pltpu.VMEM((GLA_DV, GLA_DK), F32)],
        compiler_params=_cparams(("arbitrary", "arbitrary")),
        name="gla",
    )(p_act, p_act, p_act, a_lr, w2_pad, b_alpha, gla_norm_g)


def _merge_kernel(ga_ref, gb_ref, rb_ref, oa_ref, ob_ref, x_ref, wout_ref, gffn_ref, wr_ref, br_ref,
                  x2_ref, xs_ref, slot_ref, gate_ref, cnt_ref):
    tm = x_ref.shape[0]
    slots = xs_ref.shape[0]
    ga = ga_ref[...].astype(F32)
    gb = gb_ref[...].astype(F32)
    rb = rb_ref[...].astype(F32)
    ob = ob_ref[...].astype(F32) * (rb * _sigmoid(rb))
    mix = _sigmoid(ga) * oa_ref[...].astype(F32) + _sigmoid(gb) * ob
    x2 = x_ref[...] + jnp.dot(mix.astype(BF16), wout_ref[...], preferred_element_type=F32)
    x2_ref[...] = x2
    ms = jnp.mean(x2 * x2, axis=-1, keepdims=True)
    h2 = x2 * lax.rsqrt(ms + NORM_EPS) * gffn_ref[...]

    logits = jnp.dot(h2, wr_ref[...], preferred_element_type=F32,
                     precision=lax.Precision.HIGHEST) + br_ref[...]
    lane = lax.broadcasted_iota(I32, (tm, LANES), 1)
    neg_inf = jnp.asarray(-jnp.inf, F32)
    work = jnp.where(lane < N_EXPERTS, logits, neg_inf)
    vals, idxs = [], []
    for _ in range(TOP_K):
        m = jnp.max(work, axis=-1, keepdims=True)
        idx = jnp.min(jnp.where(work == m, lane, LANES), axis=-1, keepdims=True)
        vals.append(m)
        idxs.append(idx)
        work = jnp.where(lane == idx, neg_inf, work)
    exps = [jnp.exp(v - vals[0]) for v in vals]
    denom = exps[0] + exps[1] + exps[2] + exps[3]

    r_i = lax.broadcasted_iota(I32, (tm, tm), 0)
    c_i = lax.broadcasted_iota(I32, (tm, tm), 1)
    lower = jnp.where(r_i > c_i, 1.0, 0.0).astype(BF16)
    base = jnp.zeros((1, LANES), F32)
    onehots, ranks = [], []
    for k in range(TOP_K):
        onehot = lane == idxs[k]
        before = jnp.dot(lower, jnp.where(onehot, 1.0, 0.0).astype(BF16), preferred_element_type=F32)
        onehots.append(onehot)
        ranks.append(before + base)
        base = base + jnp.sum(jnp.where(onehot, 1.0, 0.0), axis=0, keepdims=True)
    cnt8 = jnp.floor((base + (SUBLANES - 1)) * (1.0 / SUBLANES))
    e_i = lax.broadcasted_iota(I32, (LANES, LANES), 0)
    e_j = lax.broadcasted_iota(I32, (LANES, LANES), 1)
    upper = jnp.where(e_i < e_j, 1.0, 0.0).astype(BF16)
    seg8 = jnp.dot(jnp.broadcast_to(cnt8, (SUBLANES, LANES)).astype(BF16), upper,
                   preferred_element_type=F32)[0:1, :]
    seg_row = seg8 * float(SUBLANES)
    slot_out = jnp.full((tm, LANES), -1.0, F32)
    gate_out = jnp.zeros((tm, LANES), F32)
    for k in range(TOP_K):
        slot_k = jnp.sum(jnp.where(onehots[k], ranks[k] + seg_row, 0.0), axis=-1, keepdims=True)
        slot_out = jnp.where(lane == k, slot_k, slot_out)
        gate_out = jnp.where(lane == k, exps[k] / denom, gate_out)
    slot_ref[...] = slot_out.astype(I32)
    gate_ref[...] = gate_out
    cnt_ref[...] = jnp.broadcast_to(base, (SUBLANES, LANES))

    slot_t = slot_out.T
    p_i = lax.broadcasted_iota(I32, (slots, tm), 0).astype(F32)
    perm = jnp.where(p_i == slot_t[0:1, :], 1.0, 0.0)
    for k in range(1, TOP_K):
        perm = perm + jnp.where(p_i == slot_t[k:k + 1, :], 1.0, 0.0)
    perm = perm.astype(BF16)
    h2b = h2.astype(BF16)
    cw = 256
    for c in range(D_HALF // cw):
        lo = jnp.dot(perm, h2b[:, c * cw:(c + 1) * cw], preferred_element_type=F32)
        hi = jnp.dot(perm, h2b[:, D_HALF + c * cw:D_HALF + (c + 1) * cw], preferred_element_type=F32)
        xs_ref[:, c * cw:(c + 1) * cw] = _pack_halves(lo, hi)


def _merge(p_act, o_a, o_b, x2d, w_out, g_ffn, w_router, b_router):
    n = x2d.shape[0]
    tm = MERGE_TM
    nt = n // tm
    row = lambda c: pl.BlockSpec((tm, D_MODEL), lambda i: (i, c))
    const = lambda shape: pl.BlockSpec(shape, lambda i: (0, 0))
    aux = pl.BlockSpec((tm, LANES), lambda i: (i, 0))
    return pl.pallas_call(
        _merge_kernel,
        out_shape=(
            jax.ShapeDtypeStruct((n, D_MODEL), F32),
            jax.ShapeDtypeStruct((nt * MERGE_SLOTS, D_HALF), U32),
            jax.ShapeDtypeStruct((n, LANES), I32),
            jax.ShapeDtypeStruct((n, LANES), F32),
            jax.ShapeDtypeStruct((nt * SUBLANES, LANES), F32),
        ),
        grid=(nt,),
        in_specs=[
            row(COL_GA // D_MODEL), row(COL_GB // D_MODEL), row(COL_RB // D_MODEL),
            row(0), row(0), row(0),
            pl.BlockSpec((D_MODEL, D_MODEL), lambda i: (0, 0), pipeline_mode=pl.Buffered(1)),
            const((1, D_MODEL)),
            pl.BlockSpec((D_MODEL, LANES), lambda i: (0, 0), pipeline_mode=pl.Buffered(1)),
            const((1, LANES)),
        ],
        out_specs=(row(0), pl.BlockSpec((MERGE_SLOTS, D_HALF), lambda i: (i, 0)), aux, aux,
                   pl.BlockSpec((SUBLANES, LANES), lambda i: (i, 0))),
        compiler_params=_cparams(("arbitrary",)),
        name="merge",
    )(p_act, p_act, p_act, o_a, o_b, x2d, w_out, g_ffn, w_router, b_router)


def _segmove_kernel(src8_ref, dst8_ref, cnt8_ref, src_ref, zeros_ref, dst_ref, sem, *, segs_per_step):
    del zeros_ref
    i = pl.program_id(0)
    n_groups = pl.num_programs(0) - 1

    def for_each_copy(group, fn):
        def seg_body(j, carry):
            seg = group * segs_per_step + j
            s = src8_ref[seg] * SUBLANES
            d = dst8_ref[seg] * SUBLANES
            c = cnt8_ref[seg]
            off = jnp.zeros((), I32)
            for bit in reversed(range(SEG_BITS)):
                rows = SUBLANES << bit
                flag = lax.bitwise_and(lax.shift_right_logical(c, bit), 1)

                @pl.when(flag == 1)
                def _(rows=rows, off=off):
                    fn(pltpu.make_async_copy(
                        src_ref.at[pl.ds(pl.multiple_of(s + off, SUBLANES), rows), :],
                        dst_ref.at[pl.ds(pl.multiple_of(d + off, SUBLANES), rows), :], sem))

                off = off + flag * rows
            return carry

        lax.fori_loop(0, segs_per_step, seg_body, 0)

    @pl.when(i < n_groups)
    def _():
        for_each_copy(i, lambda cp: cp.start())

    @pl.when(i > 0)
    def _():
        for_each_copy(i - 1, lambda cp: cp.wait())


def _segmove(src, src8, dst8, cnt8, dst_rows, name):
    n_seg = cnt8.shape[0]
    segs_per_step = N_EXPERTS
    n_groups = n_seg // segs_per_step
    zeros = jnp.zeros((dst_rows, src.shape[1]), src.dtype)
    any_spec = pl.BlockSpec(memory_space=pl.ANY)
    grid_spec = pltpu.PrefetchScalarGridSpec(
        num_scalar_prefetch=3,
        grid=(n_groups + 1,),
        in_specs=[any_spec, any_spec],
        out_specs=any_spec,
        scratch_shapes=[pltpu.SemaphoreType.DMA],
    )
    return pl.pallas_call(
        functools.partial(_segmove_kernel, segs_per_step=segs_per_step),
        out_shape=jax.ShapeDtypeStruct((dst_rows, src.shape[1]), src.dtype),
        grid_spec=grid_spec,
        input_output_aliases={4: 0},
        compiler_params=_cparams(("arbitrary",)),
        name=name,
    )(src8, dst8, cnt8, src, zeros)


def _experts_kernel(be_ref, nv_ref, x_ref, wg_ref, wl_ref, bg_ref, bl_ref, wdl_ref, wdh_ref,
                    bdl_ref, bdh_ref, y_ref, xb_scr, act_scr):
    b = pl.program_id(0)
    s = pl.program_id(1)
    nf = D_FF // EXPERT_TF
    valid = b < nv_ref[0]

    @pl.when(valid & (s == 0))
    def _():
        words = x_ref[...]
        xb_scr[:, 0:D_HALF] = _unpack_half(words, 0).astype(BF16)
        xb_scr[:, D_HALF:D_MODEL] = _unpack_half(words, 1).astype(BF16)

    @pl.when(valid & (s < nf))
    def _():
        xb = xb_scr[...]
        glu = jnp.dot(xb, wg_ref[...], preferred_element_type=F32) + bg_ref[...]
        lin = jnp.dot(xb, wl_ref[...], preferred_element_type=F32) + bl_ref[...]
        glu = jnp.minimum(glu, SWIGLU_LIMIT)
        lin = jnp.clip(lin, -SWIGLU_LIMIT, SWIGLU_LIMIT)
        act = glu * _sigmoid(SWIGLU_ALPHA * glu) * (lin + 1.0)
        act_scr[jnp.minimum(s, nf - 1)] = act.astype(BF16)

    @pl.when(valid & (s >= nf))
    def _():
        def down(w_ref, b_ref):
            acc = jnp.dot(act_scr[0], w_ref[0:EXPERT_TF, :], preferred_element_type=F32)
            for t in range(1, nf):
                acc += jnp.dot(act_scr[t], w_ref[t * EXPERT_TF:(t + 1) * EXPERT_TF, :],
                               preferred_element_type=F32)
            return acc + b_ref[...]

        y_ref[...] = _pack_halves(down(wdl_ref, bdl_ref), down(wdh_ref, bdh_ref))

    @pl.when(jnp.logical_not(valid) & (s >= nf))
    def _():
        y_ref[...] = jnp.zeros_like(y_ref)


def _experts(x_pad, block_e, n_valid, w_gu, b_gu, w_down, b_down):
    p_rows = x_pad.shape[0]
    nb = p_rows // EXPERT_TM
    nf = D_FF // EXPERT_TF
    nfl = nf - 1
    nc = D_HALF // EXPERT_TN
    ncl = nc - 1

    def xi(b, s, be, nv):
        return (jnp.minimum(b, nv[0] - 1), 0)

    def fa(b, s, be, nv):
        return jnp.where(b < nv[0], jnp.minimum(s, nfl), nfl)

    def fb(b, s, be, nv):
        return jnp.where(b < nv[0], jnp.maximum(s - nf, 0), ncl)

    grid_spec = pltpu.PrefetchScalarGridSpec(
        num_scalar_prefetch=2,
        grid=(nb, nf + nc),
        in_specs=[
            pl.BlockSpec((EXPERT_TM, D_HALF), xi),
            pl.BlockSpec((None, D_MODEL, EXPERT_TF), lambda b, s, be, nv: (be[b], 0, fa(b, s, be, nv))),
            pl.BlockSpec((None, D_MODEL, EXPERT_TF), lambda b, s, be, nv: (be[b], 0, nf + fa(b, s, be, nv))),
            pl.BlockSpec((None, 1, EXPERT_TF), lambda b, s, be, nv: (be[b], 0, fa(b, s, be, nv))),
            pl.BlockSpec((None, 1, EXPERT_TF), lambda b, s, be, nv: (be[b], 0, nf + fa(b, s, be, nv))),
            pl.BlockSpec((None, D_FF, EXPERT_TN), lambda b, s, be, nv: (be[b], 0, fb(b, s, be, nv))),
            pl.BlockSpec((None, D_FF, EXPERT_TN), lambda b, s, be, nv: (be[b], 0, nc + fb(b, s, be, nv))),
            pl.BlockSpec((None, 1, EXPERT_TN), lambda b, s, be, nv: (be[b], 0, fb(b, s, be, nv))),
            pl.BlockSpec((None, 1, EXPERT_TN), lambda b, s, be, nv: (be[b], 0, nc + fb(b, s, be, nv))),
        ],
        out_specs=pl.BlockSpec((EXPERT_TM, EXPERT_TN), lambda b, s, be, nv: (b, jnp.maximum(s - nf, 0))),
        scratch_shapes=[pltpu.VMEM((EXPERT_TM, D_MODEL), BF16),
                        pltpu.VMEM((nf, EXPERT_TM, EXPERT_TF), BF16)],
    )
    return pl.pallas_call(
        _experts_kernel,
        out_shape=jax.ShapeDtypeStruct((p_rows, D_HALF), U32),
        grid_spec=grid_spec,
        compiler_params=_cparams(("arbitrary", "arbitrary")),
        name="experts",
    )(block_e, n_valid, x_pad, w_gu, w_gu, b_gu, b_gu, w_down, w_down, b_down, b_down)


def _combine_kernel(ys_ref, slot_ref, gate_ref, x2_ref, gfin_ref, o_ref):
    tm = x2_ref.shape[0]
    slots = ys_ref.shape[0]
    slot = slot_ref[...]
    gates = gate_ref[...]
    p_j = lax.broadcasted_iota(I32, (tm, slots), 1)
    gmat = jnp.where(p_j == slot[:, 0:1], gates[:, 0:1], 0.0)
    for k in range(1, TOP_K):
        gmat = gmat + jnp.where(p_j == slot[:, k:k + 1], gates[:, k:k + 1], 0.0)
    gmat = gmat.astype(BF16)
    words = ys_ref[...]
    moe_lo = jnp.dot(gmat, _unpack_half(words, 0).astype(BF16), preferred_element_type=F32)
    moe_hi = jnp.dot(gmat, _unpack_half(words, 1).astype(BF16), preferred_element_type=F32)
    x = x2_ref[...] + jnp.concatenate([moe_lo, moe_hi], axis=1)
    ms = jnp.mean(x * x, axis=-1, keepdims=True)
    o_ref[...] = x * lax.rsqrt(ms + NORM_EPS) * gfin_ref[...]


def _combine(y_sorted, slots, gates, x2, g_final):
    n = x2.shape[0]
    tm = MERGE_TM
    aux = pl.BlockSpec((tm, LANES), lambda i: (i, 0))
    return pl.pallas_call(
        _combine_kernel,
        out_shape=jax.ShapeDtypeStruct((n, D_MODEL), F32),
        grid=(n // tm,),
        in_specs=[
            pl.BlockSpec((MERGE_SLOTS, D_HALF), lambda i: (i, 0)),
            aux, aux,
            pl.BlockSpec((tm, D_MODEL), lambda i: (i, 0)),
            pl.BlockSpec((1, D_MODEL), lambda i: (0, 0)),
        ],
        out_specs=pl.BlockSpec((tm, D_MODEL), lambda i: (i, 0)),
        compiler_params=_cparams(("arbitrary",)),
        name="combine",
    )(y_sorted, slots, gates, x2, g_final)


def _layer(x2d, batch, seq, attn_norm_g, w_in, swa_sinks, w_alpha2, b_alpha, gla_norm_g, w_out,
           ffn_norm_g, w_router, b_router, w_gate_up, b_gate_up, w_down, b_down, final_norm_g):
    n = x2d.shape[0]
    assert n % MERGE_TM == 0 and seq % SWA_WINDOW == 0 and seq % GLA_CHUNK == 0
    o_qa, o_ka, o_va = 0, 2048, 2304
    o_qb, o_kb, o_vb, o_rb, o_al, o_ga, o_gb = 2560, 3584, 4608, 6656, 8704, 8720, 10768
    seg = lambda s, w: w_in[:, s:s + w]
    w_main = jnp.concatenate([
        seg(o_qa, 2048), seg(o_ga, 2048), seg(o_gb, 2048), seg(o_rb, 2048), seg(o_vb, 2048),
        seg(o_qb, 1024), seg(o_kb, 1024), seg(o_ka, 256), seg(o_va, 256)], axis=1).astype(BF16)
    w_a = jnp.pad(seg(o_al, GLA_GATE_RANK), ((0, 0), (0, LANES - GLA_GATE_RANK))).astype(BF16)
    w2_pad = jnp.pad(w_alpha2, ((0, LANES - GLA_GATE_RANK), (0, 0)))
    w_r = jnp.pad(w_router, ((0, 0), (0, LANES - N_EXPERTS)))
    b_r = jnp.pad(b_router, (0, LANES - N_EXPERTS)).reshape(1, LANES)
    slopes = jnp.exp2(-8.0 * jnp.arange(1, SWA_HEADS + 1, dtype=F32) / SWA_HEADS)

    p_act, a_lr = _inproj(x2d, attn_norm_g.reshape(1, D_MODEL), w_main, w_a)
    o_a = _swa(p_act, slopes, swa_sinks.astype(F32), batch, seq)
    o_b = _gla(p_act, a_lr, w2_pad, b_alpha.reshape(1, -1), gla_norm_g.reshape(1, -1), batch, seq)
    x2, x_sorted, slots, gates, cnt = _merge(
        p_act, o_a, o_b, x2d, w_out.astype(BF16), ffn_norm_g.reshape(1, D_MODEL), w_r, b_r)

    nt = n // MERGE_TM
    cnt_t = cnt[::SUBLANES, :N_EXPERTS].astype(I32)
    c8 = (cnt_t + SUBLANES - 1) // SUBLANES
    tile8 = (jnp.arange(nt, dtype=I32) * (MERGE_SLOTS // SUBLANES))[:, None]
    sorted8 = tile8 + jnp.cumsum(c8, axis=1) - c8
    rows_e = jnp.sum(c8, axis=0) * SUBLANES
    padded = (rows_e + EXPERT_TM - 1) // EXPERT_TM * EXPERT_TM
    pad_ends = jnp.cumsum(padded)
    pad_off = pad_ends - padded
    padded8 = (pad_off // SUBLANES)[None, :] + jnp.cumsum(c8, axis=0) - c8
    nb = -(-(n * TOP_K + nt * N_EXPERTS * (SUBLANES - 1)) // EXPERT_TM) + N_EXPERTS
    block_start = jnp.arange(nb, dtype=I32) * EXPERT_TM
    block_e = jnp.minimum(jnp.sum(pad_ends[None, :] <= block_start[:, None], axis=1),
                          N_EXPERTS - 1).astype(I32)
    n_valid = (pad_ends[-1] // EXPERT_TM).astype(I32).reshape(1)
    sorted8, padded8, c8 = sorted8.reshape(-1), padded8.reshape(-1), c8.reshape(-1)

    x_pad = _segmove(x_sorted, sorted8, padded8, c8, nb * EXPERT_TM, "dispatch")
    y_pad = _experts(x_pad, block_e, n_valid, w_gate_up.astype(BF16),
                     b_gate_up.reshape(N_EXPERTS, 1, 2 * D_FF), w_down.astype(BF16),
                     b_down.reshape(N_EXPERTS, 1, D_MODEL))
    y_sorted = _segmove(y_pad, padded8, sorted8, c8, nt * MERGE_SLOTS, "collect")
    return _combine(y_sorted, slots, gates, x2, final_norm_g.reshape(1, D_MODEL))


def kernel(x, attn_norm_g, w_in, swa_sinks, w_alpha2, b_alpha, gla_norm_g, w_out, ffn_norm_g,
           w_router, b_router, w_gate_up, b_gate_up, w_down, b_down, final_norm_g):
    batch, seq, _ = x.shape
    assert w_in.shape[0] == 1, "single-layer block: the final norm is fused into the MoE combine"
    out = _layer(
        x.reshape(batch * seq, D_MODEL), batch, seq, attn_norm_g[0], w_in[0], swa_sinks[0],
        w_alpha2[0], b_alpha[0], gla_norm_g[0], w_out[0], ffn_norm_g[0], w_router[0], b_router[0],
        w_gate_up[0], b_gate_up[0], w_down[0], b_down[0], final_norm_g)
    return out.reshape(batch, seq, D_MODEL)
```

```python
import functools

import jax
import jax.numpy as jnp
from jax import lax
from jax.experimental import pallas as pl
from jax.experimental.pallas import tpu as pltpu

F32 = jnp.float32
BF16 = jnp.bfloat16
I32 = jnp.int32
U32 = jnp.uint32

D_MODEL = 2048
D_HALF = D_MODEL // 2
NORM_EPS = 1e-5
SWA_HEADS = 32
SWA_KV_HEADS = 4
SWA_HEAD_DIM = 64
SWA_WINDOW = 128
SWA_GROUP = SWA_HEADS // SWA_KV_HEADS
GLA_HEADS = 4
GLA_DK = 256
GLA_DV = 512
GLA_GATE_RANK = 16
GLA_TAU = 16.0
GLA_CHUNK = 64
N_EXPERTS = 32
TOP_K = 4
D_FF = 2048
SWIGLU_LIMIT = 7.0
SWIGLU_ALPHA = 1.702

LANES = 128
SUBLANES = 8
MASK_NEG = -1e30

COL_QA = 0
COL_GA = 2048
COL_GB = 4096
COL_RB = 6144
COL_VB = 8192
COL_QB = 10240
COL_KB = 11264
COL_KA = 12288
COL_VA = 12544
P_COLS = 12800

VMEM_LIMIT = 56 * 1024 * 1024

INPROJ_TM = 1024
INPROJ_TN = 512
MERGE_TM = 256
MERGE_SLOTS = MERGE_TM * TOP_K + N_EXPERTS * SUBLANES
SEG_BITS = (MERGE_TM // SUBLANES).bit_length()
PAIR = 2
EXPERT_CHUNK = 2560
EXPERT_ROWS = EXPERT_CHUNK + MERGE_TM
EXPERT_SUB = 256
EXPERT_TF = 256
EXPERT_TN = 128


def _cparams(sem):
    return pltpu.CompilerParams(dimension_semantics=sem, vmem_limit_bytes=VMEM_LIMIT)


def _sigmoid(x):
    return 1.0 / (1.0 + jnp.exp(-x))


def _pack_rows(lo, hi):
    return pltpu.bitcast(pltpu.pack_elementwise([lo, hi], packed_dtype=BF16), BF16)


def _unpack_rows(z):
    words = pltpu.bitcast(z, U32)
    return (pltpu.unpack_elementwise(words, index=0, packed_dtype=BF16, unpacked_dtype=F32),
            pltpu.unpack_elementwise(words, index=1, packed_dtype=BF16, unpacked_dtype=F32))


def _inproj_kernel(x_ref, g_ref, w_ref, wa_ref, p_ref, a_ref, h_scr):
    @pl.when(pl.program_id(1) == 0)
    def _():
        x = x_ref[...]
        ms = jnp.mean(x * x, axis=-1, keepdims=True)
        h = (x * lax.rsqrt(ms + NORM_EPS) * g_ref[...]).astype(BF16)
        h_scr[...] = h
        a_ref[...] = jnp.dot(h, wa_ref[...], preferred_element_type=F32)

    p_ref[...] = jnp.dot(h_scr[...], w_ref[...], preferred_element_type=F32).astype(BF16)


def _inproj(x2d, g, w_main, w_a):
    n = x2d.shape[0]
    tm = min(INPROJ_TM, n)
    return pl.pallas_call(
        _inproj_kernel,
        out_shape=(jax.ShapeDtypeStruct((n, P_COLS), BF16),
                   jax.ShapeDtypeStruct((n, LANES), F32)),
        grid=(n // tm, P_COLS // INPROJ_TN),
        in_specs=[
            pl.BlockSpec((tm, D_MODEL), lambda i, j: (i, 0)),
            pl.BlockSpec((1, D_MODEL), lambda i, j: (0, 0)),
            pl.BlockSpec((D_MODEL, INPROJ_TN), lambda i, j: (0, j)),
            pl.BlockSpec((D_MODEL, LANES), lambda i, j: (0, 0)),
        ],
        out_specs=(
            pl.BlockSpec((tm, INPROJ_TN), lambda i, j: (i, j)),
            pl.BlockSpec((tm, LANES), lambda i, j: (i, 0)),
        ),
        scratch_shapes=[pltpu.VMEM((tm, D_MODEL), BF16)],
        compiler_params=_cparams(("arbitrary", "arbitrary")),
        name="inproj",
    )(x2d, g, w_main, w_a)


def _swa_kernel(slopes_ref, sinks_ref, q_ref, kp_ref, kc_ref, vp_ref, vc_ref, o_ref):
    n = pl.program_id(1)
    w = SWA_WINDOW
    kband = jnp.concatenate([kp_ref[...], kc_ref[...]], axis=0)
    vband = jnp.concatenate([vp_ref[...], vc_ref[...]], axis=0)
    vband_t = vband.astype(F32).T.astype(BF16)
    key = lax.broadcasted_iota(I32, (2 * w, w), 0)
    qry = lax.broadcasted_iota(I32, (2 * w, w), 1)
    dist = qry - key + w
    valid = (dist >= 0) & (dist < w) & ((key >= w) | (n > 0))
    negd = jnp.where(valid, -dist.astype(F32), MASK_NEG)
    q = q_ref[...] * jnp.asarray(SWA_HEAD_DIM ** -0.5, BF16)
    outs = []
    for kv in range(SWA_KV_HEADS):
        k_h = kband[:, kv * SWA_HEAD_DIM:(kv + 1) * SWA_HEAD_DIM]
        vt_h = vband_t[kv * SWA_HEAD_DIM:(kv + 1) * SWA_HEAD_DIM, :]
        for g in range(SWA_GROUP):
            h = kv * SWA_GROUP + g
            q_h = q[:, h * SWA_HEAD_DIM:(h + 1) * SWA_HEAD_DIM]
            st = lax.dot_general(k_h, q_h, (((1,), (1,)), ((), ())),
                                 preferred_element_type=F32)
            st = st + slopes_ref[h] * negd
            sink = sinks_ref[h]
            m = jnp.maximum(jnp.max(st, axis=0, keepdims=True), sink)
            p = jnp.exp(st - m)
            den = jnp.sum(p, axis=0, keepdims=True) + jnp.exp(sink - m)
            ot = jnp.dot(vt_h, p.astype(BF16), preferred_element_type=F32)
            outs.append(ot / den)
    o_t = jnp.concatenate(outs, axis=0)
    o_ref[...] = o_t.T.astype(BF16)


def _swa(p_act, slopes, sinks, batch, seq):
    n = batch * seq
    w = SWA_WINDOW
    nb = seq // w
    kvw = SWA_KV_HEADS * SWA_HEAD_DIM

    def cur(col):
        return lambda b, i: (b * nb + i, col)

    def prev(col):
        return lambda b, i: (b * nb + jnp.maximum(i - 1, 0), col)

    smem = pl.BlockSpec(memory_space=pltpu.SMEM)
    return pl.pallas_call(
        _swa_kernel,
        out_shape=jax.ShapeDtypeStruct((n, D_MODEL), BF16),
        grid=(batch, nb),
        in_specs=[
            smem, smem,
            pl.BlockSpec((w, D_MODEL), cur(COL_QA // D_MODEL)),
            pl.BlockSpec((w, kvw), prev(COL_KA // kvw)),
            pl.BlockSpec((w, kvw), cur(COL_KA // kvw)),
            pl.BlockSpec((w, kvw), prev(COL_VA // kvw)),
            pl.BlockSpec((w, kvw), cur(COL_VA // kvw)),
        ],
        out_specs=pl.BlockSpec((w, D_MODEL), lambda b, i: (b * nb + i, 0)),
        compiler_params=_cparams(("arbitrary", "arbitrary")),
        name="swa",
    )(slopes, sinks, p_act, p_act, p_act, p_act, p_act)


def _gla_kernel(q_ref, k_ref, v_ref, a_ref, w2_ref, ba_ref, gn_ref, o_ref, st_scr):
    c = GLA_CHUNK
    nchunks = q_ref.shape[0] // c
    st_scr[...] = jnp.zeros_like(st_scr)
    row = lax.broadcasted_iota(I32, (c, c), 0)
    col = lax.broadcasted_iota(I32, (c, c), 1)
    causal = row >= col
    tri = causal.astype(F32)
    scale = jnp.asarray(GLA_DK ** -0.5, F32)

    def chunk(ci, carry):
        r0 = pl.multiple_of(ci * c, c)
        z = jnp.dot(a_ref[pl.ds(r0, c), :], w2_ref[...], preferred_element_type=F32,
                    precision=lax.Precision.HIGHEST) + ba_ref[...]
        log_a = (jnp.minimum(z, 0.0) - jnp.log1p(jnp.exp(-jnp.abs(z)))) * (1.0 / GLA_TAU)
        b = jnp.dot(tri, log_a, preferred_element_type=F32,
                    precision=lax.Precision.HIGHEST)
        b_last = b[c - 1:c, :]
        b_mid = b[c // 2 - 1:c // 2, :]
        qf = q_ref[pl.ds(r0, c), :].astype(F32) * scale
        kf = k_ref[pl.ds(r0, c), :].astype(F32)
        v = v_ref[pl.ds(r0, c), :]
        q_inter = (qf * jnp.exp(b)).astype(BF16)
        q_intra = (qf * jnp.exp(b - b_mid)).astype(BF16)
        k_intra = (kf * jnp.exp(b_mid - b)).astype(BF16)
        k_state = (kf * jnp.exp(b_last - b)).astype(BF16)
        st = st_scr[...]
        inter = lax.dot_general(q_inter, st.astype(BF16), (((1,), (1,)), ((), ())),
                                preferred_element_type=F32)
        att = lax.dot_general(q_intra, k_intra, (((1,), (1,)), ((), ())),
                              preferred_element_type=F32)
        att = jnp.where(causal, att, 0.0).astype(BF16)
        intra = jnp.dot(att, v, preferred_element_type=F32)
        upd = lax.dot_general(v, k_state, (((0,), (0,)), ((), ())),
                              preferred_element_type=F32)
        st_scr[...] = st * jnp.exp(b_last) + upd
        o = inter + intra
        ms = jnp.mean(o * o, axis=-1, keepdims=True)
        o_ref[pl.ds(r0, c), :] = (o * lax.rsqrt(ms + NORM_EPS) * gn_ref[...]).astype(BF16)
        return carry

    lax.fori_loop(0, nchunks, chunk, 0)


def _gla(p_act, a_lr, w2_pad, b_alpha, gla_norm_g, batch, seq):
    n = batch * seq
    return pl.pallas_call(
        _gla_kernel,
        out_shape=jax.ShapeDtypeStruct((n, GLA_HEADS * GLA_DV), BF16),
        grid=(batch, GLA_HEADS),
        in_specs=[
            pl.BlockSpec((seq, GLA_DK), lambda b, h: (b, COL_QB // GLA_DK + h)),
            pl.BlockSpec((seq, GLA_DK), lambda b, h: (b, COL_KB // GLA_DK + h)),
            pl.BlockSpec((seq, GLA_DV), lambda b, h: (b, COL_VB // GLA_DV + h)),
            pl.BlockSpec((seq, LANES), lambda b, h: (b, 0)),
            pl.BlockSpec((LANES, GLA_DK), lambda b, h: (0, h)),
            pl.BlockSpec((1, GLA_DK), lambda b, h: (0, h)),
            pl.BlockSpec((1, GLA_DV), lambda b, h: (0, 0)),
        ],
        out_specs=pl.BlockSpec((seq, GLA_DV), lambda b, h: (b, h)),
        scratch_shapes=[pltpu.VMEM((GLA_DV, GLA_DK), F32)],
        compiler_params=_cparams(("arbitrary", "arbitrary")),
        name="gla",
    )(p_act, p_act, p_act, a_lr, w2_pad, b_alpha, gla_norm_g)


def _merge_kernel(ga_ref, gb_ref, rb_ref, oa_ref, ob_ref, x_ref, wout_ref, gffn_ref, wr_ref, br_ref,
                  x2_ref, xs_ref, slot_ref, gate_ref, cnt_ref):
    tm = x_ref.shape[0]
    slots = xs_ref.shape[0] // PAIR
    ga = ga_ref[...].astype(F32)
    gb = gb_ref[...].astype(F32)
    rb = rb_ref[...].astype(F32)
    ob = ob_ref[...].astype(F32) * (rb * _sigmoid(rb))
    mix = _sigmoid(ga) * oa_ref[...].astype(F32) + _sigmoid(gb) * ob
    x2 = x_ref[...] + jnp.dot(mix.astype(BF16), wout_ref[...], preferred_element_type=F32)
    x2_ref[...] = x2
    ms = jnp.mean(x2 * x2, axis=-1, keepdims=True)
    h2 = x2 * lax.rsqrt(ms + NORM_EPS) * gffn_ref[...]

    logits = jnp.dot(h2, wr_ref[...], preferred_element_type=F32,
                     precision=lax.Precision.HIGHEST) + br_ref[...]
    lane = lax.broadcasted_iota(I32, (tm, LANES), 1)
    neg_inf = jnp.asarray(-jnp.inf, F32)
    work = jnp.where(lane < N_EXPERTS, logits, neg_inf)
    vals, idxs = [], []
    for _ in range(TOP_K):
        m = jnp.max(work, axis=-1, keepdims=True)
        idx = jnp.min(jnp.where(work == m, lane, LANES), axis=-1, keepdims=True)
        vals.append(m)
        idxs.append(idx)
        work = jnp.where(lane == idx, neg_inf, work)
    exps = [jnp.exp(v - vals[0]) for v in vals]
    denom = exps[0] + exps[1] + exps[2] + exps[3]

    r_i = lax.broadcasted_iota(I32, (tm, tm), 0)
    c_i = lax.broadcasted_iota(I32, (tm, tm), 1)
    lower = jnp.where(r_i > c_i, 1.0, 0.0).astype(BF16)
    base = jnp.zeros((1, LANES), F32)
    onehots, ranks = [], []
    for k in range(TOP_K):
        onehot = lane == idxs[k]
        before = jnp.dot(lower, jnp.where(onehot, 1.0, 0.0).astype(BF16), preferred_element_type=F32)
        onehots.append(onehot)
        ranks.append(before + base)
        base = base + jnp.sum(jnp.where(onehot, 1.0, 0.0), axis=0, keepdims=True)
    cnt8 = jnp.floor((base + (SUBLANES - 1)) * (1.0 / SUBLANES))
    e_i = lax.broadcasted_iota(I32, (LANES, LANES), 0)
    e_j = lax.broadcasted_iota(I32, (LANES, LANES), 1)
    upper = jnp.where(e_i < e_j, 1.0, 0.0).astype(BF16)
    seg8 = jnp.dot(jnp.broadcast_to(cnt8, (SUBLANES, LANES)).astype(BF16), upper,
                   preferred_element_type=F32)[0:1, :]
    seg_row = seg8 * float(SUBLANES)
    slot_out = jnp.full((tm, LANES), -1.0, F32)
    gate_out = jnp.zeros((tm, LANES), F32)
    for k in range(TOP_K):
        slot_k = jnp.sum(jnp.where(onehots[k], ranks[k] + seg_row, 0.0), axis=-1, keepdims=True)
        slot_out = jnp.where(lane == k, slot_k, slot_out)
        gate_out = jnp.where(lane == k, exps[k] / denom, gate_out)
    slot_ref[...] = slot_out.astype(I32)
    gate_ref[...] = gate_out
    cnt_ref[...] = jnp.broadcast_to(base, (SUBLANES, LANES))

    slot_t = slot_out.T
    p_i = lax.broadcasted_iota(I32, (slots, tm), 0).astype(F32)
    perm = jnp.where(p_i == slot_t[0:1, :], 1.0, 0.0)
    for k in range(1, TOP_K):
        perm = perm + jnp.where(p_i == slot_t[k:k + 1, :], 1.0, 0.0)
    perm = perm.astype(BF16)
    h2b = h2.astype(BF16)
    cw = 256
    for c in range(D_HALF // cw):
        lo = jnp.dot(perm, h2b[:, c * cw:(c + 1) * cw], preferred_element_type=F32)
        hi = jnp.dot(perm, h2b[:, D_HALF + c * cw:D_HALF + (c + 1) * cw], preferred_element_type=F32)
        xs_ref[:, c * cw:(c + 1) * cw] = _pack_rows(lo, hi)


def _merge(p_act, o_a, o_b, x2d, w_out, g_ffn, w_router, b_router):
    n = x2d.shape[0]
    tm = MERGE_TM
    nt = n // tm
    row = lambda c: pl.BlockSpec((tm, D_MODEL), lambda i: (i, c))
    const = lambda shape: pl.BlockSpec(shape, lambda i: (0, 0))
    aux = pl.BlockSpec((tm, LANES), lambda i: (i, 0))
    return pl.pallas_call(
        _merge_kernel,
        out_shape=(
            jax.ShapeDtypeStruct((n, D_MODEL), F32),
            jax.ShapeDtypeStruct((nt * PAIR * MERGE_SLOTS, D_HALF), BF16),
            jax.ShapeDtypeStruct((n, LANES), I32),
            jax.ShapeDtypeStruct((n, LANES), F32),
            jax.ShapeDtypeStruct((nt * SUBLANES, LANES), F32),
        ),
        grid=(nt,),
        in_specs=[
            row(COL_GA // D_MODEL), row(COL_GB // D_MODEL), row(COL_RB // D_MODEL),
            row(0), row(0), row(0),
            pl.BlockSpec((D_MODEL, D_MODEL), lambda i: (0, 0), pipeline_mode=pl.Buffered(1)),
            const((1, D_MODEL)),
            pl.BlockSpec((D_MODEL, LANES), lambda i: (0, 0), pipeline_mode=pl.Buffered(1)),
            const((1, LANES)),
        ],
        out_specs=(row(0), pl.BlockSpec((PAIR * MERGE_SLOTS, D_HALF), lambda i: (i, 0)), aux, aux,
                   pl.BlockSpec((SUBLANES, LANES), lambda i: (i, 0))),
        compiler_params=_cparams(("arbitrary",)),
        name="merge",
    )(p_act, p_act, p_act, o_a, o_b, x2d, w_out, g_ffn, w_router, b_router)


def _experts_kernel(nitems_ref, ie_ref, it0_ref, it1_ref, ibase_ref, irows_ref, c8_ref, src8_ref, cum8_ref,
                    xs_ref, zeros_ref, wg_ref, wl_ref, bg_ref, bl_ref, wdl_ref, wdh_ref, bdl_ref, bdh_ref,
                    ys_ref, bufs, act_scr, sem_in, sem_out):
    del zeros_ref
    w = pl.program_id(0)
    s = pl.program_id(1)
    nfa = D_FF // EXPERT_TF
    nfb = D_HALF // EXPERT_TN
    n_items = nitems_ref[0]
    valid = w < n_items
    cur = lax.rem(w, 2)
    unit = PAIR * SUBLANES

    def for_each_copy(item, slot, outbound, fn):
        e = ie_ref[item]
        base = ibase_ref[item]
        buf = bufs.at[slot]

        def tile_body(t, carry):
            seg = t * N_EXPERTS + e
            c = c8_ref[seg]
            hbm0 = src8_ref[seg] * unit
            buf0 = (cum8_ref[seg] - base) * unit
            off = jnp.zeros((), I32)
            for bit in reversed(range(SEG_BITS)):
                rows = unit << bit
                flag = lax.bitwise_and(lax.shift_right_logical(c, bit), 1)

                @pl.when(flag == 1)
                def _(rows=rows, off=off):
                    in_buf = buf.at[pl.ds(pl.multiple_of(buf0 + off, unit), rows), :]
                    if outbound:
                        fn(pltpu.make_async_copy(
                            in_buf, ys_ref.at[pl.ds(pl.multiple_of(hbm0 + off, unit), rows), :], sem_out))
                    else:
                        fn(pltpu.make_async_copy(
                            xs_ref.at[pl.ds(pl.multiple_of(hbm0 + off, unit), rows), :], in_buf, sem_in))

                off = off + flag * rows
            return carry

        lax.fori_loop(it0_ref[item], it1_ref[item], tile_body, 0)

    start = lambda cp: cp.start()
    wait = lambda cp: cp.wait()

    @pl.when((w == 0) & (s == 0))
    def _():
        bufs[...] = jnp.zeros_like(bufs)
        act_scr[...] = jnp.zeros_like(act_scr)
        for_each_copy(0, 0, False, start)

    @pl.when(valid & (s == 0))
    def _():
        for_each_copy(w, cur, False, wait)

    n_sub = lax.div(irows_ref[w] * SUBLANES + (EXPERT_SUB - 1), EXPERT_SUB)

    @pl.when(valid & (s < nfa))
    def _():
        wg = wg_ref[...].astype(BF16)
        wl = wl_ref[...].astype(BF16)
        bg = bg_ref[...]
        bl = bl_ref[...]
        tile = jnp.minimum(s, nfa - 1)

        def sub_body(r, carry):
            z = bufs[cur, pl.ds(pl.multiple_of(r * (PAIR * EXPERT_SUB), PAIR * EXPERT_SUB),
                                PAIR * EXPERT_SUB), :]
            lo, hi = _unpack_rows(z)
            lo = lo.astype(BF16)
            hi = hi.astype(BF16)
            glu = (jnp.dot(lo, wg[0:D_HALF], preferred_element_type=F32)
                   + jnp.dot(hi, wg[D_HALF:D_MODEL], preferred_element_type=F32) + bg)
            lin = (jnp.dot(lo, wl[0:D_HALF], preferred_element_type=F32)
                   + jnp.dot(hi, wl[D_HALF:D_MODEL], preferred_element_type=F32) + bl)
            glu = jnp.minimum(glu, SWIGLU_LIMIT)
            lin = jnp.clip(lin, -SWIGLU_LIMIT, SWIGLU_LIMIT)
            act = glu * _sigmoid(SWIGLU_ALPHA * glu) * (lin + 1.0)
            act_scr[tile, pl.ds(pl.multiple_of(r * EXPERT_SUB, EXPERT_SUB), EXPERT_SUB), :] = act.astype(BF16)
            return carry

        lax.fori_loop(0, n_sub, sub_body, 0)

    @pl.when(valid & (s == nfa))
    def _():
        @pl.when(w >= 1)
        def _():
            for_each_copy(w - 1, 1 - cur, True, wait)

        @pl.when(w + 1 < n_items)
        def _():
            for_each_copy(w + 1, 1 - cur, False, start)

    @pl.when(valid & (s >= nfa))
    def _():
        wd = jnp.concatenate([wdl_ref[...], wdh_ref[...]], axis=1).astype(BF16)
        bd = jnp.concatenate([bdl_ref[...], bdh_ref[...]], axis=1)
        col = pl.multiple_of(jnp.maximum(s - nfa, 0) * EXPERT_TN, EXPERT_TN)

        def sub_body(r, carry):
            rows = pl.ds(pl.multiple_of(r * EXPERT_SUB, EXPERT_SUB), EXPERT_SUB)
            acc = jnp.dot(act_scr[0, rows, :], wd[0:EXPERT_TF], preferred_element_type=F32)
            for t in range(1, nfa):
                acc += jnp.dot(act_scr[t, rows, :], wd[t * EXPERT_TF:(t + 1) * EXPERT_TF],
                               preferred_element_type=F32)
            acc = acc + bd
            bufs[cur, pl.ds(pl.multiple_of(r * (PAIR * EXPERT_SUB), PAIR * EXPERT_SUB), PAIR * EXPERT_SUB),
                 pl.ds(col, EXPERT_TN)] = _pack_rows(acc[:, 0:EXPERT_TN], acc[:, EXPERT_TN:2 * EXPERT_TN])
            return carry

        lax.fori_loop(0, n_sub, sub_body, 0)

    @pl.when(valid & (s == nfa + nfb - 1))
    def _():
        for_each_copy(w, cur, True, start)

        @pl.when(w == n_items - 1)
        def _():
            for_each_copy(w, cur, True, wait)


def _experts(x_sorted, meta, n_items_max, w_gu, b_gu, w_down, b_down):
    nfa = D_FF // EXPERT_TF
    nfb = D_HALF // EXPERT_TN
    n_pre = len(meta)

    def fa(w, s, n_items):
        return jnp.where(w < n_items[0], jnp.minimum(s, nfa - 1), nfa - 1)

    def fb(w, s, n_items):
        return jnp.where(w < n_items[0], jnp.maximum(s - nfa, 0), nfb - 1)

    def wspec(shape, col):
        return pl.BlockSpec(shape, lambda w, s, *m: (m[1][w], 0, col(w, s, m[0])))

    any_spec = pl.BlockSpec(memory_space=pl.ANY)
    grid_spec = pltpu.PrefetchScalarGridSpec(
        num_scalar_prefetch=n_pre,
        grid=(n_items_max, nfa + nfb),
        in_specs=[
            any_spec, any_spec,
            wspec((None, D_MODEL, EXPERT_TF), fa),
            wspec((None, D_MODEL, EXPERT_TF), lambda w, s, n: nfa + fa(w, s, n)),
            wspec((None, 1, EXPERT_TF), fa),
            wspec((None, 1, EXPERT_TF), lambda w, s, n: nfa + fa(w, s, n)),
            wspec((None, D_FF, EXPERT_TN), fb),
            wspec((None, D_FF, EXPERT_TN), lambda w, s, n: nfb + fb(w, s, n)),
            wspec((None, 1, EXPERT_TN), fb),
            wspec((None, 1, EXPERT_TN), lambda w, s, n: nfb + fb(w, s, n)),
        ],
        out_specs=any_spec,
        scratch_shapes=[pltpu.VMEM((2, PAIR * EXPERT_ROWS, D_HALF), BF16),
                        pltpu.VMEM((nfa, EXPERT_ROWS, EXPERT_TF), BF16),
                        pltpu.SemaphoreType.DMA, pltpu.SemaphoreType.DMA],
    )
    zeros = jnp.zeros(x_sorted.shape, x_sorted.dtype)
    return pl.pallas_call(
        _experts_kernel,
        out_shape=jax.ShapeDtypeStruct(x_sorted.shape, x_sorted.dtype),
        grid_spec=grid_spec,
        input_output_aliases={n_pre + 1: 0},
        compiler_params=_cparams(("arbitrary", "arbitrary")),
        name="experts",
    )(*meta, x_sorted, zeros, w_gu, w_gu, b_gu, b_gu, w_down, w_down, b_down, b_down)


def _combine_kernel(ys_ref, slot_ref, gate_ref, x2_ref, gfin_ref, o_ref):
    tm = x2_ref.shape[0]
    slots = ys_ref.shape[0] // PAIR
    slot = slot_ref[...]
    gates = gate_ref[...]
    p_j = lax.broadcasted_iota(I32, (tm, slots), 1)
    gmat = jnp.where(p_j == slot[:, 0:1], gates[:, 0:1], 0.0)
    for k in range(1, TOP_K):
        gmat = gmat + jnp.where(p_j == slot[:, k:k + 1], gates[:, k:k + 1], 0.0)
    gmat = gmat.astype(BF16)
    y_lo, y_hi = _unpack_rows(ys_ref[...])
    moe_lo = jnp.dot(gmat, y_lo.astype(BF16), preferred_element_type=F32)
    moe_hi = jnp.dot(gmat, y_hi.astype(BF16), preferred_element_type=F32)
    x = x2_ref[...] + jnp.concatenate([moe_lo, moe_hi], axis=1)
    ms = jnp.mean(x * x, axis=-1, keepdims=True)
    o_ref[...] = x * lax.rsqrt(ms + NORM_EPS) * gfin_ref[...]


def _combine(y_sorted, slots, gates, x2, g_final):
    n = x2.shape[0]
    tm = MERGE_TM
    aux = pl.BlockSpec((tm, LANES), lambda i: (i, 0))
    return pl.pallas_call(
        _combine_kernel,
        out_shape=jax.ShapeDtypeStruct((n, D_MODEL), F32),
        grid=(n // tm,),
        in_specs=[
            pl.BlockSpec((PAIR * MERGE_SLOTS, D_HALF), lambda i: (i, 0)),
            aux, aux,
            pl.BlockSpec((tm, D_MODEL), lambda i: (i, 0)),
            pl.BlockSpec((1, D_MODEL), lambda i: (0, 0)),
        ],
        out_specs=pl.BlockSpec((tm, D_MODEL), lambda i: (i, 0)),
        compiler_params=_cparams(("arbitrary",)),
        name="combine",
    )(y_sorted, slots, gates, x2, g_final)


def _layer(x2d, batch, seq, attn_norm_g, w_in, swa_sinks, w_alpha2, b_alpha, gla_norm_g, w_out,
           ffn_norm_g, w_router, b_router, w_gate_up, b_gate_up, w_down, b_down, final_norm_g):
    n = x2d.shape[0]
    assert n % MERGE_TM == 0 and seq % SWA_WINDOW == 0 and seq % GLA_CHUNK == 0
    o_qa, o_ka, o_va = 0, 2048, 2304
    o_qb, o_kb, o_vb, o_rb, o_al, o_ga, o_gb = 2560, 3584, 4608, 6656, 8704, 8720, 10768
    seg = lambda s, w: w_in[:, s:s + w]
    w_main = jnp.concatenate([
        seg(o_qa, 2048), seg(o_ga, 2048), seg(o_gb, 2048), seg(o_rb, 2048), seg(o_vb, 2048),
        seg(o_qb, 1024), seg(o_kb, 1024), seg(o_ka, 256), seg(o_va, 256)], axis=1).astype(BF16)
    w_a = jnp.pad(seg(o_al, GLA_GATE_RANK), ((0, 0), (0, LANES - GLA_GATE_RANK))).astype(BF16)
    w2_pad = jnp.pad(w_alpha2, ((0, LANES - GLA_GATE_RANK), (0, 0)))
    w_r = jnp.pad(w_router, ((0, 0), (0, LANES - N_EXPERTS)))
    b_r = jnp.pad(b_router, (0, LANES - N_EXPERTS)).reshape(1, LANES)
    slopes = jnp.exp2(-8.0 * jnp.arange(1, SWA_HEADS + 1, dtype=F32) / SWA_HEADS)

    p_act, a_lr = _inproj(x2d, attn_norm_g.reshape(1, D_MODEL), w_main, w_a)
    o_a = _swa(p_act, slopes, swa_sinks.astype(F32), batch, seq)
    o_b = _gla(p_act, a_lr, w2_pad, b_alpha.reshape(1, -1), gla_norm_g.reshape(1, -1), batch, seq)
    x2, x_sorted, slots, gates, cnt = _merge(
        p_act, o_a, o_b, x2d, w_out.astype(BF16), ffn_norm_g.reshape(1, D_MODEL), w_r, b_r)

    nt = n // MERGE_TM
    chunk8 = EXPERT_CHUNK // SUBLANES
    cnt_t = cnt[::SUBLANES, :N_EXPERTS].astype(I32)
    c8 = (cnt_t + SUBLANES - 1) // SUBLANES
    tile8 = (jnp.arange(nt, dtype=I32) * (MERGE_SLOTS // SUBLANES))[:, None]
    src8 = tile8 + jnp.cumsum(c8, axis=1) - c8
    cum8 = jnp.cumsum(c8, axis=0) - c8
    tot8 = jnp.sum(c8, axis=0)
    per_e = (tot8 + chunk8 - 1) // chunk8
    item_end = jnp.cumsum(per_e)
    n_items = item_end[-1]
    n_items_max = -(-(n * TOP_K // SUBLANES + nt * N_EXPERTS) // chunk8) + N_EXPERTS
    w_idx = jnp.arange(n_items_max, dtype=I32)
    w_eff = jnp.minimum(w_idx, n_items - 1)
    item_e = jnp.minimum(jnp.sum(item_end[None, :] <= w_eff[:, None], axis=1), N_EXPERTS - 1).astype(I32)
    item_j = w_eff - (item_end - per_e)[item_e]
    cum_e = cum8[:, item_e]
    item_t0 = jnp.sum(cum_e < (item_j * chunk8)[None, :], axis=0).astype(I32)
    item_t1 = jnp.sum(cum_e < ((item_j + 1) * chunk8)[None, :], axis=0).astype(I32)
    t0c = jnp.minimum(item_t0, nt - 1)
    t1c = jnp.maximum(item_t1 - 1, 0)
    item_base = cum8[t0c, item_e]
    item_rows = cum8[t1c, item_e] + c8[t1c, item_e] - item_base
    used = w_idx < n_items
    item_t1 = jnp.where(used, item_t1, item_t0)
    item_rows = jnp.where(used & (item_t1 > item_t0), item_rows, 0).astype(I32)
    meta = (n_items.astype(I32).reshape(1), item_e, item_t0, item_t1, item_base.astype(I32), item_rows,
            c8.reshape(-1), src8.reshape(-1), cum8.reshape(-1))

    y_sorted = _experts(x_sorted, meta, n_items_max, w_gate_up,
                        b_gate_up.reshape(N_EXPERTS, 1, 2 * D_FF), w_down,
                        b_down.reshape(N_EXPERTS, 1, D_MODEL))
    return _combine(y_sorted, slots, gates, x2, final_norm_g.reshape(1, D_MODEL))


def kernel(x, attn_norm_g, w_in, swa_sinks, w_alpha2, b_alpha, gla_norm_g, w_out, ffn_norm_g,
           w_router, b_router, w_gate_up, b_gate_up, w_down, b_down, final_norm_g):
    batch, seq, _ = x.shape
    assert w_in.shape[0] == 1, "single-layer block: the final norm is fused into the MoE combine"
    out = _layer(
        x.reshape(batch * seq, D_MODEL), batch, seq, attn_norm_g[0], w_in[0], swa_sinks[0],
        w_alpha2[0], b_alpha[0], gla_norm_g[0], w_out[0], ffn_norm_g[0], w_router[0], b_router[0],
        w_gate_up[0], b_gate_up[0], w_down[0], b_down[0], final_norm_g)
    return out.reshape(batch, seq, D_MODEL)
```

```python
import functools

import jax
import jax.numpy as jnp
from jax import lax
from jax.experimental import pallas as pl
from jax.experimental.pallas import tpu as pltpu

F32 = jnp.float32
BF16 = jnp.bfloat16
I32 = jnp.int32
U32 = jnp.uint32

D_MODEL = 2048
D_HALF = D_MODEL // 2
NORM_EPS = 1e-5
SWA_HEADS = 32
SWA_KV_HEADS = 4
SWA_HEAD_DIM = 64
SWA_WINDOW = 128
SWA_GROUP = SWA_HEADS // SWA_KV_HEADS
GLA_HEADS = 4
GLA_DK = 256
GLA_DV = 512
GLA_GATE_RANK = 16
GLA_TAU = 16.0
GLA_CHUNK = 64
GLA_GROUP = 256
N_EXPERTS = 32
TOP_K = 4
D_FF = 2048
SWIGLU_LIMIT = 7.0
SWIGLU_ALPHA = 1.702

LANES = 128
SUBLANES = 8
MASK_NEG = -1e30

COL_QA = 0
COL_GA = 2048
COL_GB = 4096
COL_RB = 6144
COL_VB = 8192
COL_QB = 10240
COL_KB = 11264
COL_KA = 12288
COL_VA = 12544
P_COLS = 12800

VMEM_LIMIT = 56 * 1024 * 1024

INPROJ_TM = 1024
INPROJ_TN = 512
MERGE_TM = 256
MERGE_SLOTS = MERGE_TM * TOP_K + N_EXPERTS * SUBLANES
SEG_BITS = (MERGE_TM // SUBLANES).bit_length()
PAIR = 2
EXPERT_CHUNK = 2560
EXPERT_ROWS = EXPERT_CHUNK + MERGE_TM
EXPERT_SUB = 256
EXPERT_TF = 256
EXPERT_TN = 128


def _cparams(sem):
    return pltpu.CompilerParams(dimension_semantics=sem, vmem_limit_bytes=VMEM_LIMIT)


def _sigmoid(x):
    return 1.0 / (1.0 + jnp.exp(-x))


def _split_bf16(x):
    hi = x.astype(BF16)
    return hi, (x - hi.astype(F32)).astype(BF16)


def _dot_split(a, b_hi, b_lo):
    a_hi, a_lo = _split_bf16(a)
    return ((jnp.dot(a_hi, b_hi, preferred_element_type=F32)
             + jnp.dot(a_lo, b_hi, preferred_element_type=F32))
            + jnp.dot(a_hi, b_lo, preferred_element_type=F32))


def _pack_rows(lo, hi):
    return pltpu.bitcast(pltpu.pack_elementwise([lo, hi], packed_dtype=BF16), BF16)


def _unpack_rows(z):
    words = pltpu.bitcast(z, U32)
    return (pltpu.unpack_elementwise(words, index=0, packed_dtype=BF16, unpacked_dtype=F32),
            pltpu.unpack_elementwise(words, index=1, packed_dtype=BF16, unpacked_dtype=F32))


def _inproj_kernel(x_ref, g_ref, w_ref, wa_ref, p_ref, a_ref, h_scr):
    @pl.when(pl.program_id(1) == 0)
    def _():
        x = x_ref[...]
        ms = jnp.mean(x * x, axis=-1, keepdims=True)
        h = (x * lax.rsqrt(ms + NORM_EPS) * g_ref[...]).astype(BF16)
        h_scr[...] = h
        a_ref[...] = jnp.dot(h, wa_ref[...], preferred_element_type=F32)

    p_ref[...] = jnp.dot(h_scr[...], w_ref[...], preferred_element_type=F32).astype(BF16)


def _inproj(x2d, g, w_main, w_a):
    n = x2d.shape[0]
    tm = min(INPROJ_TM, n)
    return pl.pallas_call(
        _inproj_kernel,
        out_shape=(jax.ShapeDtypeStruct((n, P_COLS), BF16),
                   jax.ShapeDtypeStruct((n, LANES), F32)),
        grid=(n // tm, P_COLS // INPROJ_TN),
        in_specs=[
            pl.BlockSpec((tm, D_MODEL), lambda i, j: (i, 0)),
            pl.BlockSpec((1, D_MODEL), lambda i, j: (0, 0)),
            pl.BlockSpec((D_MODEL, INPROJ_TN), lambda i, j: (0, j)),
            pl.BlockSpec((D_MODEL, LANES), lambda i, j: (0, 0)),
        ],
        out_specs=(
            pl.BlockSpec((tm, INPROJ_TN), lambda i, j: (i, j)),
            pl.BlockSpec((tm, LANES), lambda i, j: (i, 0)),
        ),
        scratch_shapes=[pltpu.VMEM((tm, D_MODEL), BF16)],
        compiler_params=_cparams(("arbitrary", "arbitrary")),
        name="inproj",
    )(x2d, g, w_main, w_a)


def _swa_kernel(slopes_ref, sinks_ref, q_ref, kp_ref, kc_ref, vp_ref, vc_ref, o_ref):
    n = pl.program_id(1)
    w = SWA_WINDOW
    kband = jnp.concatenate([kp_ref[...], kc_ref[...]], axis=0)
    vband = jnp.concatenate([vp_ref[...], vc_ref[...]], axis=0)
    vband_t = vband.astype(F32).T.astype(BF16)
    key = lax.broadcasted_iota(I32, (2 * w, w), 0)
    qry = lax.broadcasted_iota(I32, (2 * w, w), 1)
    dist = qry - key + w
    valid = (dist >= 0) & (dist < w) & ((key >= w) | (n > 0))
    negd = jnp.where(valid, -dist.astype(F32), MASK_NEG)
    q = q_ref[...] * jnp.asarray(SWA_HEAD_DIM ** -0.5, BF16)
    outs = []
    for kv in range(SWA_KV_HEADS):
        k_h = kband[:, kv * SWA_HEAD_DIM:(kv + 1) * SWA_HEAD_DIM]
        vt_h = vband_t[kv * SWA_HEAD_DIM:(kv + 1) * SWA_HEAD_DIM, :]
        for g in range(SWA_GROUP):
            h = kv * SWA_GROUP + g
            q_h = q[:, h * SWA_HEAD_DIM:(h + 1) * SWA_HEAD_DIM]
            st = lax.dot_general(k_h, q_h, (((1,), (1,)), ((), ())),
                                 preferred_element_type=F32)
            st = st + slopes_ref[h] * negd
            sink = sinks_ref[h]
            m = jnp.maximum(jnp.max(st, axis=0, keepdims=True), sink)
            p = jnp.exp(st - m)
            den = jnp.sum(p, axis=0, keepdims=True) + jnp.exp(sink - m)
            ot = jnp.dot(vt_h, p.astype(BF16), preferred_element_type=F32)
            outs.append(ot / den)
    o_t = jnp.concatenate(outs, axis=0)
    o_ref[...] = o_t.T.astype(BF16)


def _swa(p_act, slopes, sinks, batch, seq):
    n = batch * seq
    w = SWA_WINDOW
    nb = seq // w
    kvw = SWA_KV_HEADS * SWA_HEAD_DIM

    def cur(col):
        return lambda b, i: (b * nb + i, col)

    def prev(col):
        return lambda b, i: (b * nb + jnp.maximum(i - 1, 0), col)

    smem = pl.BlockSpec(memory_space=pltpu.SMEM)
    return pl.pallas_call(
        _swa_kernel,
        out_shape=jax.ShapeDtypeStruct((n, D_MODEL), BF16),
        grid=(batch, nb),
        in_specs=[
            smem, smem,
            pl.BlockSpec((w, D_MODEL), cur(COL_QA // D_MODEL)),
            pl.BlockSpec((w, kvw), prev(COL_KA // kvw)),
            pl.BlockSpec((w, kvw), cur(COL_KA // kvw)),
            pl.BlockSpec((w, kvw), prev(COL_VA // kvw)),
            pl.BlockSpec((w, kvw), cur(COL_VA // kvw)),
        ],
        out_specs=pl.BlockSpec((w, D_MODEL), lambda b, i: (b * nb + i, 0)),
        compiler_params=_cparams(("arbitrary", "arbitrary")),
        name="swa",
    )(slopes, sinks, p_act, p_act, p_act, p_act, p_act)


def _gla_kernel(q_ref, k_ref, v_ref, a_ref, w2h_ref, w2l_ref, ba_ref, gn_ref, o_ref,
                st_scr, qi_scr, ks_scr, dec_scr, intra_scr):
    c = GLA_CHUNK
    grp = GLA_GROUP
    cpg = grp // c
    seq = q_ref.shape[0]
    row = lax.broadcasted_iota(I32, (grp, grp), 0)
    col = lax.broadcasted_iota(I32, (grp, grp), 1)
    tri = jnp.where((row >= col) & (row // c == col // c), 1.0, 0.0).astype(BF16)
    causal = lax.broadcasted_iota(I32, (c, c), 0) >= lax.broadcasted_iota(I32, (c, c), 1)
    scale = jnp.asarray(GLA_DK ** -0.5, F32)

    def group(gi, carry):
        g0 = pl.multiple_of(gi * grp, grp)
        z = _dot_split(a_ref[pl.ds(g0, grp), :], w2h_ref[...], w2l_ref[...]) + ba_ref[...]
        log_a = (jnp.minimum(z, 0.0) - jnp.log1p(jnp.exp(-jnp.abs(z)))) * (1.0 / GLA_TAU)
        la_hi, la_mid = _split_bf16(log_a)
        la_lo = ((log_a - la_hi.astype(F32)) - la_mid.astype(F32)).astype(BF16)
        b_all = ((jnp.dot(tri, la_hi, preferred_element_type=F32)
                  + jnp.dot(tri, la_mid, preferred_element_type=F32))
                 + jnp.dot(tri, la_lo, preferred_element_type=F32))
        qf_all = q_ref[pl.ds(g0, grp), :].astype(F32) * scale
        kf_all = k_ref[pl.ds(g0, grp), :].astype(F32)
        for j in range(cpg):
            r0 = pl.multiple_of(g0 + j * c, c)
            b = b_all[j * c:(j + 1) * c, :]
            qf = qf_all[j * c:(j + 1) * c, :]
            kf = kf_all[j * c:(j + 1) * c, :]
            b_last = b[c - 1:c, :]
            b_mid = b[c // 2 - 1:c // 2, :]
            qi_scr[pl.ds(r0, c), :] = (qf * jnp.exp(b)).astype(BF16)
            ks_scr[pl.ds(r0, c), :] = (kf * jnp.exp(b_last - b)).astype(BF16)
            dec_scr[pl.ds(gi * cpg + j, 1), :] = jnp.exp(b_last)
            q_intra = (qf * jnp.exp(b - b_mid)).astype(BF16)
            k_intra = (kf * jnp.exp(b_mid - b)).astype(BF16)
            att = lax.dot_general(q_intra, k_intra, (((1,), (1,)), ((), ())),
                                  preferred_element_type=F32)
            att = jnp.where(causal, att, 0.0).astype(BF16)
            intra_scr[pl.ds(r0, c), :] = jnp.dot(att, v_ref[pl.ds(r0, c), :], preferred_element_type=F32)
        return carry

    lax.fori_loop(0, seq // grp, group, 0, unroll=2)

    st_scr[...] = jnp.zeros_like(st_scr)

    def chunk(ci, carry):
        r0 = pl.multiple_of(ci * c, c)
        st = st_scr[...]
        inter = lax.dot_general(qi_scr[pl.ds(r0, c), :], st.astype(BF16), (((1,), (1,)), ((), ())),
                                preferred_element_type=F32)
        upd = lax.dot_general(v_ref[pl.ds(r0, c), :], ks_scr[pl.ds(r0, c), :], (((0,), (0,)), ((), ())),
                              preferred_element_type=F32)
        st_scr[...] = st * dec_scr[pl.ds(ci, 1), :] + upd
        o = inter + intra_scr[pl.ds(r0, c), :]
        ms = jnp.mean(o * o, axis=-1, keepdims=True)
        o_ref[pl.ds(r0, c), :] = (o * lax.rsqrt(ms + NORM_EPS) * gn_ref[...]).astype(BF16)
        return carry

    lax.fori_loop(0, seq // c, chunk, 0, unroll=4)


def _gla(p_act, a_lr, w2_hi, w2_lo, b_alpha, gla_norm_g, batch, seq):
    n = batch * seq
    assert seq % GLA_GROUP == 0 and GLA_GROUP % GLA_CHUNK == 0
    return pl.pallas_call(
        _gla_kernel,
        out_shape=jax.ShapeDtypeStruct((n, GLA_HEADS * GLA_DV), BF16),
        grid=(batch, GLA_HEADS),
        in_specs=[
            pl.BlockSpec((seq, GLA_DK), lambda b, h: (b, COL_QB // GLA_DK + h)),
            pl.BlockSpec((seq, GLA_DK), lambda b, h: (b, COL_KB // GLA_DK + h)),
            pl.BlockSpec((seq, GLA_DV), lambda b, h: (b, COL_VB // GLA_DV + h)),
            pl.BlockSpec((seq, LANES), lambda b, h: (b, 0)),
            pl.BlockSpec((LANES, GLA_DK), lambda b, h: (0, h)),
            pl.BlockSpec((LANES, GLA_DK), lambda b, h: (0, h)),
            pl.BlockSpec((1, GLA_DK), lambda b, h: (0, h)),
            pl.BlockSpec((1, GLA_DV), lambda b, h: (0, 0)),
        ],
        out_specs=pl.BlockSpec((seq, GLA_DV), lambda b, h: (b, h)),
        scratch_shapes=[pltpu.VMEM((GLA_DV, GLA_DK), F32),
                        pltpu.VMEM((seq, GLA_DK), BF16),
                        pltpu.VMEM((seq, GLA_DK), BF16),
                        pltpu.VMEM((seq // GLA_CHUNK, GLA_DK), F32),
                        pltpu.VMEM((seq, GLA_DV), F32)],
        compiler_params=_cparams(("arbitrary", "arbitrary")),
        name="gla",
    )(p_act, p_act, p_act, a_lr, w2_hi, w2_lo, b_alpha, gla_norm_g)


def _merge_kernel(ga_ref, gb_ref, rb_ref, oa_ref, ob_ref, x_ref, wout_ref, gffn_ref, wrh_ref, wrl_ref, br_ref,
                  x2_ref, xs_ref, slot_ref, gate_ref, cnt_ref):
    tm = x_ref.shape[0]
    slots = xs_ref.shape[0] // PAIR
    ga = ga_ref[...].astype(F32)
    gb = gb_ref[...].astype(F32)
    rb = rb_ref[...].astype(F32)
    ob = ob_ref[...].astype(F32) * (rb * _sigmoid(rb))
    mix = _sigmoid(ga) * oa_ref[...].astype(F32) + _sigmoid(gb) * ob
    x2 = x_ref[...] + jnp.dot(mix.astype(BF16), wout_ref[...], preferred_element_type=F32)
    x2_ref[...] = x2
    ms = jnp.mean(x2 * x2, axis=-1, keepdims=True)
    h2 = x2 * lax.rsqrt(ms + NORM_EPS) * gffn_ref[...]

    logits = _dot_split(h2, wrh_ref[...], wrl_ref[...]) + br_ref[...]
    lane = lax.broadcasted_iota(I32, (tm, LANES), 1)
    neg_inf = jnp.asarray(-jnp.inf, F32)
    work = jnp.where(lane < N_EXPERTS, logits, neg_inf)
    vals, idxs = [], []
    for _ in range(TOP_K):
        m = jnp.max(work, axis=-1, keepdims=True)
        idx = jnp.min(jnp.where(work == m, lane, LANES), axis=-1, keepdims=True)
        vals.append(m)
        idxs.append(idx)
        work = jnp.where(lane == idx, neg_inf, work)
    exps = [jnp.exp(v - vals[0]) for v in vals]
    denom = exps[0] + exps[1] + exps[2] + exps[3]

    r_i = lax.broadcasted_iota(I32, (tm, tm), 0)
    c_i = lax.broadcasted_iota(I32, (tm, tm), 1)
    lower = jnp.where(r_i > c_i, 1.0, 0.0).astype(BF16)
    base = jnp.zeros((1, LANES), F32)
    onehots, ranks = [], []
    for k in range(TOP_K):
        onehot = lane == idxs[k]
        before = jnp.dot(lower, jnp.where(onehot, 1.0, 0.0).astype(BF16), preferred_element_type=F32)
        onehots.append(onehot)
        ranks.append(before + base)
        base = base + jnp.sum(jnp.where(onehot, 1.0, 0.0), axis=0, keepdims=True)
    cnt8 = jnp.floor((base + (SUBLANES - 1)) * (1.0 / SUBLANES))
    e_i = lax.broadcasted_iota(I32, (LANES, LANES), 0)
    e_j = lax.broadcasted_iota(I32, (LANES, LANES), 1)
    upper = jnp.where(e_i < e_j, 1.0, 0.0).astype(BF16)
    seg8 = jnp.dot(jnp.broadcast_to(cnt8, (SUBLANES, LANES)).astype(BF16), upper,
                   preferred_element_type=F32)[0:1, :]
    seg_row = seg8 * float(SUBLANES)
    slot_out = jnp.full((tm, LANES), -1.0, F32)
    gate_out = jnp.zeros((tm, LANES), F32)
    for k in range(TOP_K):
        slot_k = jnp.sum(jnp.where(onehots[k], ranks[k] + seg_row, 0.0), axis=-1, keepdims=True)
        slot_out = jnp.where(lane == k, slot_k, slot_out)
        gate_out = jnp.where(lane == k, exps[k] / denom, gate_out)
    slot_ref[...] = slot_out.astype(I32)
    gate_ref[...] = gate_out
    cnt_ref[...] = jnp.broadcast_to(base, (SUBLANES, LANES))

    slot_t = slot_out.T
    p_i = lax.broadcasted_iota(I32, (slots, tm), 0).astype(F32)
    perm = jnp.where(p_i == slot_t[0:1, :], 1.0, 0.0)
    for k in range(1, TOP_K):
        perm = perm + jnp.where(p_i == slot_t[k:k + 1, :], 1.0, 0.0)
    perm = perm.astype(BF16)
    h2b = h2.astype(BF16)
    cw = 256
    for c in range(D_HALF // cw):
        lo = jnp.dot(perm, h2b[:, c * cw:(c + 1) * cw], preferred_element_type=F32)
        hi = jnp.dot(perm, h2b[:, D_HALF + c * cw:D_HALF + (c + 1) * cw], preferred_element_type=F32)
        xs_ref[:, c * cw:(c + 1) * cw] = _pack_rows(lo, hi)


def _merge(p_act, o_a, o_b, x2d, w_out, g_ffn, w_router_hi, w_router_lo, b_router):
    n = x2d.shape[0]
    tm = MERGE_TM
    nt = n // tm
    row = lambda c: pl.BlockSpec((tm, D_MODEL), lambda i: (i, c))
    const = lambda shape: pl.BlockSpec(shape, lambda i: (0, 0))
    aux = pl.BlockSpec((tm, LANES), lambda i: (i, 0))
    return pl.pallas_call(
        _merge_kernel,
        out_shape=(
            jax.ShapeDtypeStruct((n, D_MODEL), F32),
            jax.ShapeDtypeStruct((nt * PAIR * MERGE_SLOTS, D_HALF), BF16),
            jax.ShapeDtypeStruct((n, LANES), I32),
            jax.ShapeDtypeStruct((n, LANES), F32),
            jax.ShapeDtypeStruct((nt * SUBLANES, LANES), F32),
        ),
        grid=(nt,),
        in_specs=[
            row(COL_GA // D_MODEL), row(COL_GB // D_MODEL), row(COL_RB // D_MODEL),
            row(0), row(0), row(0),
            pl.BlockSpec((D_MODEL, D_MODEL), lambda i: (0, 0), pipeline_mode=pl.Buffered(1)),
            const((1, D_MODEL)),
            pl.BlockSpec((D_MODEL, LANES), lambda i: (0, 0), pipeline_mode=pl.Buffered(1)),
            pl.BlockSpec((D_MODEL, LANES), lambda i: (0, 0), pipeline_mode=pl.Buffered(1)),
            const((1, LANES)),
        ],
        out_specs=(row(0), pl.BlockSpec((PAIR * MERGE_SLOTS, D_HALF), lambda i: (i, 0)), aux, aux,
                   pl.BlockSpec((SUBLANES, LANES), lambda i: (i, 0))),
        compiler_params=_cparams(("arbitrary",)),
        name="merge",
    )(p_act, p_act, p_act, o_a, o_b, x2d, w_out, g_ffn, w_router_hi, w_router_lo, b_router)


def _experts_kernel(nitems_ref, ie_ref, it0_ref, it1_ref, ibase_ref, irows_ref, c8_ref, src8_ref, cum8_ref,
                    xs_ref, zeros_ref, wg_ref, wl_ref, bg_ref, bl_ref, wdl_ref, wdh_ref, bdl_ref, bdh_ref,
                    ys_ref, bufs, act_scr, sem_in, sem_out):
    del zeros_ref
    w = pl.program_id(0)
    s = pl.program_id(1)
    nfa = D_FF // EXPERT_TF
    nfb = D_HALF // EXPERT_TN
    n_items = nitems_ref[0]
    valid = w < n_items
    cur = lax.rem(w, 2)
    unit = PAIR * SUBLANES

    def for_each_copy(item, slot, outbound, fn):
        e = ie_ref[item]
        base = ibase_ref[item]
        buf = bufs.at[slot]

        def tile_body(t, carry):
            seg = t * N_EXPERTS + e
            c = c8_ref[seg]
            hbm0 = src8_ref[seg] * unit
            buf0 = (cum8_ref[seg] - base) * unit
            off = jnp.zeros((), I32)
            for bit in reversed(range(SEG_BITS)):
                rows = unit << bit
                flag = lax.bitwise_and(lax.shift_right_logical(c, bit), 1)

                @pl.when(flag == 1)
                def _(rows=rows, off=off):
                    in_buf = buf.at[pl.ds(pl.multiple_of(buf0 + off, unit), rows), :]
                    if outbound:
                        fn(pltpu.make_async_copy(
                            in_buf, ys_ref.at[pl.ds(pl.multiple_of(hbm0 + off, unit), rows), :], sem_out))
                    else:
                        fn(pltpu.make_async_copy(
                            xs_ref.at[pl.ds(pl.multiple_of(hbm0 + off, unit), rows), :], in_buf, sem_in))

                off = off + flag * rows
            return carry

        lax.fori_loop(it0_ref[item], it1_ref[item], tile_body, 0)

    start = lambda cp: cp.start()
    wait = lambda cp: cp.wait()

    @pl.when((w == 0) & (s == 0))
    def _():
        bufs[...] = jnp.zeros_like(bufs)
        act_scr[...] = jnp.zeros_like(act_scr)
        for_each_copy(0, 0, False, start)

    @pl.when(valid & (s == 0))
    def _():
        for_each_copy(w, cur, False, wait)

    n_sub = lax.div(irows_ref[w] * SUBLANES + (EXPERT_SUB - 1), EXPERT_SUB)

    @pl.when(valid & (s < nfa))
    def _():
        wg = wg_ref[...].astype(BF16)
        wl = wl_ref[...].astype(BF16)
        bg = bg_ref[...]
        bl = bl_ref[...]
        tile = jnp.minimum(s, nfa - 1)

        def sub_body(r, carry):
            z = bufs[cur, pl.ds(pl.multiple_of(r * (PAIR * EXPERT_SUB), PAIR * EXPERT_SUB),
                                PAIR * EXPERT_SUB), :]
            lo, hi = _unpack_rows(z)
            lo = lo.astype(BF16)
            hi = hi.astype(BF16)
            glu = (jnp.dot(lo, wg[0:D_HALF], preferred_element_type=F32)
                   + jnp.dot(hi, wg[D_HALF:D_MODEL], preferred_element_type=F32) + bg)
            lin = (jnp.dot(lo, wl[0:D_HALF], preferred_element_type=F32)
                   + jnp.dot(hi, wl[D_HALF:D_MODEL], preferred_element_type=F32) + bl)
            glu = jnp.minimum(glu, SWIGLU_LIMIT)
            lin = jnp.clip(lin, -SWIGLU_LIMIT, SWIGLU_LIMIT)
            act = glu * _sigmoid(SWIGLU_ALPHA * glu) * (lin + 1.0)
            act_scr[tile, pl.ds(pl.multiple_of(r * EXPERT_SUB, EXPERT_SUB), EXPERT_SUB), :] = act.astype(BF16)
            return carry

        lax.fori_loop(0, n_sub, sub_body, 0)

    @pl.when(valid & (s == nfa))
    def _():
        @pl.when(w >= 1)
        def _():
            for_each_copy(w - 1, 1 - cur, True, wait)

        @pl.when(w + 1 < n_items)
        def _():
            for_each_copy(w + 1, 1 - cur, False, start)

    @pl.when(valid & (s >= nfa))
    def _():
        wd = jnp.concatenate([wdl_ref[...], wdh_ref[...]], axis=1).astype(BF16)
        bd = jnp.concatenate([bdl_ref[...], bdh_ref[...]], axis=1)
        col = pl.multiple_of(jnp.maximum(s - nfa, 0) * EXPERT_TN, EXPERT_TN)

        def sub_body(r, carry):
            rows = pl.ds(pl.multiple_of(r * EXPERT_SUB, EXPERT_SUB), EXPERT_SUB)
            acc = jnp.dot(act_scr[0, rows, :], wd[0:EXPERT_TF], preferred_element_type=F32)
            for t in range(1, nfa):
                acc += jnp.dot(act_scr[t, rows, :], wd[t * EXPERT_TF:(t + 1) * EXPERT_TF],
                               preferred_element_type=F32)
            acc = acc + bd
            bufs[cur, pl.ds(pl.multiple_of(r * (PAIR * EXPERT_SUB), PAIR * EXPERT_SUB), PAIR * EXPERT_SUB),
                 pl.ds(col, EXPERT_TN)] = _pack_rows(acc[:, 0:EXPERT_TN], acc[:, EXPERT_TN:2 * EXPERT_TN])
            return carry

        lax.fori_loop(0, n_sub, sub_body, 0)

    @pl.when(valid & (s == nfa + nfb - 1))
    def _():
        for_each_copy(w, cur, True, start)

        @pl.when(w == n_items - 1)
        def _():
            for_each_copy(w, cur, True, wait)


def _experts(x_sorted, meta, n_items_max, w_gu, b_gu, w_down, b_down):
    nfa = D_FF // EXPERT_TF
    nfb = D_HALF // EXPERT_TN
    n_pre = len(meta)

    def fa(w, s, n_items):
        return jnp.where(w < n_items[0], jnp.minimum(s, nfa - 1), nfa - 1)

    def fb(w, s, n_items):
        return jnp.where(w < n_items[0], jnp.maximum(s - nfa, 0), nfb - 1)

    def wspec(shape, col):
        return pl.BlockSpec(shape, lambda w, s, *m: (m[1][w], 0, col(w, s, m[0])))

    any_spec = pl.BlockSpec(memory_space=pl.ANY)
    grid_spec = pltpu.PrefetchScalarGridSpec(
        num_scalar_prefetch=n_pre,
        grid=(n_items_max, nfa + nfb),
        in_specs=[
            any_spec, any_spec,
            wspec((None, D_MODEL, EXPERT_TF), fa),
            wspec((None, D_MODEL, EXPERT_TF), lambda w, s, n: nfa + fa(w, s, n)),
            wspec((None, 1, EXPERT_TF), fa),
            wspec((None, 1, EXPERT_TF), lambda w, s, n: nfa + fa(w, s, n)),
            wspec((None, D_FF, EXPERT_TN), fb),
            wspec((None, D_FF, EXPERT_TN), lambda w, s, n: nfb + fb(w, s, n)),
            wspec((None, 1, EXPERT_TN), fb),
            wspec((None, 1, EXPERT_TN), lambda w, s, n: nfb + fb(w, s, n)),
        ],
        out_specs=any_spec,
        scratch_shapes=[pltpu.VMEM((2, PAIR * EXPERT_ROWS, D_HALF), BF16),
                        pltpu.VMEM((nfa, EXPERT_ROWS, EXPERT_TF), BF16),
                        pltpu.SemaphoreType.DMA, pltpu.SemaphoreType.DMA],
    )
    zeros = jnp.zeros(x_sorted.shape, x_sorted.dtype)
    return pl.pallas_call(
        _experts_kernel,
        out_shape=jax.ShapeDtypeStruct(x_sorted.shape, x_sorted.dtype),
        grid_spec=grid_spec,
        input_output_aliases={n_pre + 1: 0},
        compiler_params=_cparams(("arbitrary", "arbitrary")),
        name="experts",
    )(*meta, x_sorted, zeros, w_gu, w_gu, b_gu, b_gu, w_down, w_down, b_down, b_down)


def _combine_kernel(ys_ref, slot_ref, gate_ref, x2_ref, gfin_ref, o_ref):
    tm = x2_ref.shape[0]
    slots = ys_ref.shape[0] // PAIR
    slot = slot_ref[...]
    gates = gate_ref[...]
    p_j = lax.broadcasted_iota(I32, (tm, slots), 1)
    gmat = jnp.where(p_j == slot[:, 0:1], gates[:, 0:1], 0.0)
    for k in range(1, TOP_K):
        gmat = gmat + jnp.where(p_j == slot[:, k:k + 1], gates[:, k:k + 1], 0.0)
    gmat = gmat.astype(BF16)
    y_lo, y_hi = _unpack_rows(ys_ref[...])
    moe_lo = jnp.dot(gmat, y_lo.astype(BF16), preferred_element_type=F32)
    moe_hi = jnp.dot(gmat, y_hi.astype(BF16), preferred_element_type=F32)
    x = x2_ref[...] + jnp.concatenate([moe_lo, moe_hi], axis=1)
    ms = jnp.mean(x * x, axis=-1, keepdims=True)
    o_ref[...] = x * lax.rsqrt(ms + NORM_EPS) * gfin_ref[...]


def _combine(y_sorted, slots, gates, x2, g_final):
    n = x2.shape[0]
    tm = MERGE_TM
    aux = pl.BlockSpec((tm, LANES), lambda i: (i, 0))
    return pl.pallas_call(
        _combine_kernel,
        out_shape=jax.ShapeDtypeStruct((n, D_MODEL), F32),
        grid=(n // tm,),
        in_specs=[
            pl.BlockSpec((PAIR * MERGE_SLOTS, D_HALF), lambda i: (i, 0)),
            aux, aux,
            pl.BlockSpec((tm, D_MODEL), lambda i: (i, 0)),
            pl.BlockSpec((1, D_MODEL), lambda i: (0, 0)),
        ],
        out_specs=pl.BlockSpec((tm, D_MODEL), lambda i: (i, 0)),
        compiler_params=_cparams(("arbitrary",)),
        name="combine",
    )(y_sorted, slots, gates, x2, g_final)


def _layer(x2d, batch, seq, attn_norm_g, w_in, swa_sinks, w_alpha2, b_alpha, gla_norm_g, w_out,
           ffn_norm_g, w_router, b_router, w_gate_up, b_gate_up, w_down, b_down, final_norm_g):
    n = x2d.shape[0]
    assert n % MERGE_TM == 0 and seq % SWA_WINDOW == 0 and seq % GLA_CHUNK == 0
    o_qa, o_ka, o_va = 0, 2048, 2304
    o_qb, o_kb, o_vb, o_rb, o_al, o_ga, o_gb = 2560, 3584, 4608, 6656, 8704, 8720, 10768
    seg = lambda s, w: w_in[:, s:s + w]
    w_main = jnp.concatenate([
        seg(o_qa, 2048), seg(o_ga, 2048), seg(o_gb, 2048), seg(o_rb, 2048), seg(o_vb, 2048),
        seg(o_qb, 1024), seg(o_kb, 1024), seg(o_ka, 256), seg(o_va, 256)], axis=1).astype(BF16)
    w_a = jnp.pad(seg(o_al, GLA_GATE_RANK), ((0, 0), (0, LANES - GLA_GATE_RANK))).astype(BF16)
    w2_hi, w2_lo = _split_bf16(jnp.pad(w_alpha2, ((0, LANES - GLA_GATE_RANK), (0, 0))))
    wr_hi, wr_lo = _split_bf16(jnp.pad(w_router, ((0, 0), (0, LANES - N_EXPERTS))))
    b_r = jnp.pad(b_router, (0, LANES - N_EXPERTS)).reshape(1, LANES)
    slopes = jnp.exp2(-8.0 * jnp.arange(1, SWA_HEADS + 1, dtype=F32) / SWA_HEADS)

    p_act, a_lr = _inproj(x2d, attn_norm_g.reshape(1, D_MODEL), w_main, w_a)
    o_a = _swa(p_act, slopes, swa_sinks.astype(F32), batch, seq)
    o_b = _gla(p_act, a_lr, w2_hi, w2_lo, b_alpha.reshape(1, -1), gla_norm_g.reshape(1, -1), batch, seq)
    x2, x_sorted, slots, gates, cnt = _merge(
        p_act, o_a, o_b, x2d, w_out.astype(BF16), ffn_norm_g.reshape(1, D_MODEL), wr_hi, wr_lo, b_r)

    nt = n // MERGE_TM
    chunk8 = EXPERT_CHUNK // SUBLANES
    cnt_t = cnt[::SUBLANES, :N_EXPERTS].astype(I32)
    c8 = (cnt_t + SUBLANES - 1) // SUBLANES
    tile8 = (jnp.arange(nt, dtype=I32) * (MERGE_SLOTS // SUBLANES))[:, None]
    src8 = tile8 + jnp.cumsum(c8, axis=1) - c8
    cum8 = jnp.cumsum(c8, axis=0) - c8
    tot8 = jnp.sum(c8, axis=0)
    per_e = (tot8 + chunk8 - 1) // chunk8
    item_end = jnp.cumsum(per_e)
    n_items = item_end[-1]
    n_items_max = -(-(n * TOP_K // SUBLANES + nt * N_EXPERTS) // chunk8) + N_EXPERTS
    w_idx = jnp.arange(n_items_max, dtype=I32)
    w_eff = jnp.minimum(w_idx, n_items - 1)
    item_e = jnp.minimum(jnp.sum(item_end[None, :] <= w_eff[:, None], axis=1), N_EXPERTS - 1).astype(I32)
    item_j = w_eff - (item_end - per_e)[item_e]
    cum_e = cum8[:, item_e]
    item_t0 = jnp.sum(cum_e < (item_j * chunk8)[None, :], axis=0).astype(I32)
    item_t1 = jnp.sum(cum_e < ((item_j + 1) * chunk8)[None, :], axis=0).astype(I32)
    t0c = jnp.minimum(item_t0, nt - 1)
    t1c = jnp.maximum(item_t1 - 1, 0)
    item_base = cum8[t0c, item_e]
    item_rows = cum8[t1c, item_e] + c8[t1c, item_e] - item_base
    used = w_idx < n_items
    item_t1 = jnp.where(used, item_t1, item_t0)
    item_rows = jnp.where(used & (item_t1 > item_t0), item_rows, 0).astype(I32)
    meta = (n_items.astype(I32).reshape(1), item_e, item_t0, item_t1, item_base.astype(I32), item_rows,
            c8.reshape(-1), src8.reshape(-1), cum8.reshape(-1))

    y_sorted = _experts(x_sorted, meta, n_items_max, w_gate_up,
                        b_gate_up.reshape(N_EXPERTS, 1, 2 * D_FF), w_down,
                        b_down.reshape(N_EXPERTS, 1, D_MODEL))
    return _combine(y_sorted, slots, gates, x2, final_norm_g.reshape(1, D_MODEL))


def kernel(x, attn_norm_g, w_in, swa_sinks, w_alpha2, b_alpha, gla_norm_g, w_out, ffn_norm_g,
           w_router, b_router, w_gate_up, b_gate_up, w_down, b_down, final_norm_g):
    batch, seq, _ = x.shape
    assert w_in.shape[0] == 1, "single-layer block: the final norm is fused into the MoE combine"
    out = _layer(
        x.reshape(batch * seq, D_MODEL), batch, seq, attn_norm_g[0], w_in[0], swa_sinks[0],
        w_alpha2[0], b_alpha[0], gla_norm_g[0], w_out[0], ffn_norm_g[0], w_router[0], b_router[0],
        w_gate_up[0], b_gate_up[0], w_down[0], b_down[0], final_norm_g)
    return out.reshape(batch, seq, D_MODEL)
```

```python
import functools

import jax
import jax.numpy as jnp
from jax import lax
from jax.experimental import pallas as pl
from jax.experimental.pallas import tpu as pltpu

F32 = jnp.float32
BF16 = jnp.bfloat16
I32 = jnp.int32
U32 = jnp.uint32

D_MODEL = 2048
D_HALF = D_MODEL // 2
NORM_EPS = 1e-5
SWA_HEADS = 32
SWA_KV_HEADS = 4
SWA_HEAD_DIM = 64
SWA_WINDOW = 128
SWA_GROUP = SWA_HEADS // SWA_KV_HEADS
GLA_HEADS = 4
GLA_DK = 256
GLA_DV = 512
GLA_GATE_RANK = 16
GLA_TAU = 16.0
GLA_CHUNK = 64
GLA_GROUP = 256
N_EXPERTS = 32
TOP_K = 4
D_FF = 2048
SWIGLU_LIMIT = 7.0
SWIGLU_ALPHA = 1.702

LANES = 128
SUBLANES = 8
MASK_NEG = -1e30

COL_QA = 0
COL_GA = 2048
COL_GB = 4096
COL_RB = 6144
COL_VB = 8192
COL_QB = 10240
COL_KB = 11264
COL_KA = 12288
COL_VA = 12544
P_COLS = 12800

VMEM_LIMIT = 56 * 1024 * 1024

INPROJ_TM = 1024
INPROJ_TN = 512
MERGE_TM = 256
MERGE_SLOTS = MERGE_TM * TOP_K + N_EXPERTS * SUBLANES
SEG_BITS = (MERGE_TM // SUBLANES).bit_length()
PAIR = 2
EXPERT_CHUNK = 2560
EXPERT_SUB = 592
EXPERT_ROWS = -(-(EXPERT_CHUNK + MERGE_TM) // EXPERT_SUB) * EXPERT_SUB
EXPERT_TF = 256
EXPERT_TN = 128


def _cparams(sem):
    return pltpu.CompilerParams(dimension_semantics=sem, vmem_limit_bytes=VMEM_LIMIT)


def _sigmoid(x):
    return 1.0 / (1.0 + jnp.exp(-x))


def _split_bf16(x):
    hi = x.astype(BF16)
    return hi, (x - hi.astype(F32)).astype(BF16)


def _dot_split(a, b_hi, b_lo):
    a_hi, a_lo = _split_bf16(a)
    return ((jnp.dot(a_hi, b_hi, preferred_element_type=F32)
             + jnp.dot(a_lo, b_hi, preferred_element_type=F32))
            + jnp.dot(a_hi, b_lo, preferred_element_type=F32))


def _pack_rows(lo, hi):
    return pltpu.bitcast(pltpu.pack_elementwise([lo, hi], packed_dtype=BF16), BF16)


def _unpack_rows(z):
    words = pltpu.bitcast(z, U32)
    return (pltpu.unpack_elementwise(words, index=0, packed_dtype=BF16, unpacked_dtype=F32),
            pltpu.unpack_elementwise(words, index=1, packed_dtype=BF16, unpacked_dtype=F32))


def _inproj_kernel(x_ref, g_ref, w_ref, wa_ref, p_ref, a_ref, h_scr):
    @pl.when(pl.program_id(1) == 0)
    def _():
        x = x_ref[...]
        ms = jnp.mean(x * x, axis=-1, keepdims=True)
        h = (x * lax.rsqrt(ms + NORM_EPS) * g_ref[...]).astype(BF16)
        h_scr[...] = h
        a_ref[...] = jnp.dot(h, wa_ref[...], preferred_element_type=F32)

    p_ref[...] = jnp.dot(h_scr[...], w_ref[...], preferred_element_type=F32).astype(BF16)


def _inproj(x2d, g, w_main, w_a):
    n = x2d.shape[0]
    tm = min(INPROJ_TM, n)
    return pl.pallas_call(
        _inproj_kernel,
        out_shape=(jax.ShapeDtypeStruct((n, P_COLS), BF16),
                   jax.ShapeDtypeStruct((n, LANES), F32)),
        grid=(n // tm, P_COLS // INPROJ_TN),
        in_specs=[
            pl.BlockSpec((tm, D_MODEL), lambda i, j: (i, 0)),
            pl.BlockSpec((1, D_MODEL), lambda i, j: (0, 0)),
            pl.BlockSpec((D_MODEL, INPROJ_TN), lambda i, j: (0, j)),
            pl.BlockSpec((D_MODEL, LANES), lambda i, j: (0, 0)),
        ],
        out_specs=(
            pl.BlockSpec((tm, INPROJ_TN), lambda i, j: (i, j)),
            pl.BlockSpec((tm, LANES), lambda i, j: (i, 0)),
        ),
        scratch_shapes=[pltpu.VMEM((tm, D_MODEL), BF16)],
        compiler_params=_cparams(("arbitrary", "arbitrary")),
        name="inproj",
    )(x2d, g, w_main, w_a)


def _swa_kernel(slopes_ref, sinks_ref, q_ref, kp_ref, kc_ref, vp_ref, vc_ref, o_ref):
    n = pl.program_id(1)
    w = SWA_WINDOW
    hd = SWA_HEAD_DIM
    pairs = SWA_GROUP // 2
    lane = lax.broadcasted_iota(I32, (2 * w, LANES), 1)
    low = lane < hd
    kband = jnp.concatenate([kp_ref[...], kc_ref[...]], axis=0).astype(F32)
    vband = jnp.concatenate([vp_ref[...], vc_ref[...]], axis=0).astype(F32)
    qry = lax.broadcasted_iota(I32, (w, 2 * w), 0)
    key = lax.broadcasted_iota(I32, (w, 2 * w), 1)
    dist = qry - key + w
    valid = (dist >= 0) & (dist < w) & ((key >= w) | (n > 0))
    negd = jnp.where(valid, -dist.astype(F32), MASK_NEG)
    scale = jnp.asarray(hd ** -0.5, BF16)
    ones = jnp.ones((2 * w, LANES), BF16)
    out_lane_low = lax.broadcasted_iota(I32, (w, LANES), 1) < hd
    for kv in range(SWA_KV_HEADS):
        grp = slice((kv // 2) * LANES, (kv // 2 + 1) * LANES)
        kg, vg = kband[:, grp], vband[:, grp]
        kg_sw, vg_sw = pltpu.roll(kg, hd, axis=1), pltpu.roll(vg, hd, axis=1)
        if kv % 2 == 0:
            k_even, k_odd = jnp.where(low, kg, 0.0), jnp.where(low, 0.0, kg_sw)
            v_dup = jnp.where(low, vg, vg_sw)
        else:
            k_even, k_odd = jnp.where(low, kg_sw, 0.0), jnp.where(low, 0.0, kg)
            v_dup = jnp.where(low, vg_sw, vg)
        v_ext = jnp.concatenate([v_dup.astype(BF16), ones], axis=1)
        h0 = kv * SWA_GROUP
        q_pairs = jnp.concatenate(
            [q_ref[:, (h0 + 2 * i) * hd:(h0 + 2 * i + 2) * hd] for i in range(pairs)], axis=0) * scale
        nt = (((1,), (1,)), ((), ()))
        s_par = (lax.dot_general(q_pairs, k_even.astype(BF16), nt, preferred_element_type=F32),
                 lax.dot_general(q_pairs, k_odd.astype(BF16), nt, preferred_element_type=F32))
        probs, sink_terms = [], []
        for par in range(2):
            for i in range(pairs):
                h = h0 + 2 * i + par
                s = s_par[par][i * w:(i + 1) * w, :] + slopes_ref[h] * negd
                sink = sinks_ref[h]
                m = jnp.maximum(jnp.max(s, axis=-1, keepdims=True), sink)
                probs.append(jnp.exp(s - m).astype(BF16))
                sink_terms.append(jnp.exp(sink - m))
        o_ext = jnp.dot(jnp.concatenate(probs, axis=0), v_ext, preferred_element_type=F32)
        for i in range(pairs):
            halves = []
            for par in range(2):
                r0 = (par * pairs + i) * w
                blk = o_ext[r0:r0 + w, :]
                halves.append(blk[:, 0:LANES] / (blk[:, LANES:2 * LANES] + sink_terms[par * pairs + i]))
            c0 = (h0 + 2 * i) * hd
            o_ref[:, c0:c0 + LANES] = jnp.where(out_lane_low, halves[0], halves[1]).astype(BF16)


def _swa(p_act, slopes, sinks, batch, seq):
    n = batch * seq
    w = SWA_WINDOW
    nb = seq // w
    kvw = SWA_KV_HEADS * SWA_HEAD_DIM

    def cur(col):
        return lambda b, i: (b * nb + i, col)

    def prev(col):
        return lambda b, i: (b * nb + jnp.maximum(i - 1, 0), col)

    smem = pl.BlockSpec(memory_space=pltpu.SMEM)
    return pl.pallas_call(
        _swa_kernel,
        out_shape=jax.ShapeDtypeStruct((n, D_MODEL), BF16),
        grid=(batch, nb),
        in_specs=[
            smem, smem,
            pl.BlockSpec((w, D_MODEL), cur(COL_QA // D_MODEL)),
            pl.BlockSpec((w, kvw), prev(COL_KA // kvw)),
            pl.BlockSpec((w, kvw), cur(COL_KA // kvw)),
            pl.BlockSpec((w, kvw), prev(COL_VA // kvw)),
            pl.BlockSpec((w, kvw), cur(COL_VA // kvw)),
        ],
        out_specs=pl.BlockSpec((w, D_MODEL), lambda b, i: (b * nb + i, 0)),
        compiler_params=_cparams(("arbitrary", "arbitrary")),
        name="swa",
    )(slopes, sinks, p_act, p_act, p_act, p_act, p_act)


def _gla_kernel(q_ref, k_ref, v_ref, a_ref, w2h_ref, w2l_ref, ba_ref, gn_ref, o_ref,
                st_scr, qi_scr, ks_scr, dec_scr, intra_scr):
    c = GLA_CHUNK
    grp = GLA_GROUP
    cpg = grp // c
    seq = q_ref.shape[0]
    row = lax.broadcasted_iota(I32, (grp, grp), 0)
    col = lax.broadcasted_iota(I32, (grp, grp), 1)
    tri = jnp.where((row >= col) & (row // c == col // c), 1.0, 0.0).astype(BF16)
    causal = lax.broadcasted_iota(I32, (c, c), 0) >= lax.broadcasted_iota(I32, (c, c), 1)
    scale = jnp.asarray(GLA_DK ** -0.5, F32)

    def group(gi, carry):
        g0 = pl.multiple_of(gi * grp, grp)
        z = _dot_split(a_ref[pl.ds(g0, grp), :], w2h_ref[...], w2l_ref[...]) + ba_ref[...]
        log_a = (jnp.minimum(z, 0.0) - jnp.log1p(jnp.exp(-jnp.abs(z)))) * (1.0 / GLA_TAU)
        la_hi, la_mid = _split_bf16(log_a)
        la_lo = ((log_a - la_hi.astype(F32)) - la_mid.astype(F32)).astype(BF16)
        b_all = ((jnp.dot(tri, la_hi, preferred_element_type=F32)
                  + jnp.dot(tri, la_mid, preferred_element_type=F32))
                 + jnp.dot(tri, la_lo, preferred_element_type=F32))
        qf_all = q_ref[pl.ds(g0, grp), :].astype(F32) * scale
        kf_all = k_ref[pl.ds(g0, grp), :].astype(F32)
        for j in range(cpg):
            r0 = pl.multiple_of(g0 + j * c, c)
            b = b_all[j * c:(j + 1) * c, :]
            qf = qf_all[j * c:(j + 1) * c, :]
            kf = kf_all[j * c:(j + 1) * c, :]
            b_last = b[c - 1:c, :]
            b_mid = b[c // 2 - 1:c // 2, :]
            qi_scr[pl.ds(r0, c), :] = (qf * jnp.exp(b)).astype(BF16)
            ks_scr[pl.ds(r0, c), :] = (kf * jnp.exp(b_last - b)).astype(BF16)
            dec_scr[pl.ds(gi * cpg + j, 1), :] = jnp.exp(b_last)
            q_intra = (qf * jnp.exp(b - b_mid)).astype(BF16)
            k_intra = (kf * jnp.exp(b_mid - b)).astype(BF16)
            att = lax.dot_general(q_intra, k_intra, (((1,), (1,)), ((), ())),
                                  preferred_element_type=F32)
            att = jnp.where(causal, att, 0.0).astype(BF16)
            intra_scr[pl.ds(r0, c), :] = jnp.dot(att, v_ref[pl.ds(r0, c), :], preferred_element_type=F32)
        return carry

    lax.fori_loop(0, seq // grp, group, 0, unroll=2)

    st_scr[...] = jnp.zeros_like(st_scr)

    def chunk(ci, carry):
        r0 = pl.multiple_of(ci * c, c)
        st = st_scr[...]
        inter = lax.dot_general(qi_scr[pl.ds(r0, c), :], st.astype(BF16), (((1,), (1,)), ((), ())),
                                preferred_element_type=F32)
        upd = lax.dot_general(v_ref[pl.ds(r0, c), :], ks_scr[pl.ds(r0, c), :], (((0,), (0,)), ((), ())),
                              preferred_element_type=F32)
        st_scr[...] = st * dec_scr[pl.ds(ci, 1), :] + upd
        o = inter + intra_scr[pl.ds(r0, c), :]
        ms = jnp.mean(o * o, axis=-1, keepdims=True)
        o_ref[pl.ds(r0, c), :] = (o * lax.rsqrt(ms + NORM_EPS) * gn_ref[...]).astype(BF16)
        return carry

    lax.fori_loop(0, seq // c, chunk, 0, unroll=4)


def _gla(p_act, a_lr, w2_hi, w2_lo, b_alpha, gla_norm_g, batch, seq):
    n = batch * seq
    assert seq % GLA_GROUP == 0 and GLA_GROUP % GLA_CHUNK == 0
    return pl.pallas_call(
        _gla_kernel,
        out_shape=jax.ShapeDtypeStruct((n, GLA_HEADS * GLA_DV), BF16),
        grid=(batch, GLA_HEADS),
        in_specs=[
            pl.BlockSpec((seq, GLA_DK), lambda b, h: (b, COL_QB // GLA_DK + h)),
            pl.BlockSpec((seq, GLA_DK), lambda b, h: (b, COL_KB // GLA_DK + h)),
            pl.BlockSpec((seq, GLA_DV), lambda b, h: (b, COL_VB // GLA_DV + h)),
            pl.BlockSpec((seq, LANES), lambda b, h: (b, 0)),
            pl.BlockSpec((LANES, GLA_DK), lambda b, h: (0, h)),
            pl.BlockSpec((LANES, GLA_DK), lambda b, h: (0, h)),
            pl.BlockSpec((1, GLA_DK), lambda b, h: (0, h)),
            pl.BlockSpec((1, GLA_DV), lambda b, h: (0, 0)),
        ],
        out_specs=pl.BlockSpec((seq, GLA_DV), lambda b, h: (b, h)),
        scratch_shapes=[pltpu.VMEM((GLA_DV, GLA_DK), F32),
                        pltpu.VMEM((seq, GLA_DK), BF16),
                        pltpu.VMEM((seq, GLA_DK), BF16),
                        pltpu.VMEM((seq // GLA_CHUNK, GLA_DK), F32),
                        pltpu.VMEM((seq, GLA_DV), F32)],
        compiler_params=_cparams(("arbitrary", "arbitrary")),
        name="gla",
    )(p_act, p_act, p_act, a_lr, w2_hi, w2_lo, b_alpha, gla_norm_g)


def _merge_kernel(ga_ref, gb_ref, rb_ref, oa_ref, ob_ref, x_ref, wout_ref, gffn_ref, wrh_ref, wrl_ref, br_ref,
                  x2_ref, xs_ref, slot_ref, gate_ref, cnt_ref):
    tm = x_ref.shape[0]
    slots = xs_ref.shape[0] // PAIR
    ga = ga_ref[...].astype(F32)
    gb = gb_ref[...].astype(F32)
    rb = rb_ref[...].astype(F32)
    ob = ob_ref[...].astype(F32) * (rb * _sigmoid(rb))
    mix = _sigmoid(ga) * oa_ref[...].astype(F32) + _sigmoid(gb) * ob
    x2 = x_ref[...] + jnp.dot(mix.astype(BF16), wout_ref[...], preferred_element_type=F32)
    x2_ref[...] = x2
    ms = jnp.mean(x2 * x2, axis=-1, keepdims=True)
    h2 = x2 * lax.rsqrt(ms + NORM_EPS) * gffn_ref[...]

    logits = _dot_split(h2, wrh_ref[...], wrl_ref[...]) + br_ref[...]
    lane = lax.broadcasted_iota(I32, (tm, LANES), 1)
    neg_inf = jnp.asarray(-jnp.inf, F32)
    work = jnp.where(lane < N_EXPERTS, logits, neg_inf)
    vals, idxs = [], []
    for _ in range(TOP_K):
        m = jnp.max(work, axis=-1, keepdims=True)
        idx = jnp.min(jnp.where(work == m, lane, LANES), axis=-1, keepdims=True)
        vals.append(m)
        idxs.append(idx)
        work = jnp.where(lane == idx, neg_inf, work)
    exps = [jnp.exp(v - vals[0]) for v in vals]
    denom = exps[0] + exps[1] + exps[2] + exps[3]

    r_i = lax.broadcasted_iota(I32, (tm, tm), 0)
    c_i = lax.broadcasted_iota(I32, (tm, tm), 1)
    lower = jnp.where(r_i > c_i, 1.0, 0.0).astype(BF16)
    base = jnp.zeros((1, LANES), F32)
    onehots, ranks = [], []
    for k in range(TOP_K):
        onehot = lane == idxs[k]
        before = jnp.dot(lower, jnp.where(onehot, 1.0, 0.0).astype(BF16), preferred_element_type=F32)
        onehots.append(onehot)
        ranks.append(before + base)
        base = base + jnp.sum(jnp.where(onehot, 1.0, 0.0), axis=0, keepdims=True)
    cnt8 = jnp.floor((base + (SUBLANES - 1)) * (1.0 / SUBLANES))
    e_i = lax.broadcasted_iota(I32, (LANES, LANES), 0)
    e_j = lax.broadcasted_iota(I32, (LANES, LANES), 1)
    upper = jnp.where(e_i < e_j, 1.0, 0.0).astype(BF16)
    seg8 = jnp.dot(jnp.broadcast_to(cnt8, (SUBLANES, LANES)).astype(BF16), upper,
                   preferred_element_type=F32)[0:1, :]
    seg_row = seg8 * float(SUBLANES)
    slot_out = jnp.full((tm, LANES), -1.0, F32)
    gate_out = jnp.zeros((tm, LANES), F32)
    for k in range(TOP_K):
        slot_k = jnp.sum(jnp.where(onehots[k], ranks[k] + seg_row, 0.0), axis=-1, keepdims=True)
        slot_out = jnp.where(lane == k, slot_k, slot_out)
        gate_out = jnp.where(lane == k, exps[k] / denom, gate_out)
    slot_ref[...] = slot_out.astype(I32)
    gate_ref[...] = gate_out
    cnt_ref[...] = jnp.broadcast_to(base, (SUBLANES, LANES))

    slot_t = slot_out.T
    p_i = lax.broadcasted_iota(I32, (slots, tm), 0).astype(F32)
    perm = jnp.where(p_i == slot_t[0:1, :], 1.0, 0.0)
    for k in range(1, TOP_K):
        perm = perm + jnp.where(p_i == slot_t[k:k + 1, :], 1.0, 0.0)
    perm = perm.astype(BF16)
    h2b = h2.astype(BF16)
    cw = 256
    for c in range(D_HALF // cw):
        lo = jnp.dot(perm, h2b[:, c * cw:(c + 1) * cw], preferred_element_type=F32)
        hi = jnp.dot(perm, h2b[:, D_HALF + c * cw:D_HALF + (c + 1) * cw], preferred_element_type=F32)
        xs_ref[:, c * cw:(c + 1) * cw] = _pack_rows(lo, hi)


def _merge(p_act, o_a, o_b, x2d, w_out, g_ffn, w_router_hi, w_router_lo, b_router):
    n = x2d.shape[0]
    tm = MERGE_TM
    nt = n // tm
    row = lambda c: pl.BlockSpec((tm, D_MODEL), lambda i: (i, c))
    const = lambda shape: pl.BlockSpec(shape, lambda i: (0, 0))
    aux = pl.BlockSpec((tm, LANES), lambda i: (i, 0))
    return pl.pallas_call(
        _merge_kernel,
        out_shape=(
            jax.ShapeDtypeStruct((n, D_MODEL), F32),
            jax.ShapeDtypeStruct((nt * PAIR * MERGE_SLOTS, D_HALF), BF16),
            jax.ShapeDtypeStruct((n, LANES), I32),
            jax.ShapeDtypeStruct((n, LANES), F32),
            jax.ShapeDtypeStruct((nt * SUBLANES, LANES), F32),
        ),
        grid=(nt,),
        in_specs=[
            row(COL_GA // D_MODEL), row(COL_GB // D_MODEL), row(COL_RB // D_MODEL),
            row(0), row(0), row(0),
            pl.BlockSpec((D_MODEL, D_MODEL), lambda i: (0, 0), pipeline_mode=pl.Buffered(1)),
            const((1, D_MODEL)),
            pl.BlockSpec((D_MODEL, LANES), lambda i: (0, 0), pipeline_mode=pl.Buffered(1)),
            pl.BlockSpec((D_MODEL, LANES), lambda i: (0, 0), pipeline_mode=pl.Buffered(1)),
            const((1, LANES)),
        ],
        out_specs=(row(0), pl.BlockSpec((PAIR * MERGE_SLOTS, D_HALF), lambda i: (i, 0)), aux, aux,
                   pl.BlockSpec((SUBLANES, LANES), lambda i: (i, 0))),
        compiler_params=_cparams(("arbitrary",)),
        name="merge",
    )(p_act, p_act, p_act, o_a, o_b, x2d, w_out, g_ffn, w_router_hi, w_router_lo, b_router)


def _experts_kernel(nitems_ref, ie_ref, it0_ref, it1_ref, ibase_ref, irows_ref, c8_ref, src8_ref, cum8_ref,
                    xs_ref, zeros_ref, wg_ref, wl_ref, bg_ref, bl_ref, wdl_ref, wdh_ref, bdl_ref, bdh_ref,
                    ys_ref, bufs, act_scr, sem_in, sem_out):
    del zeros_ref
    w = pl.program_id(0)
    s = pl.program_id(1)
    nfa = D_FF // EXPERT_TF
    nfb = D_HALF // EXPERT_TN
    n_items = nitems_ref[0]
    valid = w < n_items
    cur = lax.rem(w, 2)
    unit = PAIR * SUBLANES

    def for_each_copy(item, slot, outbound, fn):
        e = ie_ref[item]
        base = ibase_ref[item]
        buf = bufs.at[slot]

        def tile_body(t, carry):
            seg = t * N_EXPERTS + e
            c = c8_ref[seg]
            hbm0 = src8_ref[seg] * unit
            buf0 = (cum8_ref[seg] - base) * unit
            off = jnp.zeros((), I32)
            for bit in reversed(range(SEG_BITS)):
                rows = unit << bit
                flag = lax.bitwise_and(lax.shift_right_logical(c, bit), 1)

                @pl.when(flag == 1)
                def _(rows=rows, off=off):
                    in_buf = buf.at[pl.ds(pl.multiple_of(buf0 + off, unit), rows), :]
                    if outbound:
                        fn(pltpu.make_async_copy(
                            in_buf, ys_ref.at[pl.ds(pl.multiple_of(hbm0 + off, unit), rows), :], sem_out))
                    else:
                        fn(pltpu.make_async_copy(
                            xs_ref.at[pl.ds(pl.multiple_of(hbm0 + off, unit), rows), :], in_buf, sem_in))

                off = off + flag * rows
            return carry

        lax.fori_loop(it0_ref[item], it1_ref[item], tile_body, 0)

    start = lambda cp: cp.start()
    wait = lambda cp: cp.wait()

    @pl.when((w == 0) & (s == 0))
    def _():
        bufs[...] = jnp.zeros_like(bufs)
        act_scr[...] = jnp.zeros_like(act_scr)
        for_each_copy(0, 0, False, start)

    @pl.when(valid & (s == 0))
    def _():
        for_each_copy(w, cur, False, wait)

    n_sub = lax.div(irows_ref[w] * SUBLANES + (EXPERT_SUB - 1), EXPERT_SUB)

    @pl.when(valid & (s < nfa))
    def _():
        wg = wg_ref[...].astype(BF16)
        wl = wl_ref[...].astype(BF16)
        bg = bg_ref[...]
        bl = bl_ref[...]
        tile = jnp.minimum(s, nfa - 1)

        def sub_body(r, carry):
            z = bufs[cur, pl.ds(pl.multiple_of(r * (PAIR * EXPERT_SUB), PAIR * EXPERT_SUB),
                                PAIR * EXPERT_SUB), :]
            lo, hi = _unpack_rows(z)
            lo = lo.astype(BF16)
            hi = hi.astype(BF16)
            glu = (jnp.dot(lo, wg[0:D_HALF], preferred_element_type=F32)
                   + jnp.dot(hi, wg[D_HALF:D_MODEL], preferred_element_type=F32) + bg)
            lin = (jnp.dot(lo, wl[0:D_HALF], preferred_element_type=F32)
                   + jnp.dot(hi, wl[D_HALF:D_MODEL], preferred_element_type=F32) + bl)
            glu = jnp.minimum(glu, SWIGLU_LIMIT)
            lin = jnp.clip(lin, -SWIGLU_LIMIT, SWIGLU_LIMIT)
            act = glu * _sigmoid(SWIGLU_ALPHA * glu) * (lin + 1.0)
            act_scr[tile, pl.ds(pl.multiple_of(r * EXPERT_SUB, EXPERT_SUB), EXPERT_SUB), :] = act.astype(BF16)
            return carry

        lax.fori_loop(0, n_sub, sub_body, 0)

    @pl.when(valid & (s == nfa))
    def _():
        @pl.when(w >= 1)
        def _():
            for_each_copy(w - 1, 1 - cur, True, wait)

        @pl.when(w + 1 < n_items)
        def _():
            for_each_copy(w + 1, 1 - cur, False, start)

    @pl.when(valid & (s >= nfa))
    def _():
        wd = jnp.concatenate([wdl_ref[...], wdh_ref[...]], axis=1).astype(BF16)
        bd = jnp.concatenate([bdl_ref[...], bdh_ref[...]], axis=1)
        col = pl.multiple_of(jnp.maximum(s - nfa, 0) * EXPERT_TN, EXPERT_TN)

        def sub_body(r, carry):
            rows = pl.ds(pl.multiple_of(r * EXPERT_SUB, EXPERT_SUB), EXPERT_SUB)
            acc = jnp.dot(act_scr[0, rows, :], wd[0:EXPERT_TF], preferred_element_type=F32)
            for t in range(1, nfa):
                acc += jnp.dot(act_scr[t, rows, :], wd[t * EXPERT_TF:(t + 1) * EXPERT_TF],
                               preferred_element_type=F32)
            acc = acc + bd
            bufs[cur, pl.ds(pl.multiple_of(r * (PAIR * EXPERT_SUB), PAIR * EXPERT_SUB), PAIR * EXPERT_SUB),
                 pl.ds(col, EXPERT_TN)] = _pack_rows(acc[:, 0:EXPERT_TN], acc[:, EXPERT_TN:2 * EXPERT_TN])
            return carry

        lax.fori_loop(0, n_sub, sub_body, 0)

    @pl.when(valid & (s == nfa + nfb - 1))
    def _():
        for_each_copy(w, cur, True, start)

        @pl.when(w == n_items - 1)
        def _():
            for_each_copy(w, cur, True, wait)


def _experts(x_sorted, meta, n_items_max, w_gu, b_gu, w_down, b_down):
    nfa = D_FF // EXPERT_TF
    nfb = D_HALF // EXPERT_TN
    n_pre = len(meta)

    def fa(w, s, n_items):
        return jnp.where(w < n_items[0], jnp.minimum(s, nfa - 1), nfa - 1)

    def fb(w, s, n_items):
        return jnp.where(w < n_items[0], jnp.maximum(s - nfa, 0), nfb - 1)

    def wspec(shape, col):
        return pl.BlockSpec(shape, lambda w, s, *m: (m[1][w], 0, col(w, s, m[0])))

    any_spec = pl.BlockSpec(memory_space=pl.ANY)
    grid_spec = pltpu.PrefetchScalarGridSpec(
        num_scalar_prefetch=n_pre,
        grid=(n_items_max, nfa + nfb),
        in_specs=[
            any_spec, any_spec,
            wspec((None, D_MODEL, EXPERT_TF), fa),
            wspec((None, D_MODEL, EXPERT_TF), lambda w, s, n: nfa + fa(w, s, n)),
            wspec((None, 1, EXPERT_TF), fa),
            wspec((None, 1, EXPERT_TF), lambda w, s, n: nfa + fa(w, s, n)),
            wspec((None, D_FF, EXPERT_TN), fb),
            wspec((None, D_FF, EXPERT_TN), lambda w, s, n: nfb + fb(w, s, n)),
            wspec((None, 1, EXPERT_TN), fb),
            wspec((None, 1, EXPERT_TN), lambda w, s, n: nfb + fb(w, s, n)),
        ],
        out_specs=any_spec,
        scratch_shapes=[pltpu.VMEM((2, PAIR * EXPERT_ROWS, D_HALF), BF16),
                        pltpu.VMEM((nfa, EXPERT_ROWS, EXPERT_TF), BF16),
                        pltpu.SemaphoreType.DMA, pltpu.SemaphoreType.DMA],
    )
    zeros = jnp.zeros(x_sorted.shape, x_sorted.dtype)
    return pl.pallas_call(
        _experts_kernel,
        out_shape=jax.ShapeDtypeStruct(x_sorted.shape, x_sorted.dtype),
        grid_spec=grid_spec,
        input_output_aliases={n_pre + 1: 0},
        compiler_params=_cparams(("arbitrary", "arbitrary")),
        name="experts",
    )(*meta, x_sorted, zeros, w_gu, w_gu, b_gu, b_gu, w_down, w_down, b_down, b_down)


def _combine_kernel(ys_ref, slot_ref, gate_ref, x2_ref, gfin_ref, o_ref):
    tm = x2_ref.shape[0]
    slots = ys_ref.shape[0] // PAIR
    slot = slot_ref[...]
    gates = gate_ref[...]
    p_j = lax.broadcasted_iota(I32, (tm, slots), 1)
    gmat = jnp.where(p_j == slot[:, 0:1], gates[:, 0:1], 0.0)
    for k in range(1, TOP_K):
        gmat = gmat + jnp.where(p_j == slot[:, k:k + 1], gates[:, k:k + 1], 0.0)
    gmat = gmat.astype(BF16)
    y_lo, y_hi = _unpack_rows(ys_ref[...])
    moe_lo = jnp.dot(gmat, y_lo.astype(BF16), preferred_element_type=F32)
    moe_hi = jnp.dot(gmat, y_hi.astype(BF16), preferred_element_type=F32)
    x = x2_ref[...] + jnp.concatenate([moe_lo, moe_hi], axis=1)
    ms = jnp.mean(x * x, axis=-1, keepdims=True)
    o_ref[...] = x * lax.rsqrt(ms + NORM_EPS) * gfin_ref[...]


def _combine(y_sorted, slots, gates, x2, g_final):
    n = x2.shape[0]
    tm = MERGE_TM
    aux = pl.BlockSpec((tm, LANES), lambda i: (i, 0))
    return pl.pallas_call(
        _combine_kernel,
        out_shape=jax.ShapeDtypeStruct((n, D_MODEL), F32),
        grid=(n // tm,),
        in_specs=[
            pl.BlockSpec((PAIR * MERGE_SLOTS, D_HALF), lambda i: (i, 0)),
            aux, aux,
            pl.BlockSpec((tm, D_MODEL), lambda i: (i, 0)),
            pl.BlockSpec((1, D_MODEL), lambda i: (0, 0)),
        ],
        out_specs=pl.BlockSpec((tm, D_MODEL), lambda i: (i, 0)),
        compiler_params=_cparams(("arbitrary",)),
        name="combine",
    )(y_sorted, slots, gates, x2, g_final)


def _layer(x2d, batch, seq, attn_norm_g, w_in, swa_sinks, w_alpha2, b_alpha, gla_norm_g, w_out,
           ffn_norm_g, w_router, b_router, w_gate_up, b_gate_up, w_down, b_down, final_norm_g):
    n = x2d.shape[0]
    assert n % MERGE_TM == 0 and seq % SWA_WINDOW == 0 and seq % GLA_CHUNK == 0
    o_qa, o_ka, o_va = 0, 2048, 2304
    o_qb, o_kb, o_vb, o_rb, o_al, o_ga, o_gb = 2560, 3584, 4608, 6656, 8704, 8720, 10768
    seg = lambda s, w: w_in[:, s:s + w]
    w_main = jnp.concatenate([
        seg(o_qa, 2048), seg(o_ga, 2048), seg(o_gb, 2048), seg(o_rb, 2048), seg(o_vb, 2048),
        seg(o_qb, 1024), seg(o_kb, 1024), seg(o_ka, 256), seg(o_va, 256)], axis=1).astype(BF16)
    w_a = jnp.pad(seg(o_al, GLA_GATE_RANK), ((0, 0), (0, LANES - GLA_GATE_RANK))).astype(BF16)
    w2_hi, w2_lo = _split_bf16(jnp.pad(w_alpha2, ((0, LANES - GLA_GATE_RANK), (0, 0))))
    wr_hi, wr_lo = _split_bf16(jnp.pad(w_router, ((0, 0), (0, LANES - N_EXPERTS))))
    b_r = jnp.pad(b_router, (0, LANES - N_EXPERTS)).reshape(1, LANES)
    slopes = jnp.exp2(-8.0 * jnp.arange(1, SWA_HEADS + 1, dtype=F32) / SWA_HEADS)

    p_act, a_lr = _inproj(x2d, attn_norm_g.reshape(1, D_MODEL), w_main, w_a)
    o_a = _swa(p_act, slopes, swa_sinks.astype(F32), batch, seq)
    o_b = _gla(p_act, a_lr, w2_hi, w2_lo, b_alpha.reshape(1, -1), gla_norm_g.reshape(1, -1), batch, seq)
    x2, x_sorted, slots, gates, cnt = _merge(
        p_act, o_a, o_b, x2d, w_out.astype(BF16), ffn_norm_g.reshape(1, D_MODEL), wr_hi, wr_lo, b_r)

    nt = n // MERGE_TM
    chunk8 = EXPERT_CHUNK // SUBLANES
    cnt_t = cnt[::SUBLANES, :N_EXPERTS].astype(I32)
    c8 = (cnt_t + SUBLANES - 1) // SUBLANES
    tile8 = (jnp.arange(nt, dtype=I32) * (MERGE_SLOTS // SUBLANES))[:, None]
    src8 = tile8 + jnp.cumsum(c8, axis=1) - c8
    cum8 = jnp.cumsum(c8, axis=0) - c8
    tot8 = jnp.sum(c8, axis=0)
    per_e = (tot8 + chunk8 - 1) // chunk8
    item_end = jnp.cumsum(per_e)
    n_items = item_end[-1]
    n_items_max = -(-(n * TOP_K // SUBLANES + nt * N_EXPERTS) // chunk8) + N_EXPERTS
    w_idx = jnp.arange(n_items_max, dtype=I32)
    w_eff = jnp.minimum(w_idx, n_items - 1)
    item_e = jnp.minimum(jnp.sum(item_end[None, :] <= w_eff[:, None], axis=1), N_EXPERTS - 1).astype(I32)
    item_j = w_eff - (item_end - per_e)[item_e]
    cum_e = cum8[:, item_e]
    item_t0 = jnp.sum(cum_e < (item_j * chunk8)[None, :], axis=0).astype(I32)
    item_t1 = jnp.sum(cum_e < ((item_j + 1) * chunk8)[None, :], axis=0).astype(I32)
    t0c = jnp.minimum(item_t0, nt - 1)
    t1c = jnp.maximum(item_t1 - 1, 0)
    item_base = cum8[t0c, item_e]
    item_rows = cum8[t1c, item_e] + c8[t1c, item_e] - item_base
    used = w_idx < n_items
    item_t1 = jnp.where(used, item_t1, item_t0)
    item_rows = jnp.where(used & (item_t1 > item_t0), item_rows, 0).astype(I32)
    meta = (n_items.astype(I32).reshape(1), item_e, item_t0, item_t1, item_base.astype(I32), item_rows,
            c8.reshape(-1), src8.reshape(-1), cum8.reshape(-1))

    y_sorted = _experts(x_sorted, meta, n_items_max, w_gate_up,
                        b_gate_up.reshape(N_EXPERTS, 1, 2 * D_FF), w_down,
                        b_down.reshape(N_EXPERTS, 1, D_MODEL))
    return _combine(y_sorted, slots, gates, x2, final_norm_g.reshape(1, D_MODEL))


def kernel(x, attn_norm_g, w_in, swa_sinks, w_alpha2, b_alpha, gla_norm_g, w_out, ffn_norm_g,
           w_router, b_router, w_gate_up, b_gate_up, w_down, b_down, final_norm_g):
    batch, seq, _ = x.shape
    assert w_in.shape[0] == 1, "single-layer block: the final norm is fused into the MoE combine"
    out = _layer(
        x.reshape(batch * seq, D_MODEL), batch, seq, attn_norm_g[0], w_in[0], swa_sinks[0],
        w_alpha2[0], b_alpha[0], gla_norm_g[0], w_out[0], ffn_norm_g[0], w_router[0], b_router[0],
        w_gate_up[0], b_gate_up[0], w_down[0], b_down[0], final_norm_g)
    return out.reshape(batch, seq, D_MODEL)
```

```python
import functools

import jax
import jax.numpy as jnp
from jax import lax
from jax.experimental import pallas as pl
from jax.experimental.pallas import tpu as pltpu

F32 = jnp.float32
BF16 = jnp.bfloat16
I32 = jnp.int32
U32 = jnp.uint32

D_MODEL = 2048
D_HALF = D_MODEL // 2
NORM_EPS = 1e-5
SWA_HEADS = 32
SWA_KV_HEADS = 4
SWA_HEAD_DIM = 64
SWA_WINDOW = 128
SWA_GROUP = SWA_HEADS // SWA_KV_HEADS
GLA_HEADS = 4
GLA_DK = 256
GLA_DV = 512
GLA_GATE_RANK = 16
GLA_TAU = 16.0
GLA_CHUNK = 64
GLA_GROUP = 256
N_EXPERTS = 32
TOP_K = 4
D_FF = 2048
SWIGLU_LIMIT = 7.0
SWIGLU_ALPHA = 1.702

LANES = 128
SUBLANES = 8
MASK_NEG = -1e30

COL_QA = 0
COL_GA = 2048
COL_GB = 4096
COL_RB = 6144
COL_VB = 8192
COL_QB = 10240
COL_KB = 11264
COL_KA = 12288
COL_VA = 12544
P_COLS = 12800

VMEM_LIMIT = 56 * 1024 * 1024

INPROJ_TM = 1024
INPROJ_TN = 1280
MERGE_TM = 256
MERGE_SLOTS = MERGE_TM * TOP_K + N_EXPERTS * SUBLANES
SEG_BITS = (MERGE_TM // SUBLANES).bit_length()
PAIR = 2
EXPERT_CHUNK = 2560
EXPERT_SUB = 592
EXPERT_ROWS = -(-(EXPERT_CHUNK + MERGE_TM) // EXPERT_SUB) * EXPERT_SUB
EXPERT_TF = 256
EXPERT_TN = 128


def _cparams(sem):
    return pltpu.CompilerParams(dimension_semantics=sem, vmem_limit_bytes=VMEM_LIMIT)


def _sigmoid(x):
    return 1.0 / (1.0 + jnp.exp(-x))


def _split_bf16(x):
    hi = x.astype(BF16)
    return hi, (x - hi.astype(F32)).astype(BF16)


def _dot_split(a, b_hi, b_lo):
    a_hi, a_lo = _split_bf16(a)
    return ((jnp.dot(a_hi, b_hi, preferred_element_type=F32)
             + jnp.dot(a_lo, b_hi, preferred_element_type=F32))
            + jnp.dot(a_hi, b_lo, preferred_element_type=F32))


def _pack_rows(lo, hi):
    return pltpu.bitcast(pltpu.pack_elementwise([lo, hi], packed_dtype=BF16), BF16)


def _unpack_rows(z):
    words = pltpu.bitcast(z, U32)
    return (pltpu.unpack_elementwise(words, index=0, packed_dtype=BF16, unpacked_dtype=F32),
            pltpu.unpack_elementwise(words, index=1, packed_dtype=BF16, unpacked_dtype=F32))


def _inproj_kernel(x_ref, g_ref, w_ref, wa_ref, p_ref, a_ref, h_scr):
    @pl.when(pl.program_id(1) == 0)
    def _():
        x = x_ref[...]
        ms = jnp.mean(x * x, axis=-1, keepdims=True)
        h = (x * lax.rsqrt(ms + NORM_EPS) * g_ref[...]).astype(BF16)
        h_scr[...] = h
        a_ref[...] = jnp.dot(h, wa_ref[...], preferred_element_type=F32)

    p_ref[...] = jnp.dot(h_scr[...], w_ref[...], preferred_element_type=F32).astype(BF16)


def _inproj(x2d, g, w_main, w_a):
    n = x2d.shape[0]
    tm = min(INPROJ_TM, n)
    return pl.pallas_call(
        _inproj_kernel,
        out_shape=(jax.ShapeDtypeStruct((n, P_COLS), BF16),
                   jax.ShapeDtypeStruct((n, LANES), F32)),
        grid=(n // tm, P_COLS // INPROJ_TN),
        in_specs=[
            pl.BlockSpec((tm, D_MODEL), lambda i, j: (i, 0)),
            pl.BlockSpec((1, D_MODEL), lambda i, j: (0, 0)),
            pl.BlockSpec((D_MODEL, INPROJ_TN), lambda i, j: (0, j)),
            pl.BlockSpec((D_MODEL, LANES), lambda i, j: (0, 0)),
        ],
        out_specs=(
            pl.BlockSpec((tm, INPROJ_TN), lambda i, j: (i, j)),
            pl.BlockSpec((tm, LANES), lambda i, j: (i, 0)),
        ),
        scratch_shapes=[pltpu.VMEM((tm, D_MODEL), BF16)],
        compiler_params=_cparams(("arbitrary", "arbitrary")),
        name="inproj",
    )(x2d, g, w_main, w_a)


def _swa_kernel(slopes_ref, sinks_ref, q_ref, kp_ref, kc_ref, vp_ref, vc_ref, o_ref):
    n = pl.program_id(1)
    w = SWA_WINDOW
    hd = SWA_HEAD_DIM
    pairs = SWA_GROUP // 2
    lane = lax.broadcasted_iota(I32, (2 * w, LANES), 1)
    low = lane < hd
    kband = jnp.concatenate([kp_ref[...], kc_ref[...]], axis=0).astype(F32)
    vband = jnp.concatenate([vp_ref[...], vc_ref[...]], axis=0).astype(F32)
    qry = lax.broadcasted_iota(I32, (w, 2 * w), 0)
    key = lax.broadcasted_iota(I32, (w, 2 * w), 1)
    dist = qry - key + w
    valid = (dist >= 0) & (dist < w) & ((key >= w) | (n > 0))
    negd = jnp.where(valid, -dist.astype(F32), MASK_NEG)
    scale = jnp.asarray(hd ** -0.5, BF16)
    ones = jnp.ones((2 * w, LANES), BF16)
    out_lane_low = lax.broadcasted_iota(I32, (w, LANES), 1) < hd
    for kv in range(SWA_KV_HEADS):
        grp = slice((kv // 2) * LANES, (kv // 2 + 1) * LANES)
        kg, vg = kband[:, grp], vband[:, grp]
        kg_sw, vg_sw = pltpu.roll(kg, hd, axis=1), pltpu.roll(vg, hd, axis=1)
        if kv % 2 == 0:
            k_even, k_odd = jnp.where(low, kg, 0.0), jnp.where(low, 0.0, kg_sw)
            v_dup = jnp.where(low, vg, vg_sw)
        else:
            k_even, k_odd = jnp.where(low, kg_sw, 0.0), jnp.where(low, 0.0, kg)
            v_dup = jnp.where(low, vg_sw, vg)
        v_ext = jnp.concatenate([v_dup.astype(BF16), ones], axis=1)
        h0 = kv * SWA_GROUP
        q_pairs = jnp.concatenate(
            [q_ref[:, (h0 + 2 * i) * hd:(h0 + 2 * i + 2) * hd] for i in range(pairs)], axis=0) * scale
        nt = (((1,), (1,)), ((), ()))
        s_par = (lax.dot_general(q_pairs, k_even.astype(BF16), nt, preferred_element_type=F32),
                 lax.dot_general(q_pairs, k_odd.astype(BF16), nt, preferred_element_type=F32))
        probs, sink_terms = [], []
        for par in range(2):
            for i in range(pairs):
                h = h0 + 2 * i + par
                s = s_par[par][i * w:(i + 1) * w, :] + slopes_ref[h] * negd
                sink = sinks_ref[h]
                m = jnp.maximum(jnp.max(s, axis=-1, keepdims=True), sink)
                probs.append(jnp.exp(s - m).astype(BF16))
                sink_terms.append(jnp.exp(sink - m))
        o_ext = jnp.dot(jnp.concatenate(probs, axis=0), v_ext, preferred_element_type=F32)
        for i in range(pairs):
            halves = []
            for par in range(2):
                r0 = (par * pairs + i) * w
                blk = o_ext[r0:r0 + w, :]
                halves.append(blk[:, 0:LANES] / (blk[:, LANES:2 * LANES] + sink_terms[par * pairs + i]))
            c0 = (h0 + 2 * i) * hd
            o_ref[:, c0:c0 + LANES] = jnp.where(out_lane_low, halves[0], halves[1]).astype(BF16)


def _swa(p_act, slopes, sinks, batch, seq):
    n = batch * seq
    w = SWA_WINDOW
    nb = seq // w
    kvw = SWA_KV_HEADS * SWA_HEAD_DIM

    def cur(col):
        return lambda b, i: (b * nb + i, col)

    def prev(col):
        return lambda b, i: (b * nb + jnp.maximum(i - 1, 0), col)

    smem = pl.BlockSpec(memory_space=pltpu.SMEM)
    return pl.pallas_call(
        _swa_kernel,
        out_shape=jax.ShapeDtypeStruct((n, D_MODEL), BF16),
        grid=(batch, nb),
        in_specs=[
            smem, smem,
            pl.BlockSpec((w, D_MODEL), cur(COL_QA // D_MODEL)),
            pl.BlockSpec((w, kvw), prev(COL_KA // kvw)),
            pl.BlockSpec((w, kvw), cur(COL_KA // kvw)),
            pl.BlockSpec((w, kvw), prev(COL_VA // kvw)),
            pl.BlockSpec((w, kvw), cur(COL_VA // kvw)),
        ],
        out_specs=pl.BlockSpec((w, D_MODEL), lambda b, i: (b * nb + i, 0)),
        compiler_params=_cparams(("arbitrary", "arbitrary")),
        name="swa",
    )(slopes, sinks, p_act, p_act, p_act, p_act, p_act)


def _gla_kernel(q_ref, k_ref, v_ref, a_ref, w2h_ref, w2l_ref, ba_ref, gn_ref, o_ref,
                st_scr, qi_scr, ks_scr, dec_scr, intra_scr):
    c = GLA_CHUNK
    grp = GLA_GROUP
    cpg = grp // c
    seq = q_ref.shape[0]
    row = lax.broadcasted_iota(I32, (grp, grp), 0)
    col = lax.broadcasted_iota(I32, (grp, grp), 1)
    tri = jnp.where((row >= col) & (row // c == col // c), 1.0, 0.0).astype(BF16)
    causal = lax.broadcasted_iota(I32, (c, c), 0) >= lax.broadcasted_iota(I32, (c, c), 1)
    scale = jnp.asarray(GLA_DK ** -0.5, F32)

    def group(gi, carry):
        g0 = pl.multiple_of(gi * grp, grp)
        z = _dot_split(a_ref[pl.ds(g0, grp), :], w2h_ref[...], w2l_ref[...]) + ba_ref[...]
        log_a = (jnp.minimum(z, 0.0) - jnp.log1p(jnp.exp(-jnp.abs(z)))) * (1.0 / GLA_TAU)
        la_hi, la_mid = _split_bf16(log_a)
        la_lo = ((log_a - la_hi.astype(F32)) - la_mid.astype(F32)).astype(BF16)
        b_all = ((jnp.dot(tri, la_hi, preferred_element_type=F32)
                  + jnp.dot(tri, la_mid, preferred_element_type=F32))
                 + jnp.dot(tri, la_lo, preferred_element_type=F32))
        qf_all = q_ref[pl.ds(g0, grp), :].astype(F32) * scale
        kf_all = k_ref[pl.ds(g0, grp), :].astype(F32)
        atts = []
        for j in range(cpg):
            r0 = pl.multiple_of(g0 + j * c, c)
            b = b_all[j * c:(j + 1) * c, :]
            qf = qf_all[j * c:(j + 1) * c, :]
            kf = kf_all[j * c:(j + 1) * c, :]
            b_last = b[c - 1:c, :]
            b_mid = b[c // 2 - 1:c // 2, :]
            qi_scr[pl.ds(r0, c), :] = (qf * jnp.exp(b)).astype(BF16)
            ks_scr[pl.ds(r0, c), :] = (kf * jnp.exp(b_last - b)).astype(BF16)
            dec_scr[pl.ds(gi * cpg + j, 1), :] = jnp.exp(b_last)
            q_intra = (qf * jnp.exp(b - b_mid)).astype(BF16)
            k_intra = (kf * jnp.exp(b_mid - b)).astype(BF16)
            atts.append(lax.dot_general(q_intra, k_intra, (((1,), (1,)), ((), ())),
                                        preferred_element_type=F32))
        for j in range(cpg):
            r0 = pl.multiple_of(g0 + j * c, c)
            att = jnp.where(causal, atts[j], 0.0).astype(BF16)
            intra_scr[pl.ds(r0, c), :] = jnp.dot(att, v_ref[pl.ds(r0, c), :], preferred_element_type=F32)
        return carry

    lax.fori_loop(0, seq // grp, group, 0, unroll=2)

    st_scr[...] = jnp.zeros_like(st_scr)

    def scan_group(gi, carry):
        rows = [pl.multiple_of((gi * cpg + j) * c, c) for j in range(cpg)]
        upds = [lax.dot_general(v_ref[pl.ds(r0, c), :], ks_scr[pl.ds(r0, c), :], (((0,), (0,)), ((), ())),
                                preferred_element_type=F32) for r0 in rows]
        st = st_scr[...]
        for j, r0 in enumerate(rows):
            inter = lax.dot_general(qi_scr[pl.ds(r0, c), :], st.astype(BF16), (((1,), (1,)), ((), ())),
                                    preferred_element_type=F32)
            st = st * dec_scr[pl.ds(gi * cpg + j, 1), :] + upds[j]
            o = inter + intra_scr[pl.ds(r0, c), :]
            ms = jnp.mean(o * o, axis=-1, keepdims=True)
            o_ref[pl.ds(r0, c), :] = (o * lax.rsqrt(ms + NORM_EPS) * gn_ref[...]).astype(BF16)
        st_scr[...] = st
        return carry

    lax.fori_loop(0, seq // grp, scan_group, 0)


def _gla(p_act, a_lr, w2_hi, w2_lo, b_alpha, gla_norm_g, batch, seq):
    n = batch * seq
    assert seq % GLA_GROUP == 0 and GLA_GROUP % GLA_CHUNK == 0
    return pl.pallas_call(
        _gla_kernel,
        out_shape=jax.ShapeDtypeStruct((n, GLA_HEADS * GLA_DV), BF16),
        grid=(batch, GLA_HEADS),
        in_specs=[
            pl.BlockSpec((seq, GLA_DK), lambda b, h: (b, COL_QB // GLA_DK + h)),
            pl.BlockSpec((seq, GLA_DK), lambda b, h: (b, COL_KB // GLA_DK + h)),
            pl.BlockSpec((seq, GLA_DV), lambda b, h: (b, COL_VB // GLA_DV + h)),
            pl.BlockSpec((seq, LANES), lambda b, h: (b, 0)),
            pl.BlockSpec((LANES, GLA_DK), lambda b, h: (0, h)),
            pl.BlockSpec((LANES, GLA_DK), lambda b, h: (0, h)),
            pl.BlockSpec((1, GLA_DK), lambda b, h: (0, h)),
            pl.BlockSpec((1, GLA_DV), lambda b, h: (0, 0)),
        ],
        out_specs=pl.BlockSpec((seq, GLA_DV), lambda b, h: (b, h)),
        scratch_shapes=[pltpu.VMEM((GLA_DV, GLA_DK), F32),
                        pltpu.VMEM((seq, GLA_DK), BF16),
                        pltpu.VMEM((seq, GLA_DK), BF16),
                        pltpu.VMEM((seq // GLA_CHUNK, GLA_DK), F32),
                        pltpu.VMEM((seq, GLA_DV), F32)],
        compiler_params=_cparams(("arbitrary", "arbitrary")),
        name="gla",
    )(p_act, p_act, p_act, a_lr, w2_hi, w2_lo, b_alpha, gla_norm_g)


def _merge_kernel(ga_ref, gb_ref, rb_ref, oa_ref, ob_ref, x_ref, wout_ref, gffn_ref, wrh_ref, wrl_ref, br_ref,
                  x2_ref, xs_ref, slot_ref, gate_ref, cnt_ref):
    tm = x_ref.shape[0]
    slots = xs_ref.shape[0] // PAIR
    ga = ga_ref[...].astype(F32)
    gb = gb_ref[...].astype(F32)
    rb = rb_ref[...].astype(F32)
    ob = ob_ref[...].astype(F32) * (rb * _sigmoid(rb))
    mix = _sigmoid(ga) * oa_ref[...].astype(F32) + _sigmoid(gb) * ob
    x2 = x_ref[...] + jnp.dot(mix.astype(BF16), wout_ref[...], preferred_element_type=F32)
    x2_ref[...] = x2
    ms = jnp.mean(x2 * x2, axis=-1, keepdims=True)
    h2 = x2 * lax.rsqrt(ms + NORM_EPS) * gffn_ref[...]

    logits = _dot_split(h2, wrh_ref[...], wrl_ref[...]) + br_ref[...]
    lane = lax.broadcasted_iota(I32, (tm, LANES), 1)
    neg_inf = jnp.asarray(-jnp.inf, F32)
    work = jnp.where(lane < N_EXPERTS, logits, neg_inf)
    vals, idxs = [], []
    for _ in range(TOP_K):
        m = jnp.max(work, axis=-1, keepdims=True)
        idx = jnp.min(jnp.where(work == m, lane, LANES), axis=-1, keepdims=True)
        vals.append(m)
        idxs.append(idx)
        work = jnp.where(lane == idx, neg_inf, work)
    exps = [jnp.exp(v - vals[0]) for v in vals]
    denom = exps[0] + exps[1] + exps[2] + exps[3]

    r_i = lax.broadcasted_iota(I32, (tm, tm), 0)
    c_i = lax.broadcasted_iota(I32, (tm, tm), 1)
    lower = jnp.where(r_i > c_i, 1.0, 0.0).astype(BF16)
    base = jnp.zeros((1, LANES), F32)
    onehots, ranks = [], []
    for k in range(TOP_K):
        onehot = lane == idxs[k]
        before = jnp.dot(lower, jnp.where(onehot, 1.0, 0.0).astype(BF16), preferred_element_type=F32)
        onehots.append(onehot)
        ranks.append(before + base)
        base = base + jnp.sum(jnp.where(onehot, 1.0, 0.0), axis=0, keepdims=True)
    cnt8 = jnp.floor((base + (SUBLANES - 1)) * (1.0 / SUBLANES))
    e_i = lax.broadcasted_iota(I32, (LANES, LANES), 0)
    e_j = lax.broadcasted_iota(I32, (LANES, LANES), 1)
    upper = jnp.where(e_i < e_j, 1.0, 0.0).astype(BF16)
    seg8 = jnp.dot(jnp.broadcast_to(cnt8, (SUBLANES, LANES)).astype(BF16), upper,
                   preferred_element_type=F32)[0:1, :]
    seg_row = seg8 * float(SUBLANES)
    slot_out = jnp.full((tm, LANES), -1.0, F32)
    gate_out = jnp.zeros((tm, LANES), F32)
    for k in range(TOP_K):
        slot_k = jnp.sum(jnp.where(onehots[k], ranks[k] + seg_row, 0.0), axis=-1, keepdims=True)
        slot_out = jnp.where(lane == k, slot_k, slot_out)
        gate_out = jnp.where(lane == k, exps[k] / denom, gate_out)
    slot_ref[...] = slot_out.astype(I32)
    gate_ref[...] = gate_out
    cnt_ref[...] = jnp.broadcast_to(base, (SUBLANES, LANES))

    slot_t = slot_out.T
    p_i = lax.broadcasted_iota(I32, (slots, tm), 0).astype(F32)
    perm = jnp.where(p_i == slot_t[0:1, :], 1.0, 0.0)
    for k in range(1, TOP_K):
        perm = perm + jnp.where(p_i == slot_t[k:k + 1, :], 1.0, 0.0)
    perm = perm.astype(BF16)
    h2b = h2.astype(BF16)
    cw = 256
    for c in range(D_HALF // cw):
        lo = jnp.dot(perm, h2b[:, c * cw:(c + 1) * cw], preferred_element_type=F32)
        hi = jnp.dot(perm, h2b[:, D_HALF + c * cw:D_HALF + (c + 1) * cw], preferred_element_type=F32)
        xs_ref[:, c * cw:(c + 1) * cw] = _pack_rows(lo, hi)


def _merge(p_act, o_a, o_b, x2d, w_out, g_ffn, w_router_hi, w_router_lo, b_router):
    n = x2d.shape[0]
    tm = MERGE_TM
    nt = n // tm
    row = lambda c: pl.BlockSpec((tm, D_MODEL), lambda i: (i, c))
    const = lambda shape: pl.BlockSpec(shape, lambda i: (0, 0))
    aux = pl.BlockSpec((tm, LANES), lambda i: (i, 0))
    return pl.pallas_call(
        _merge_kernel,
        out_shape=(
            jax.ShapeDtypeStruct((n, D_MODEL), F32),
            jax.ShapeDtypeStruct((nt * PAIR * MERGE_SLOTS, D_HALF), BF16),
            jax.ShapeDtypeStruct((n, LANES), I32),
            jax.ShapeDtypeStruct((n, LANES), F32),
            jax.ShapeDtypeStruct((nt * SUBLANES, LANES), F32),
        ),
        grid=(nt,),
        in_specs=[
            row(COL_GA // D_MODEL), row(COL_GB // D_MODEL), row(COL_RB // D_MODEL),
            row(0), row(0), row(0),
            pl.BlockSpec((D_MODEL, D_MODEL), lambda i: (0, 0), pipeline_mode=pl.Buffered(1)),
            const((1, D_MODEL)),
            pl.BlockSpec((D_MODEL, LANES), lambda i: (0, 0), pipeline_mode=pl.Buffered(1)),
            pl.BlockSpec((D_MODEL, LANES), lambda i: (0, 0), pipeline_mode=pl.Buffered(1)),
            const((1, LANES)),
        ],
        out_specs=(row(0), pl.BlockSpec((PAIR * MERGE_SLOTS, D_HALF), lambda i: (i, 0)), aux, aux,
                   pl.BlockSpec((SUBLANES, LANES), lambda i: (i, 0))),
        compiler_params=_cparams(("arbitrary",)),
        name="merge",
    )(p_act, p_act, p_act, o_a, o_b, x2d, w_out, g_ffn, w_router_hi, w_router_lo, b_router)


def _experts_kernel(nitems_ref, ie_ref, it0_ref, it1_ref, ibase_ref, irows_ref, c8_ref, src8_ref, cum8_ref,
                    xs_ref, wg_ref, wl_ref, bg_ref, bl_ref, wdl_ref, wdh_ref, bdl_ref, bdh_ref,
                    ys_ref, bufs, act_scr, sem_in, sem_out):
    w = pl.program_id(0)
    s = pl.program_id(1)
    nfa = D_FF // EXPERT_TF
    nfb = D_HALF // EXPERT_TN
    n_items = nitems_ref[0]
    valid = w < n_items
    cur = lax.rem(w, 2)
    unit = PAIR * SUBLANES

    def for_each_copy(item, slot, outbound, fn):
        e = ie_ref[item]
        base = ibase_ref[item]
        buf = bufs.at[slot]

        def tile_body(t, carry):
            seg = t * N_EXPERTS + e
            c = c8_ref[seg]
            hbm0 = src8_ref[seg] * unit
            buf0 = (cum8_ref[seg] - base) * unit

            def copies(bits, off):
                for bit in bits:
                    rows = unit << bit
                    flag = lax.bitwise_and(lax.shift_right_logical(c, bit), 1)

                    @pl.when(flag == 1)
                    def _(rows=rows, off=off):
                        in_buf = buf.at[pl.ds(pl.multiple_of(buf0 + off, unit), rows), :]
                        if outbound:
                            fn(pltpu.make_async_copy(
                                in_buf, ys_ref.at[pl.ds(pl.multiple_of(hbm0 + off, unit), rows), :], sem_out))
                        else:
                            fn(pltpu.make_async_copy(
                                xs_ref.at[pl.ds(pl.multiple_of(hbm0 + off, unit), rows), :], in_buf, sem_in))

                    off = off + flag * rows

            low_bits = 3
            high = lax.shift_right_logical(c, low_bits)

            @pl.when(high > 0)
            def _():
                copies(range(SEG_BITS - 1, low_bits - 1, -1), jnp.zeros((), I32))

            copies(range(low_bits - 1, -1, -1), lax.shift_left(high, low_bits) * unit)
            return carry

        lax.fori_loop(it0_ref[item], it1_ref[item], tile_body, 0)

    start = lambda cp: cp.start()
    wait = lambda cp: cp.wait()

    @pl.when((w == 0) & (s == 0))
    def _():
        bufs[...] = jnp.zeros_like(bufs)
        act_scr[...] = jnp.zeros_like(act_scr)
        for_each_copy(0, 0, False, start)

    @pl.when(valid & (s == 0))
    def _():
        for_each_copy(w, cur, False, wait)

    n_sub = lax.div(irows_ref[w] * SUBLANES + (EXPERT_SUB - 1), EXPERT_SUB)

    @pl.when(valid & (s < nfa))
    def _():
        wg = wg_ref[...].astype(BF16)
        wl = wl_ref[...].astype(BF16)
        bg = bg_ref[...]
        bl = bl_ref[...]
        tile = jnp.minimum(s, nfa - 1)

        def sub_body(r, carry):
            z = bufs[cur, pl.ds(pl.multiple_of(r * (PAIR * EXPERT_SUB), PAIR * EXPERT_SUB),
                                PAIR * EXPERT_SUB), :]
            lo, hi = _unpack_rows(z)
            lo = lo.astype(BF16)
            hi = hi.astype(BF16)
            glu = (jnp.dot(lo, wg[0:D_HALF], preferred_element_type=F32)
                   + jnp.dot(hi, wg[D_HALF:D_MODEL], preferred_element_type=F32) + bg)
            lin = (jnp.dot(lo, wl[0:D_HALF], preferred_element_type=F32)
                   + jnp.dot(hi, wl[D_HALF:D_MODEL], preferred_element_type=F32) + bl)
            glu = jnp.minimum(glu, SWIGLU_LIMIT)
            lin = jnp.clip(lin, -SWIGLU_LIMIT, SWIGLU_LIMIT)
            act = glu * _sigmoid(SWIGLU_ALPHA * glu) * (lin + 1.0)
            act_scr[tile, pl.ds(pl.multiple_of(r * EXPERT_SUB, EXPERT_SUB), EXPERT_SUB), :] = act.astype(BF16)
            return carry

        lax.fori_loop(0, n_sub, sub_body, 0)

    @pl.when(valid & (s == nfa))
    def _():
        @pl.when(w >= 1)
        def _():
            for_each_copy(w - 1, 1 - cur, True, wait)

        @pl.when(w + 1 < n_items)
        def _():
            for_each_copy(w + 1, 1 - cur, False, start)

    @pl.when(valid & (s >= nfa))
    def _():
        wd = jnp.concatenate([wdl_ref[...], wdh_ref[...]], axis=1).astype(BF16)
        bd = jnp.concatenate([bdl_ref[...], bdh_ref[...]], axis=1)
        col = pl.multiple_of(jnp.maximum(s - nfa, 0) * EXPERT_TN, EXPERT_TN)

        def sub_body(r, carry):
            rows = pl.ds(pl.multiple_of(r * EXPERT_SUB, EXPERT_SUB), EXPERT_SUB)
            acc = jnp.dot(act_scr[0, rows, :], wd[0:EXPERT_TF], preferred_element_type=F32)
            for t in range(1, nfa):
                acc += jnp.dot(act_scr[t, rows, :], wd[t * EXPERT_TF:(t + 1) * EXPERT_TF],
                               preferred_element_type=F32)
            acc = acc + bd
            bufs[cur, pl.ds(pl.multiple_of(r * (PAIR * EXPERT_SUB), PAIR * EXPERT_SUB), PAIR * EXPERT_SUB),
                 pl.ds(col, EXPERT_TN)] = _pack_rows(acc[:, 0:EXPERT_TN], acc[:, EXPERT_TN:2 * EXPERT_TN])
            return carry

        lax.fori_loop(0, n_sub, sub_body, 0)

    @pl.when(valid & (s == nfa + nfb - 1))
    def _():
        for_each_copy(w, cur, True, start)

        @pl.when(w == n_items - 1)
        def _():
            for_each_copy(w, cur, True, wait)


def _experts(x_sorted, meta, n_items_max, w_gu, b_gu, w_down, b_down):
    nfa = D_FF // EXPERT_TF
    nfb = D_HALF // EXPERT_TN
    n_pre = len(meta)

    def fa(w, s, n_items):
        return jnp.where(w < n_items[0], jnp.minimum(s, nfa - 1), nfa - 1)

    def fb(w, s, n_items):
        return jnp.where(w < n_items[0], jnp.maximum(s - nfa, 0), nfb - 1)

    def wspec(shape, col):
        return pl.BlockSpec(shape, lambda w, s, *m: (m[1][w], 0, col(w, s, m[0])))

    any_spec = pl.BlockSpec(memory_space=pl.ANY)
    grid_spec = pltpu.PrefetchScalarGridSpec(
        num_scalar_prefetch=n_pre,
        grid=(n_items_max, nfa + nfb),
        in_specs=[
            any_spec,
            wspec((None, D_MODEL, EXPERT_TF), fa),
            wspec((None, D_MODEL, EXPERT_TF), lambda w, s, n: nfa + fa(w, s, n)),
            wspec((None, 1, EXPERT_TF), fa),
            wspec((None, 1, EXPERT_TF), lambda w, s, n: nfa + fa(w, s, n)),
            wspec((None, D_FF, EXPERT_TN), fb),
            wspec((None, D_FF, EXPERT_TN), lambda w, s, n: nfb + fb(w, s, n)),
            wspec((None, 1, EXPERT_TN), fb),
            wspec((None, 1, EXPERT_TN), lambda w, s, n: nfb + fb(w, s, n)),
        ],
        out_specs=any_spec,
        scratch_shapes=[pltpu.VMEM((2, PAIR * EXPERT_ROWS, D_HALF), BF16),
                        pltpu.VMEM((nfa, EXPERT_ROWS, EXPERT_TF), BF16),
                        pltpu.SemaphoreType.DMA, pltpu.SemaphoreType.DMA],
    )
    return pl.pallas_call(
        _experts_kernel,
        out_shape=jax.ShapeDtypeStruct(x_sorted.shape, x_sorted.dtype),
        grid_spec=grid_spec,
        input_output_aliases={n_pre: 0},
        compiler_params=_cparams(("arbitrary", "arbitrary")),
        name="experts",
    )(*meta, x_sorted, w_gu, w_gu, b_gu, b_gu, w_down, w_down, b_down, b_down)


def _combine_kernel(ys_ref, slot_ref, gate_ref, x2_ref, gfin_ref, o_ref):
    tm = x2_ref.shape[0]
    slots = ys_ref.shape[0] // PAIR
    slot = slot_ref[...]
    gates = gate_ref[...]
    p_j = lax.broadcasted_iota(I32, (tm, slots), 1)
    gmat = jnp.where(p_j == slot[:, 0:1], gates[:, 0:1], 0.0)
    for k in range(1, TOP_K):
        gmat = gmat + jnp.where(p_j == slot[:, k:k + 1], gates[:, k:k + 1], 0.0)
    gmat = gmat.astype(BF16)
    y_lo, y_hi = _unpack_rows(ys_ref[...])
    moe_lo = jnp.dot(gmat, y_lo.astype(BF16), preferred_element_type=F32)
    moe_hi = jnp.dot(gmat, y_hi.astype(BF16), preferred_element_type=F32)
    x = x2_ref[...] + jnp.concatenate([moe_lo, moe_hi], axis=1)
    ms = jnp.mean(x * x, axis=-1, keepdims=True)
    o_ref[...] = x * lax.rsqrt(ms + NORM_EPS) * gfin_ref[...]


def _combine(y_sorted, slots, gates, x2, g_final):
    n = x2.shape[0]
    tm = MERGE_TM
    aux = pl.BlockSpec((tm, LANES), lambda i: (i, 0))
    return pl.pallas_call(
        _combine_kernel,
        out_shape=jax.ShapeDtypeStruct((n, D_MODEL), F32),
        grid=(n // tm,),
        in_specs=[
            pl.BlockSpec((PAIR * MERGE_SLOTS, D_HALF), lambda i: (i, 0)),
            aux, aux,
            pl.BlockSpec((tm, D_MODEL), lambda i: (i, 0)),
            pl.BlockSpec((1, D_MODEL), lambda i: (0, 0)),
        ],
        out_specs=pl.BlockSpec((tm, D_MODEL), lambda i: (i, 0)),
        compiler_params=_cparams(("arbitrary",)),
        name="combine",
    )(y_sorted, slots, gates, x2, g_final)


def _layer(x2d, batch, seq, attn_norm_g, w_in, swa_sinks, w_alpha2, b_alpha, gla_norm_g, w_out,
           ffn_norm_g, w_router, b_router, w_gate_up, b_gate_up, w_down, b_down, final_norm_g):
    n = x2d.shape[0]
    assert n % MERGE_TM == 0 and seq % SWA_WINDOW == 0 and seq % GLA_CHUNK == 0
    o_qa, o_ka, o_va = 0, 2048, 2304
    o_qb, o_kb, o_vb, o_rb, o_al, o_ga, o_gb = 2560, 3584, 4608, 6656, 8704, 8720, 10768
    seg = lambda s, w: w_in[:, s:s + w]
    w_main = jnp.concatenate([
        seg(o_qa, 2048), seg(o_ga, 2048), seg(o_gb, 2048), seg(o_rb, 2048), seg(o_vb, 2048),
        seg(o_qb, 1024), seg(o_kb, 1024), seg(o_ka, 256), seg(o_va, 256)], axis=1).astype(BF16)
    w_a = jnp.pad(seg(o_al, GLA_GATE_RANK), ((0, 0), (0, LANES - GLA_GATE_RANK))).astype(BF16)
    w2_hi, w2_lo = _split_bf16(jnp.pad(w_alpha2, ((0, LANES - GLA_GATE_RANK), (0, 0))))
    wr_hi, wr_lo = _split_bf16(jnp.pad(w_router, ((0, 0), (0, LANES - N_EXPERTS))))
    b_r = jnp.pad(b_router, (0, LANES - N_EXPERTS)).reshape(1, LANES)
    slopes = jnp.exp2(-8.0 * jnp.arange(1, SWA_HEADS + 1, dtype=F32) / SWA_HEADS)

    p_act, a_lr = _inproj(x2d, attn_norm_g.reshape(1, D_MODEL), w_main, w_a)
    o_a = _swa(p_act, slopes, swa_sinks.astype(F32), batch, seq)
    o_b = _gla(p_act, a_lr, w2_hi, w2_lo, b_alpha.reshape(1, -1), gla_norm_g.reshape(1, -1), batch, seq)
    x2, x_sorted, slots, gates, cnt = _merge(
        p_act, o_a, o_b, x2d, w_out.astype(BF16), ffn_norm_g.reshape(1, D_MODEL), wr_hi, wr_lo, b_r)

    nt = n // MERGE_TM
    chunk8 = EXPERT_CHUNK // SUBLANES
    cnt_t = cnt[::SUBLANES, :N_EXPERTS].astype(I32)
    c8 = (cnt_t + SUBLANES - 1) // SUBLANES
    tile8 = (jnp.arange(nt, dtype=I32) * (MERGE_SLOTS // SUBLANES))[:, None]
    src8 = tile8 + jnp.cumsum(c8, axis=1) - c8
    cum8 = jnp.cumsum(c8, axis=0) - c8
    tot8 = jnp.sum(c8, axis=0)
    per_e = (tot8 + chunk8 - 1) // chunk8
    item_end = jnp.cumsum(per_e)
    n_items = item_end[-1]
    n_items_max = -(-(n * TOP_K // SUBLANES + nt * N_EXPERTS) // chunk8) + N_EXPERTS
    w_idx = jnp.arange(n_items_max, dtype=I32)
    w_eff = jnp.minimum(w_idx, n_items - 1)
    item_e = jnp.minimum(jnp.sum(item_end[None, :] <= w_eff[:, None], axis=1), N_EXPERTS - 1).astype(I32)
    item_j = w_eff - (item_end - per_e)[item_e]
    cum_e = cum8[:, item_e]
    item_t0 = jnp.sum(cum_e < (item_j * chunk8)[None, :], axis=0).astype(I32)
    item_t1 = jnp.sum(cum_e < ((item_j + 1) * chunk8)[None, :], axis=0).astype(I32)
    t0c = jnp.minimum(item_t0, nt - 1)
    t1c = jnp.maximum(item_t1 - 1, 0)
    item_base = cum8[t0c, item_e]
    item_rows = cum8[t1c, item_e] + c8[t1c, item_e] - item_base
    used = w_idx < n_items
    item_t1 = jnp.where(used, item_t1, item_t0)
    item_rows = jnp.where(used & (item_t1 > item_t0), item_rows, 0).astype(I32)
    meta = (n_items.astype(I32).reshape(1), item_e, item_t0, item_t1, item_base.astype(I32), item_rows,
            c8.reshape(-1), src8.reshape(-1), cum8.reshape(-1))

    y_sorted = _experts(x_sorted, meta, n_items_max, w_gate_up,
                        b_gate_up.reshape(N_EXPERTS, 1, 2 * D_FF), w_down,
                        b_down.reshape(N_EXPERTS, 1, D_MODEL))
    return _combine(y_sorted, slots, gates, x2, final_norm_g.reshape(1, D_MODEL))


def kernel(x, attn_norm_g, w_in, swa_sinks, w_alpha2, b_alpha, gla_norm_g, w_out, ffn_norm_g,
           w_router, b_router, w_gate_up, b_gate_up, w_down, b_down, final_norm_g):
    batch, seq, _ = x.shape
    assert w_in.shape[0] == 1, "single-layer block: the final norm is fused into the MoE combine"
    out = _layer(
        x.reshape(batch * seq, D_MODEL), batch, seq, attn_norm_g[0], w_in[0], swa_sinks[0],
        w_alpha2[0], b_alpha[0], gla_norm_g[0], w_out[0], ffn_norm_g[0], w_router[0], b_router[0],
        w_gate_up[0], b_gate_up[0], w_down[0], b_down[0], final_norm_g)
    return out.reshape(batch, seq, D_MODEL)
```

```python
import functools

import jax
import jax.numpy as jnp
from jax import lax
from jax.experimental import pallas as pl
from jax.experimental.pallas import tpu as pltpu

F32 = jnp.float32
BF16 = jnp.bfloat16
I32 = jnp.int32
U32 = jnp.uint32

D_MODEL = 2048
D_HALF = D_MODEL // 2
NORM_EPS = 1e-5
SWA_HEADS = 32
SWA_KV_HEADS = 4
SWA_HEAD_DIM = 64
SWA_WINDOW = 128
SWA_GROUP = SWA_HEADS // SWA_KV_HEADS
GLA_HEADS = 4
GLA_DK = 256
GLA_DV = 512
GLA_GATE_RANK = 16
GLA_TAU = 16.0
GLA_CHUNK = 64
GLA_GROUP = 256
N_EXPERTS = 32
TOP_K = 4
D_FF = 2048
SWIGLU_LIMIT = 7.0
SWIGLU_ALPHA = 1.702

LANES = 128
SUBLANES = 8
MASK_NEG = -1e30

COL_QA = 0
COL_GA = 2048
COL_GB = 4096
COL_RB = 6144
COL_VB = 8192
COL_QB = 10240
COL_KB = 11264
COL_KA = 12288
COL_VA = 12544
P_COLS = 12800

VMEM_LIMIT = 56 * 1024 * 1024

INPROJ_TM = 1024
INPROJ_TN = 1280
MERGE_TM = 256
MERGE_SLOTS = MERGE_TM * TOP_K + N_EXPERTS * SUBLANES
SEG_BITS = (MERGE_TM // SUBLANES).bit_length()
PAIR = 2
PACK_GROUP = 256
EXPERT_CHUNK = 2560
EXPERT_SUB = 592
EXPERT_ROWS = -(-(EXPERT_CHUNK + MERGE_TM) // EXPERT_SUB) * EXPERT_SUB
EXPERT_TF = 256
EXPERT_TN = 128


def _cparams(sem):
    return pltpu.CompilerParams(dimension_semantics=sem, vmem_limit_bytes=VMEM_LIMIT)


def _sigmoid(x):
    return 1.0 / (1.0 + jnp.exp(-x))


def _split_bf16(x):
    hi = x.astype(BF16)
    return hi, (x - hi.astype(F32)).astype(BF16)


def _dot_split(a, b_hi, b_lo):
    a_hi, a_lo = _split_bf16(a)
    return ((jnp.dot(a_hi, b_hi, preferred_element_type=F32)
             + jnp.dot(a_lo, b_hi, preferred_element_type=F32))
            + jnp.dot(a_hi, b_lo, preferred_element_type=F32))


def _split_cols(x):
    half = PACK_GROUP // 2
    groups = range(x.shape[1] // PACK_GROUP)
    return (jnp.concatenate([x[:, g * PACK_GROUP:g * PACK_GROUP + half] for g in groups], axis=1),
            jnp.concatenate([x[:, g * PACK_GROUP + half:(g + 1) * PACK_GROUP] for g in groups], axis=1))


def _merge_cols(lo, hi):
    half = PACK_GROUP // 2
    parts = []
    for g in range(lo.shape[1] // half):
        parts += [lo[:, g * half:(g + 1) * half], hi[:, g * half:(g + 1) * half]]
    return jnp.concatenate(parts, axis=1)


def _pack_rows(lo, hi):
    return pltpu.bitcast(pltpu.pack_elementwise([lo, hi], packed_dtype=BF16), BF16)


def _unpack_rows(z):
    words = pltpu.bitcast(z, U32)
    return (pltpu.unpack_elementwise(words, index=0, packed_dtype=BF16, unpacked_dtype=F32),
            pltpu.unpack_elementwise(words, index=1, packed_dtype=BF16, unpacked_dtype=F32))


def _inproj_kernel(x_ref, g_ref, w_ref, wa_ref, p_ref, a_ref, h_scr):
    @pl.when(pl.program_id(1) == 0)
    def _():
        x = x_ref[...]
        ms = jnp.mean(x * x, axis=-1, keepdims=True)
        h = (x * lax.rsqrt(ms + NORM_EPS) * g_ref[...]).astype(BF16)
        h_scr[...] = h
        a_ref[...] = jnp.dot(h, wa_ref[...], preferred_element_type=F32)

    p_ref[...] = jnp.dot(h_scr[...], w_ref[...], preferred_element_type=F32).astype(BF16)


def _inproj(x2d, g, w_main, w_a):
    n = x2d.shape[0]
    tm = min(INPROJ_TM, n)
    return pl.pallas_call(
        _inproj_kernel,
        out_shape=(jax.ShapeDtypeStruct((n, P_COLS), BF16),
                   jax.ShapeDtypeStruct((n, LANES), F32)),
        grid=(n // tm, P_COLS // INPROJ_TN),
        in_specs=[
            pl.BlockSpec((tm, D_MODEL), lambda i, j: (i, 0)),
            pl.BlockSpec((1, D_MODEL), lambda i, j: (0, 0)),
            pl.BlockSpec((D_MODEL, INPROJ_TN), lambda i, j: (0, j)),
            pl.BlockSpec((D_MODEL, LANES), lambda i, j: (0, 0)),
        ],
        out_specs=(
            pl.BlockSpec((tm, INPROJ_TN), lambda i, j: (i, j)),
            pl.BlockSpec((tm, LANES), lambda i, j: (i, 0)),
        ),
        scratch_shapes=[pltpu.VMEM((tm, D_MODEL), BF16)],
        compiler_params=_cparams(("arbitrary", "arbitrary")),
        name="inproj",
    )(x2d, g, w_main, w_a)


def _swa_kernel(slopes_ref, sinks_ref, q_ref, kp_ref, kc_ref, vp_ref, vc_ref, o_ref):
    n = pl.program_id(1)
    w = SWA_WINDOW
    hd = SWA_HEAD_DIM
    pairs = SWA_GROUP // 2
    lane = lax.broadcasted_iota(I32, (2 * w, LANES), 1)
    low = lane < hd
    kband = jnp.concatenate([kp_ref[...], kc_ref[...]], axis=0).astype(F32)
    vband = jnp.concatenate([vp_ref[...], vc_ref[...]], axis=0).astype(F32)
    qry = lax.broadcasted_iota(I32, (w, 2 * w), 0)
    key = lax.broadcasted_iota(I32, (w, 2 * w), 1)
    dist = qry - key + w
    valid = (dist >= 0) & (dist < w) & ((key >= w) | (n > 0))
    negd = jnp.where(valid, -dist.astype(F32), MASK_NEG)
    scale = jnp.asarray(hd ** -0.5, BF16)
    ones = jnp.ones((2 * w, LANES), BF16)
    out_lane_low = lax.broadcasted_iota(I32, (w, LANES), 1) < hd
    for kv in range(SWA_KV_HEADS):
        grp = slice((kv // 2) * LANES, (kv // 2 + 1) * LANES)
        kg, vg = kband[:, grp], vband[:, grp]
        kg_sw, vg_sw = pltpu.roll(kg, hd, axis=1), pltpu.roll(vg, hd, axis=1)
        if kv % 2 == 0:
            k_even, k_odd = jnp.where(low, kg, 0.0), jnp.where(low, 0.0, kg_sw)
            v_dup = jnp.where(low, vg, vg_sw)
        else:
            k_even, k_odd = jnp.where(low, kg_sw, 0.0), jnp.where(low, 0.0, kg)
            v_dup = jnp.where(low, vg_sw, vg)
        v_ext = jnp.concatenate([v_dup.astype(BF16), ones], axis=1)
        h0 = kv * SWA_GROUP
        q_pairs = jnp.concatenate(
            [q_ref[:, (h0 + 2 * i) * hd:(h0 + 2 * i + 2) * hd] for i in range(pairs)], axis=0) * scale
        nt = (((1,), (1,)), ((), ()))
        s_par = (lax.dot_general(q_pairs, k_even.astype(BF16), nt, preferred_element_type=F32),
                 lax.dot_general(q_pairs, k_odd.astype(BF16), nt, preferred_element_type=F32))
        probs, sink_terms = [], []
        for par in range(2):
            for i in range(pairs):
                h = h0 + 2 * i + par
                s = s_par[par][i * w:(i + 1) * w, :] + slopes_ref[h] * negd
                sink = sinks_ref[h]
                m = jnp.maximum(jnp.max(s, axis=-1, keepdims=True), sink)
                probs.append(jnp.exp(s - m).astype(BF16))
                sink_terms.append(jnp.exp(sink - m))
        o_ext = jnp.dot(jnp.concatenate(probs, axis=0), v_ext, preferred_element_type=F32)
        for i in range(pairs):
            halves = []
            for par in range(2):
                r0 = (par * pairs + i) * w
                blk = o_ext[r0:r0 + w, :]
                halves.append(blk[:, 0:LANES] / (blk[:, LANES:2 * LANES] + sink_terms[par * pairs + i]))
            c0 = (h0 + 2 * i) * hd
            o_ref[:, c0:c0 + LANES] = jnp.where(out_lane_low, halves[0], halves[1]).astype(BF16)


def _swa(p_act, slopes, sinks, batch, seq):
    n = batch * seq
    w = SWA_WINDOW
    nb = seq // w
    kvw = SWA_KV_HEADS * SWA_HEAD_DIM

    def cur(col):
        return lambda b, i: (b * nb + i, col)

    def prev(col):
        return lambda b, i: (b * nb + jnp.maximum(i - 1, 0), col)

    smem = pl.BlockSpec(memory_space=pltpu.SMEM)
    return pl.pallas_call(
        _swa_kernel,
        out_shape=jax.ShapeDtypeStruct((n, D_MODEL), BF16),
        grid=(batch, nb),
        in_specs=[
            smem, smem,
            pl.BlockSpec((w, D_MODEL), cur(COL_QA // D_MODEL)),
            pl.BlockSpec((w, kvw), prev(COL_KA // kvw)),
            pl.BlockSpec((w, kvw), cur(COL_KA // kvw)),
            pl.BlockSpec((w, kvw), prev(COL_VA // kvw)),
            pl.BlockSpec((w, kvw), cur(COL_VA // kvw)),
        ],
        out_specs=pl.BlockSpec((w, D_MODEL), lambda b, i: (b * nb + i, 0)),
        compiler_params=_cparams(("arbitrary", "arbitrary")),
        name="swa",
    )(slopes, sinks, p_act, p_act, p_act, p_act, p_act)


def _gla_kernel(q_ref, k_ref, v_ref, a_ref, w2h_ref, w2l_ref, ba_ref, gn_ref, o_ref,
                st_scr, qi_scr, ks_scr, dec_scr, intra_scr):
    c = GLA_CHUNK
    grp = GLA_GROUP
    cpg = grp // c
    seq = q_ref.shape[0]
    row = lax.broadcasted_iota(I32, (grp, grp), 0)
    col = lax.broadcasted_iota(I32, (grp, grp), 1)
    tri = jnp.where((row >= col) & (row // c == col // c), 1.0, 0.0).astype(BF16)
    causal = lax.broadcasted_iota(I32, (c, c), 0) >= lax.broadcasted_iota(I32, (c, c), 1)
    scale = jnp.asarray(GLA_DK ** -0.5, F32)

    def group(gi, carry):
        g0 = pl.multiple_of(gi * grp, grp)
        z = _dot_split(a_ref[pl.ds(g0, grp), :], w2h_ref[...], w2l_ref[...]) + ba_ref[...]
        log_a = (jnp.minimum(z, 0.0) - jnp.log1p(jnp.exp(-jnp.abs(z)))) * (1.0 / GLA_TAU)
        la_hi, la_mid = _split_bf16(log_a)
        la_lo = ((log_a - la_hi.astype(F32)) - la_mid.astype(F32)).astype(BF16)
        b_all = ((jnp.dot(tri, la_hi, preferred_element_type=F32)
                  + jnp.dot(tri, la_mid, preferred_element_type=F32))
                 + jnp.dot(tri, la_lo, preferred_element_type=F32))
        qf_all = q_ref[pl.ds(g0, grp), :].astype(F32) * scale
        kf_all = k_ref[pl.ds(g0, grp), :].astype(F32)
        atts = []
        for j in range(cpg):
            r0 = pl.multiple_of(g0 + j * c, c)
            b = b_all[j * c:(j + 1) * c, :]
            qf = qf_all[j * c:(j + 1) * c, :]
            kf = kf_all[j * c:(j + 1) * c, :]
            b_last = b[c - 1:c, :]
            b_mid = b[c // 2 - 1:c // 2, :]
            qi_scr[pl.ds(r0, c), :] = (qf * jnp.exp(b)).astype(BF16)
            ks_scr[pl.ds(r0, c), :] = (kf * jnp.exp(b_last - b)).astype(BF16)
            dec_scr[pl.ds(gi * cpg + j, 1), :] = jnp.exp(b_last)
            q_intra = (qf * jnp.exp(b - b_mid)).astype(BF16)
            k_intra = (kf * jnp.exp(b_mid - b)).astype(BF16)
            atts.append(lax.dot_general(q_intra, k_intra, (((1,), (1,)), ((), ())),
                                        preferred_element_type=F32))
        for j in range(cpg):
            r0 = pl.multiple_of(g0 + j * c, c)
            att = jnp.where(causal, atts[j], 0.0).astype(BF16)
            intra_scr[pl.ds(r0, c), :] = jnp.dot(att, v_ref[pl.ds(r0, c), :], preferred_element_type=F32)
        return carry

    lax.fori_loop(0, seq // grp, group, 0, unroll=2)

    st_scr[...] = jnp.zeros_like(st_scr)

    def scan_group(gi, carry):
        rows = [pl.multiple_of((gi * cpg + j) * c, c) for j in range(cpg)]
        upds = [lax.dot_general(v_ref[pl.ds(r0, c), :], ks_scr[pl.ds(r0, c), :], (((0,), (0,)), ((), ())),
                                preferred_element_type=F32) for r0 in rows]
        st = st_scr[...]
        for j, r0 in enumerate(rows):
            inter = lax.dot_general(qi_scr[pl.ds(r0, c), :], st.astype(BF16), (((1,), (1,)), ((), ())),
                                    preferred_element_type=F32)
            st = st * dec_scr[pl.ds(gi * cpg + j, 1), :] + upds[j]
            o = inter + intra_scr[pl.ds(r0, c), :]
            ms = jnp.mean(o * o, axis=-1, keepdims=True)
            o_ref[pl.ds(r0, c), :] = (o * lax.rsqrt(ms + NORM_EPS) * gn_ref[...]).astype(BF16)
        st_scr[...] = st
        return carry

    lax.fori_loop(0, seq // grp, scan_group, 0)


def _gla(p_act, a_lr, w2_hi, w2_lo, b_alpha, gla_norm_g, batch, seq):
    n = batch * seq
    assert seq % GLA_GROUP == 0 and GLA_GROUP % GLA_CHUNK == 0
    return pl.pallas_call(
        _gla_kernel,
        out_shape=jax.ShapeDtypeStruct((n, GLA_HEADS * GLA_DV), BF16),
        grid=(batch, GLA_HEADS),
        in_specs=[
            pl.BlockSpec((seq, GLA_DK), lambda b, h: (b, COL_QB // GLA_DK + h)),
            pl.BlockSpec((seq, GLA_DK), lambda b, h: (b, COL_KB // GLA_DK + h)),
            pl.BlockSpec((seq, GLA_DV), lambda b, h: (b, COL_VB // GLA_DV + h)),
            pl.BlockSpec((seq, LANES), lambda b, h: (b, 0)),
            pl.BlockSpec((LANES, GLA_DK), lambda b, h: (0, h)),
            pl.BlockSpec((LANES, GLA_DK), lambda b, h: (0, h)),
            pl.BlockSpec((1, GLA_DK), lambda b, h: (0, h)),
            pl.BlockSpec((1, GLA_DV), lambda b, h: (0, 0)),
        ],
        out_specs=pl.BlockSpec((seq, GLA_DV), lambda b, h: (b, h)),
        scratch_shapes=[pltpu.VMEM((GLA_DV, GLA_DK), F32),
                        pltpu.VMEM((seq, GLA_DK), BF16),
                        pltpu.VMEM((seq, GLA_DK), BF16),
                        pltpu.VMEM((seq // GLA_CHUNK, GLA_DK), F32),
                        pltpu.VMEM((seq, GLA_DV), F32)],
        compiler_params=_cparams(("arbitrary", "arbitrary")),
        name="gla",
    )(p_act, p_act, p_act, a_lr, w2_hi, w2_lo, b_alpha, gla_norm_g)


def _merge_kernel(ga_ref, gb_ref, rb_ref, oa_ref, ob_ref, x_ref, wout_ref, gffn_ref, wrh_ref, wrl_ref, br_ref,
                  x2_ref, xs_ref, slot_ref, gate_ref, cnt_ref):
    tm = x_ref.shape[0]
    slots = xs_ref.shape[0] // PAIR
    ga = ga_ref[...].astype(F32)
    gb = gb_ref[...].astype(F32)
    rb = rb_ref[...].astype(F32)
    ob = ob_ref[...].astype(F32) * (rb * _sigmoid(rb))
    mix = _sigmoid(ga) * oa_ref[...].astype(F32) + _sigmoid(gb) * ob
    x2 = x_ref[...] + jnp.dot(mix.astype(BF16), wout_ref[...], preferred_element_type=F32)
    x2_ref[...] = x2
    ms = jnp.mean(x2 * x2, axis=-1, keepdims=True)
    h2 = x2 * lax.rsqrt(ms + NORM_EPS) * gffn_ref[...]

    logits = _dot_split(h2, wrh_ref[...], wrl_ref[...]) + br_ref[...]
    lane = lax.broadcasted_iota(I32, (tm, LANES), 1)
    neg_inf = jnp.asarray(-jnp.inf, F32)
    work = jnp.where(lane < N_EXPERTS, logits, neg_inf)
    vals, idxs = [], []
    for _ in range(TOP_K):
        m = jnp.max(work, axis=-1, keepdims=True)
        idx = jnp.min(jnp.where(work == m, lane, LANES), axis=-1, keepdims=True)
        vals.append(m)
        idxs.append(idx)
        work = jnp.where(lane == idx, neg_inf, work)
    exps = [jnp.exp(v - vals[0]) for v in vals]
    denom = exps[0] + exps[1] + exps[2] + exps[3]

    r_i = lax.broadcasted_iota(I32, (tm, tm), 0)
    c_i = lax.broadcasted_iota(I32, (tm, tm), 1)
    lower = jnp.where(r_i > c_i, 1.0, 0.0).astype(BF16)
    base = jnp.zeros((1, LANES), F32)
    onehots, ranks = [], []
    for k in range(TOP_K):
        onehot = lane == idxs[k]
        before = jnp.dot(lower, jnp.where(onehot, 1.0, 0.0).astype(BF16), preferred_element_type=F32)
        onehots.append(onehot)
        ranks.append(before + base)
        base = base + jnp.sum(jnp.where(onehot, 1.0, 0.0), axis=0, keepdims=True)
    cnt8 = jnp.floor((base + (SUBLANES - 1)) * (1.0 / SUBLANES))
    e_i = lax.broadcasted_iota(I32, (LANES, LANES), 0)
    e_j = lax.broadcasted_iota(I32, (LANES, LANES), 1)
    upper = jnp.where(e_i < e_j, 1.0, 0.0).astype(BF16)
    seg8 = jnp.dot(jnp.broadcast_to(cnt8, (SUBLANES, LANES)).astype(BF16), upper,
                   preferred_element_type=F32)[0:1, :]
    seg_row = seg8 * float(SUBLANES)
    slot_out = jnp.full((tm, LANES), -1.0, F32)
    gate_out = jnp.zeros((tm, LANES), F32)
    for k in range(TOP_K):
        slot_k = jnp.sum(jnp.where(onehots[k], ranks[k] + seg_row, 0.0), axis=-1, keepdims=True)
        slot_out = jnp.where(lane == k, slot_k, slot_out)
        gate_out = jnp.where(lane == k, exps[k] / denom, gate_out)
    slot_ref[...] = slot_out.astype(I32)
    gate_ref[...] = gate_out
    cnt_ref[...] = jnp.broadcast_to(base, (SUBLANES, LANES))

    slot_t = slot_out.T
    p_i = lax.broadcasted_iota(I32, (slots, tm), 0).astype(F32)
    perm = jnp.where(p_i == slot_t[0:1, :], 1.0, 0.0)
    for k in range(1, TOP_K):
        perm = perm + jnp.where(p_i == slot_t[k:k + 1, :], 1.0, 0.0)
    perm = perm.astype(BF16)
    h2b = h2.astype(BF16)
    cw = 2 * PACK_GROUP
    for c in range(D_MODEL // cw):
        y = jnp.dot(perm, h2b[:, c * cw:(c + 1) * cw], preferred_element_type=F32)
        lo, hi = _split_cols(y)
        xs_ref[:, c * (cw // 2):(c + 1) * (cw // 2)] = _pack_rows(lo, hi)


def _merge(p_act, o_a, o_b, x2d, w_out, g_ffn, w_router_hi, w_router_lo, b_router):
    n = x2d.shape[0]
    tm = MERGE_TM
    nt = n // tm
    row = lambda c: pl.BlockSpec((tm, D_MODEL), lambda i: (i, c))
    const = lambda shape: pl.BlockSpec(shape, lambda i: (0, 0))
    aux = pl.BlockSpec((tm, LANES), lambda i: (i, 0))
    return pl.pallas_call(
        _merge_kernel,
        out_shape=(
            jax.ShapeDtypeStruct((n, D_MODEL), F32),
            jax.ShapeDtypeStruct((nt * PAIR * MERGE_SLOTS, D_HALF), BF16),
            jax.ShapeDtypeStruct((n, LANES), I32),
            jax.ShapeDtypeStruct((n, LANES), F32),
            jax.ShapeDtypeStruct((nt * SUBLANES, LANES), F32),
        ),
        grid=(nt,),
        in_specs=[
            row(COL_GA // D_MODEL), row(COL_GB // D_MODEL), row(COL_RB // D_MODEL),
            row(0), row(0), row(0),
            pl.BlockSpec((D_MODEL, D_MODEL), lambda i: (0, 0), pipeline_mode=pl.Buffered(1)),
            const((1, D_MODEL)),
            pl.BlockSpec((D_MODEL, LANES), lambda i: (0, 0), pipeline_mode=pl.Buffered(1)),
            pl.BlockSpec((D_MODEL, LANES), lambda i: (0, 0), pipeline_mode=pl.Buffered(1)),
            const((1, LANES)),
        ],
        out_specs=(row(0), pl.BlockSpec((PAIR * MERGE_SLOTS, D_HALF), lambda i: (i, 0)), aux, aux,
                   pl.BlockSpec((SUBLANES, LANES), lambda i: (i, 0))),
        compiler_params=_cparams(("arbitrary",)),
        name="merge",
    )(p_act, p_act, p_act, o_a, o_b, x2d, w_out, g_ffn, w_router_hi, w_router_lo, b_router)


def _experts_kernel(nitems_ref, ie_ref, it0_ref, it1_ref, ibase_ref, irows_ref, c8_ref, src8_ref, cum8_ref,
                    xs_ref, wg_ref, wl_ref, bg_ref, bl_ref, wd_ref, bd_ref,
                    ys_ref, bufs, act_scr, sem_in, sem_out):
    w = pl.program_id(0)
    s = pl.program_id(1)
    nfa = D_FF // EXPERT_TF
    nfb = D_HALF // EXPERT_TN
    n_items = nitems_ref[0]
    valid = w < n_items
    cur = lax.rem(w, 2)
    unit = PAIR * SUBLANES

    def for_each_copy(item, slot, outbound, fn):
        e = ie_ref[item]
        base = ibase_ref[item]
        buf = bufs.at[slot]

        def tile_body(t, carry):
            seg = t * N_EXPERTS + e
            c = c8_ref[seg]
            hbm0 = src8_ref[seg] * unit
            buf0 = (cum8_ref[seg] - base) * unit

            def copies(bits, off):
                for bit in bits:
                    rows = unit << bit
                    flag = lax.bitwise_and(lax.shift_right_logical(c, bit), 1)

                    @pl.when(flag == 1)
                    def _(rows=rows, off=off):
                        in_buf = buf.at[pl.ds(pl.multiple_of(buf0 + off, unit), rows), :]
                        if outbound:
                            fn(pltpu.make_async_copy(
                                in_buf, ys_ref.at[pl.ds(pl.multiple_of(hbm0 + off, unit), rows), :], sem_out))
                        else:
                            fn(pltpu.make_async_copy(
                                xs_ref.at[pl.ds(pl.multiple_of(hbm0 + off, unit), rows), :], in_buf, sem_in))

                    off = off + flag * rows

            low_bits = 3
            high = lax.shift_right_logical(c, low_bits)

            @pl.when(high > 0)
            def _():
                copies(range(SEG_BITS - 1, low_bits - 1, -1), jnp.zeros((), I32))

            copies(range(low_bits - 1, -1, -1), lax.shift_left(high, low_bits) * unit)
            return carry

        lax.fori_loop(it0_ref[item], it1_ref[item], tile_body, 0)

    start = lambda cp: cp.start()
    wait = lambda cp: cp.wait()

    @pl.when((w == 0) & (s == 0))
    def _():
        bufs[...] = jnp.zeros_like(bufs)
        act_scr[...] = jnp.zeros_like(act_scr)
        for_each_copy(0, 0, False, start)

    @pl.when(valid & (s == 0))
    def _():
        for_each_copy(w, cur, False, wait)

    n_sub = lax.div(irows_ref[w] * SUBLANES + (EXPERT_SUB - 1), EXPERT_SUB)

    @pl.when(valid & (s < nfa))
    def _():
        def packed_order(w_ref):
            half = PACK_GROUP // 2
            groups = range(D_MODEL // PACK_GROUP)
            return jnp.concatenate(
                [w_ref[g * PACK_GROUP:g * PACK_GROUP + half, :].astype(BF16) for g in groups]
                + [w_ref[g * PACK_GROUP + half:(g + 1) * PACK_GROUP, :].astype(BF16) for g in groups], axis=0)

        wg = packed_order(wg_ref)
        wl = packed_order(wl_ref)
        bg = bg_ref[...]
        bl = bl_ref[...]
        tile = jnp.minimum(s, nfa - 1)

        def sub_body(r, carry):
            z = bufs[cur, pl.ds(pl.multiple_of(r * (PAIR * EXPERT_SUB), PAIR * EXPERT_SUB),
                                PAIR * EXPERT_SUB), :]
            lo, hi = _unpack_rows(z)
            lo = lo.astype(BF16)
            hi = hi.astype(BF16)
            glu = (jnp.dot(lo, wg[0:D_HALF], preferred_element_type=F32)
                   + jnp.dot(hi, wg[D_HALF:D_MODEL], preferred_element_type=F32) + bg)
            lin = (jnp.dot(lo, wl[0:D_HALF], preferred_element_type=F32)
                   + jnp.dot(hi, wl[D_HALF:D_MODEL], preferred_element_type=F32) + bl)
            glu = jnp.minimum(glu, SWIGLU_LIMIT)
            lin = jnp.clip(lin, -SWIGLU_LIMIT, SWIGLU_LIMIT)
            act = glu * _sigmoid(SWIGLU_ALPHA * glu) * (lin + 1.0)
            act_scr[tile, pl.ds(pl.multiple_of(r * EXPERT_SUB, EXPERT_SUB), EXPERT_SUB), :] = act.astype(BF16)
            return carry

        lax.fori_loop(0, n_sub, sub_body, 0)

    @pl.when(valid & (s == nfa))
    def _():
        @pl.when(w >= 1)
        def _():
            for_each_copy(w - 1, 1 - cur, True, wait)

        @pl.when(w + 1 < n_items)
        def _():
            for_each_copy(w + 1, 1 - cur, False, start)

    @pl.when(valid & (s >= nfa))
    def _():
        wd = wd_ref[...].astype(BF16)
        bd = bd_ref[...]
        col = pl.multiple_of(jnp.maximum(s - nfa, 0) * EXPERT_TN, EXPERT_TN)

        def sub_body(r, carry):
            rows = pl.ds(pl.multiple_of(r * EXPERT_SUB, EXPERT_SUB), EXPERT_SUB)
            acc = jnp.dot(act_scr[0, rows, :], wd[0:EXPERT_TF], preferred_element_type=F32)
            for t in range(1, nfa):
                acc += jnp.dot(act_scr[t, rows, :], wd[t * EXPERT_TF:(t + 1) * EXPERT_TF],
                               preferred_element_type=F32)
            acc = acc + bd
            bufs[cur, pl.ds(pl.multiple_of(r * (PAIR * EXPERT_SUB), PAIR * EXPERT_SUB), PAIR * EXPERT_SUB),
                 pl.ds(col, EXPERT_TN)] = _pack_rows(acc[:, 0:EXPERT_TN], acc[:, EXPERT_TN:2 * EXPERT_TN])
            return carry

        lax.fori_loop(0, n_sub, sub_body, 0)

    @pl.when(valid & (s == nfa + nfb - 1))
    def _():
        for_each_copy(w, cur, True, start)

        @pl.when(w == n_items - 1)
        def _():
            for_each_copy(w, cur, True, wait)


def _experts(x_sorted, meta, n_items_max, w_gu, b_gu, w_down, b_down):
    nfa = D_FF // EXPERT_TF
    nfb = D_HALF // EXPERT_TN
    n_pre = len(meta)

    def fa(w, s, n_items):
        return jnp.where(w < n_items[0], jnp.minimum(s, nfa - 1), nfa - 1)

    def fb(w, s, n_items):
        return jnp.where(w < n_items[0], jnp.maximum(s - nfa, 0), nfb - 1)

    def wspec(shape, col):
        return pl.BlockSpec(shape, lambda w, s, *m: (m[1][w], 0, col(w, s, m[0])))

    any_spec = pl.BlockSpec(memory_space=pl.ANY)
    grid_spec = pltpu.PrefetchScalarGridSpec(
        num_scalar_prefetch=n_pre,
        grid=(n_items_max, nfa + nfb),
        in_specs=[
            any_spec,
            wspec((None, D_MODEL, EXPERT_TF), fa),
            wspec((None, D_MODEL, EXPERT_TF), lambda w, s, n: nfa + fa(w, s, n)),
            wspec((None, 1, EXPERT_TF), fa),
            wspec((None, 1, EXPERT_TF), lambda w, s, n: nfa + fa(w, s, n)),
            wspec((None, D_FF, 2 * EXPERT_TN), fb),
            wspec((None, 1, 2 * EXPERT_TN), fb),
        ],
        out_specs=any_spec,
        scratch_shapes=[pltpu.VMEM((2, PAIR * EXPERT_ROWS, D_HALF), BF16),
                        pltpu.VMEM((nfa, EXPERT_ROWS, EXPERT_TF), BF16),
                        pltpu.SemaphoreType.DMA, pltpu.SemaphoreType.DMA],
    )
    return pl.pallas_call(
        _experts_kernel,
        out_shape=jax.ShapeDtypeStruct(x_sorted.shape, x_sorted.dtype),
        grid_spec=grid_spec,
        input_output_aliases={n_pre: 0},
        compiler_params=_cparams(("arbitrary", "arbitrary")),
        name="experts",
    )(*meta, x_sorted, w_gu, w_gu, b_gu, b_gu, w_down, b_down)


def _combine_kernel(ys_ref, slot_ref, gate_ref, x2_ref, gfin_ref, o_ref):
    tm = x2_ref.shape[0]
    slots = ys_ref.shape[0] // PAIR
    slot = slot_ref[...]
    gates = gate_ref[...]
    p_j = lax.broadcasted_iota(I32, (tm, slots), 1)
    gmat = jnp.where(p_j == slot[:, 0:1], gates[:, 0:1], 0.0)
    for k in range(1, TOP_K):
        gmat = gmat + jnp.where(p_j == slot[:, k:k + 1], gates[:, k:k + 1], 0.0)
    gmat = gmat.astype(BF16)
    y_lo, y_hi = _unpack_rows(ys_ref[...])
    moe_lo = jnp.dot(gmat, y_lo.astype(BF16), preferred_element_type=F32)
    moe_hi = jnp.dot(gmat, y_hi.astype(BF16), preferred_element_type=F32)
    x = x2_ref[...] + _merge_cols(moe_lo, moe_hi)
    ms = jnp.mean(x * x, axis=-1, keepdims=True)
    o_ref[...] = x * lax.rsqrt(ms + NORM_EPS) * gfin_ref[...]


def _combine(y_sorted, slots, gates, x2, g_final):
    n = x2.shape[0]
    tm = MERGE_TM
    aux = pl.BlockSpec((tm, LANES), lambda i: (i, 0))
    return pl.pallas_call(
        _combine_kernel,
        out_shape=jax.ShapeDtypeStruct((n, D_MODEL), F32),
        grid=(n // tm,),
        in_specs=[
            pl.BlockSpec((PAIR * MERGE_SLOTS, D_HALF), lambda i: (i, 0)),
            aux, aux,
            pl.BlockSpec((tm, D_MODEL), lambda i: (i, 0)),
            pl.BlockSpec((1, D_MODEL), lambda i: (0, 0)),
        ],
        out_specs=pl.BlockSpec((tm, D_MODEL), lambda i: (i, 0)),
        compiler_params=_cparams(("arbitrary",)),
        name="combine",
    )(y_sorted, slots, gates, x2, g_final)


def _layer(x2d, batch, seq, attn_norm_g, w_in, swa_sinks, w_alpha2, b_alpha, gla_norm_g, w_out,
           ffn_norm_g, w_router, b_router, w_gate_up, b_gate_up, w_down, b_down, final_norm_g):
    n = x2d.shape[0]
    assert n % MERGE_TM == 0 and seq % SWA_WINDOW == 0 and seq % GLA_CHUNK == 0
    o_qa, o_ka, o_va = 0, 2048, 2304
    o_qb, o_kb, o_vb, o_rb, o_al, o_ga, o_gb = 2560, 3584, 4608, 6656, 8704, 8720, 10768
    seg = lambda s, w: w_in[:, s:s + w]
    w_main = jnp.concatenate([
        seg(o_qa, 2048), seg(o_ga, 2048), seg(o_gb, 2048), seg(o_rb, 2048), seg(o_vb, 2048),
        seg(o_qb, 1024), seg(o_kb, 1024), seg(o_ka, 256), seg(o_va, 256)], axis=1).astype(BF16)
    w_a = jnp.pad(seg(o_al, GLA_GATE_RANK), ((0, 0), (0, LANES - GLA_GATE_RANK))).astype(BF16)
    w2_hi, w2_lo = _split_bf16(jnp.pad(w_alpha2, ((0, LANES - GLA_GATE_RANK), (0, 0))))
    wr_hi, wr_lo = _split_bf16(jnp.pad(w_router, ((0, 0), (0, LANES - N_EXPERTS))))
    b_r = jnp.pad(b_router, (0, LANES - N_EXPERTS)).reshape(1, LANES)
    slopes = jnp.exp2(-8.0 * jnp.arange(1, SWA_HEADS + 1, dtype=F32) / SWA_HEADS)

    p_act, a_lr = _inproj(x2d, attn_norm_g.reshape(1, D_MODEL), w_main, w_a)
    o_a = _swa(p_act, slopes, swa_sinks.astype(F32), batch, seq)
    o_b = _gla(p_act, a_lr, w2_hi, w2_lo, b_alpha.reshape(1, -1), gla_norm_g.reshape(1, -1), batch, seq)
    x2, x_sorted, slots, gates, cnt = _merge(
        p_act, o_a, o_b, x2d, w_out.astype(BF16), ffn_norm_g.reshape(1, D_MODEL), wr_hi, wr_lo, b_r)

    nt = n // MERGE_TM
    chunk8 = EXPERT_CHUNK // SUBLANES
    cnt_t = cnt[::SUBLANES, :N_EXPERTS].astype(I32)
    c8 = (cnt_t + SUBLANES - 1) // SUBLANES
    tile8 = (jnp.arange(nt, dtype=I32) * (MERGE_SLOTS // SUBLANES))[:, None]
    src8 = tile8 + jnp.cumsum(c8, axis=1) - c8
    cum8 = jnp.cumsum(c8, axis=0) - c8
    tot8 = jnp.sum(c8, axis=0)
    per_e = (tot8 + chunk8 - 1) // chunk8
    item_end = jnp.cumsum(per_e)
    n_items = item_end[-1]
    n_items_max = -(-(n * TOP_K // SUBLANES + nt * N_EXPERTS) // chunk8) + N_EXPERTS
    w_idx = jnp.arange(n_items_max, dtype=I32)
    w_eff = jnp.minimum(w_idx, n_items - 1)
    item_e = jnp.minimum(jnp.sum(item_end[None, :] <= w_eff[:, None], axis=1), N_EXPERTS - 1).astype(I32)
    item_j = w_eff - (item_end - per_e)[item_e]
    cum_e = cum8[:, item_e]
    item_t0 = jnp.sum(cum_e < (item_j * chunk8)[None, :], axis=0).astype(I32)
    item_t1 = jnp.sum(cum_e < ((item_j + 1) * chunk8)[None, :], axis=0).astype(I32)
    t0c = jnp.minimum(item_t0, nt - 1)
    t1c = jnp.maximum(item_t1 - 1, 0)
    item_base = cum8[t0c, item_e]
    item_rows = cum8[t1c, item_e] + c8[t1c, item_e] - item_base
    used = w_idx < n_items
    item_t1 = jnp.where(used, item_t1, item_t0)
    item_rows = jnp.where(used & (item_t1 > item_t0), item_rows, 0).astype(I32)
    meta = (n_items.astype(I32).reshape(1), item_e, item_t0, item_t1, item_base.astype(I32), item_rows,
            c8.reshape(-1), src8.reshape(-1), cum8.reshape(-1))

    y_sorted = _experts(x_sorted, meta, n_items_max, w_gate_up,
                        b_gate_up.reshape(N_EXPERTS, 1, 2 * D_FF), w_down,
                        b_down.reshape(N_EXPERTS, 1, D_MODEL))
    return _combine(y_sorted, slots, gates, x2, final_norm_g.reshape(1, D_MODEL))


def kernel(x, attn_norm_g, w_in, swa_sinks, w_alpha2, b_alpha, gla_norm_g, w_out, ffn_norm_g,
           w_router, b_router, w_gate_up, b_gate_up, w_down, b_down, final_norm_g):
    batch, seq, _ = x.shape
    assert w_in.shape[0] == 1, "single-layer block: the final norm is fused into the MoE combine"
    out = _layer(
        x.reshape(batch * seq, D_MODEL), batch, seq, attn_norm_g[0], w_in[0], swa_sinks[0],
        w_alpha2[0], b_alpha[0], gla_norm_g[0], w_out[0], ffn_norm_g[0], w_router[0], b_router[0],
        w_gate_up[0], b_gate_up[0], w_down[0], b_down[0], final_norm_g)
    return out.reshape(batch, seq, D_MODEL)
```

```python
import functools

import jax
import jax.numpy as jnp
from jax import lax
from jax.experimental import pallas as pl
from jax.experimental.pallas import tpu as pltpu

F32 = jnp.float32
BF16 = jnp.bfloat16
I32 = jnp.int32
U32 = jnp.uint32

D_MODEL = 2048
D_HALF = D_MODEL // 2
NORM_EPS = 1e-5
SWA_HEADS = 32
SWA_KV_HEADS = 4
SWA_HEAD_DIM = 64
SWA_WINDOW = 128
SWA_GROUP = SWA_HEADS // SWA_KV_HEADS
GLA_HEADS = 4
GLA_DK = 256
GLA_DV = 512
GLA_GATE_RANK = 16
GLA_TAU = 16.0
GLA_CHUNK = 64
GLA_GROUP = 256
N_EXPERTS = 32
TOP_K = 4
D_FF = 2048
SWIGLU_LIMIT = 7.0
SWIGLU_ALPHA = 1.702

LANES = 128
SUBLANES = 8
MASK_NEG = -1e30

COL_QA = 0
COL_GA = 2048
COL_GB = 4096
COL_RB = 6144
COL_VB = 8192
COL_QB = 10240
COL_KB = 11264
COL_KA = 12288
COL_VA = 12544
P_COLS = 12800

VMEM_LIMIT = 56 * 1024 * 1024

INPROJ_TM = 1024
INPROJ_TN = 1280
MERGE_TM = 256
MERGE_SLOTS = MERGE_TM * TOP_K + N_EXPERTS * SUBLANES
SEG_QUAD = 4
PAIR = 2
PACK_GROUP = 256
EXPERT_CHUNK = 2560
EXPERT_SUB = 592
EXPERT_ROWS = -(-(EXPERT_CHUNK + MERGE_TM) // EXPERT_SUB) * EXPERT_SUB
EXPERT_TF = 256
EXPERT_TN = 128


def _cparams(sem):
    return pltpu.CompilerParams(dimension_semantics=sem, vmem_limit_bytes=VMEM_LIMIT)


def _sigmoid(x):
    return 1.0 / (1.0 + jnp.exp(-x))


def _split_bf16(x):
    hi = x.astype(BF16)
    return hi, (x - hi.astype(F32)).astype(BF16)


def _dot_split(a, b_hi, b_lo):
    a_hi, a_lo = _split_bf16(a)
    return ((jnp.dot(a_hi, b_hi, preferred_element_type=F32)
             + jnp.dot(a_lo, b_hi, preferred_element_type=F32))
            + jnp.dot(a_hi, b_lo, preferred_element_type=F32))


def _split_cols(x):
    half = PACK_GROUP // 2
    groups = range(x.shape[1] // PACK_GROUP)
    return (jnp.concatenate([x[:, g * PACK_GROUP:g * PACK_GROUP + half] for g in groups], axis=1),
            jnp.concatenate([x[:, g * PACK_GROUP + half:(g + 1) * PACK_GROUP] for g in groups], axis=1))


def _merge_cols(lo, hi):
    half = PACK_GROUP // 2
    parts = []
    for g in range(lo.shape[1] // half):
        parts += [lo[:, g * half:(g + 1) * half], hi[:, g * half:(g + 1) * half]]
    return jnp.concatenate(parts, axis=1)


def _pack_rows(lo, hi):
    return pltpu.bitcast(pltpu.pack_elementwise([lo, hi], packed_dtype=BF16), BF16)


def _unpack_rows(z):
    words = pltpu.bitcast(z, U32)
    return (pltpu.unpack_elementwise(words, index=0, packed_dtype=BF16, unpacked_dtype=F32),
            pltpu.unpack_elementwise(words, index=1, packed_dtype=BF16, unpacked_dtype=F32))


def _inproj_kernel(x_ref, g_ref, w_ref, wa_ref, p_ref, a_ref, h_scr):
    @pl.when(pl.program_id(1) == 0)
    def _():
        x = x_ref[...]
        ms = jnp.mean(x * x, axis=-1, keepdims=True)
        h = (x * lax.rsqrt(ms + NORM_EPS) * g_ref[...]).astype(BF16)
        h_scr[...] = h
        a_ref[...] = jnp.dot(h, wa_ref[...], preferred_element_type=F32)

    p_ref[...] = jnp.dot(h_scr[...], w_ref[...], preferred_element_type=F32).astype(BF16)


def _inproj(x2d, g, w_main, w_a):
    n = x2d.shape[0]
    tm = min(INPROJ_TM, n)
    return pl.pallas_call(
        _inproj_kernel,
        out_shape=(jax.ShapeDtypeStruct((n, P_COLS), BF16),
                   jax.ShapeDtypeStruct((n, LANES), F32)),
        grid=(n // tm, P_COLS // INPROJ_TN),
        in_specs=[
            pl.BlockSpec((tm, D_MODEL), lambda i, j: (i, 0)),
            pl.BlockSpec((1, D_MODEL), lambda i, j: (0, 0)),
            pl.BlockSpec((D_MODEL, INPROJ_TN), lambda i, j: (0, j)),
            pl.BlockSpec((D_MODEL, LANES), lambda i, j: (0, 0)),
        ],
        out_specs=(
            pl.BlockSpec((tm, INPROJ_TN), lambda i, j: (i, j)),
            pl.BlockSpec((tm, LANES), lambda i, j: (i, 0)),
        ),
        scratch_shapes=[pltpu.VMEM((tm, D_MODEL), BF16)],
        compiler_params=_cparams(("arbitrary", "arbitrary")),
        name="inproj",
    )(x2d, g, w_main, w_a)


def _swa_kernel(slopes_ref, sinks_ref, q_ref, kp_ref, kc_ref, vp_ref, vc_ref, o_ref):
    n = pl.program_id(1)
    w = SWA_WINDOW
    hd = SWA_HEAD_DIM
    pairs = SWA_GROUP // 2
    lane = lax.broadcasted_iota(I32, (2 * w, LANES), 1)
    low = lane < hd
    kband = jnp.concatenate([kp_ref[...], kc_ref[...]], axis=0).astype(F32)
    vband = jnp.concatenate([vp_ref[...], vc_ref[...]], axis=0).astype(F32)
    qry = lax.broadcasted_iota(I32, (w, 2 * w), 0)
    key = lax.broadcasted_iota(I32, (w, 2 * w), 1)
    dist = qry - key + w
    valid = (dist >= 0) & (dist < w) & ((key >= w) | (n > 0))
    negd = jnp.where(valid, -dist.astype(F32), MASK_NEG)
    scale = jnp.asarray(hd ** -0.5, BF16)
    ones = jnp.ones((2 * w, LANES), BF16)
    out_lane_low = lax.broadcasted_iota(I32, (w, LANES), 1) < hd
    for kv in range(SWA_KV_HEADS):
        grp = slice((kv // 2) * LANES, (kv // 2 + 1) * LANES)
        kg, vg = kband[:, grp], vband[:, grp]
        kg_sw, vg_sw = pltpu.roll(kg, hd, axis=1), pltpu.roll(vg, hd, axis=1)
        if kv % 2 == 0:
            k_even, k_odd = jnp.where(low, kg, 0.0), jnp.where(low, 0.0, kg_sw)
            v_dup = jnp.where(low, vg, vg_sw)
        else:
            k_even, k_odd = jnp.where(low, kg_sw, 0.0), jnp.where(low, 0.0, kg)
            v_dup = jnp.where(low, vg_sw, vg)
        v_ext = jnp.concatenate([v_dup.astype(BF16), ones], axis=1)
        h0 = kv * SWA_GROUP
        q_pairs = jnp.concatenate(
            [q_ref[:, (h0 + 2 * i) * hd:(h0 + 2 * i + 2) * hd] for i in range(pairs)], axis=0) * scale
        nt = (((1,), (1,)), ((), ()))
        s_par = (lax.dot_general(q_pairs, k_even.astype(BF16), nt, preferred_element_type=F32),
                 lax.dot_general(q_pairs, k_odd.astype(BF16), nt, preferred_element_type=F32))
        probs, sink_terms = [], []
        for par in range(2):
            for i in range(pairs):
                h = h0 + 2 * i + par
                s = s_par[par][i * w:(i + 1) * w, :] + slopes_ref[h] * negd
                sink = sinks_ref[h]
                m = jnp.maximum(jnp.max(s, axis=-1, keepdims=True), sink)
                probs.append(jnp.exp(s - m).astype(BF16))
                sink_terms.append(jnp.exp(sink - m))
        o_ext = jnp.dot(jnp.concatenate(probs, axis=0), v_ext, preferred_element_type=F32)
        for i in range(pairs):
            halves = []
            for par in range(2):
                r0 = (par * pairs + i) * w
                blk = o_ext[r0:r0 + w, :]
                halves.append(blk[:, 0:LANES] / (blk[:, LANES:2 * LANES] + sink_terms[par * pairs + i]))
            c0 = (h0 + 2 * i) * hd
            o_ref[:, c0:c0 + LANES] = jnp.where(out_lane_low, halves[0], halves[1]).astype(BF16)


def _swa(p_act, slopes, sinks, batch, seq):
    n = batch * seq
    w = SWA_WINDOW
    nb = seq // w
    kvw = SWA_KV_HEADS * SWA_HEAD_DIM

    def cur(col):
        return lambda b, i: (b * nb + i, col)

    def prev(col):
        return lambda b, i: (b * nb + jnp.maximum(i - 1, 0), col)

    smem = pl.BlockSpec(memory_space=pltpu.SMEM)
    return pl.pallas_call(
        _swa_kernel,
        out_shape=jax.ShapeDtypeStruct((n, D_MODEL), BF16),
        grid=(batch, nb),
        in_specs=[
            smem, smem,
            pl.BlockSpec((w, D_MODEL), cur(COL_QA // D_MODEL)),
            pl.BlockSpec((w, kvw), prev(COL_KA // kvw)),
            pl.BlockSpec((w, kvw), cur(COL_KA // kvw)),
            pl.BlockSpec((w, kvw), prev(COL_VA // kvw)),
            pl.BlockSpec((w, kvw), cur(COL_VA // kvw)),
        ],
        out_specs=pl.BlockSpec((w, D_MODEL), lambda b, i: (b * nb + i, 0)),
        compiler_params=_cparams(("arbitrary", "arbitrary")),
        name="swa",
    )(slopes, sinks, p_act, p_act, p_act, p_act, p_act)


def _gla_kernel(q_ref, k_ref, v_ref, a_ref, w2h_ref, w2l_ref, ba_ref, gn_ref, o_ref,
                st_scr, qi_scr, ks_scr, dec_scr, intra_scr):
    c = GLA_CHUNK
    grp = GLA_GROUP
    cpg = grp // c
    seq = q_ref.shape[0]
    row = lax.broadcasted_iota(I32, (grp, grp), 0)
    col = lax.broadcasted_iota(I32, (grp, grp), 1)
    same_chunk_causal = (row >= col) & (row // c == col // c)
    tri = jnp.where(same_chunk_causal, 1.0, 0.0).astype(BF16)
    scale = jnp.asarray(GLA_DK ** -0.5, F32)

    def group(gi, carry):
        g0 = pl.multiple_of(gi * grp, grp)
        z = _dot_split(a_ref[pl.ds(g0, grp), :], w2h_ref[...], w2l_ref[...]) + ba_ref[...]
        log_a = (jnp.minimum(z, 0.0) - jnp.log1p(jnp.exp(-jnp.abs(z)))) * (1.0 / GLA_TAU)
        la_hi, la_mid = _split_bf16(log_a)
        la_lo = ((log_a - la_hi.astype(F32)) - la_mid.astype(F32)).astype(BF16)
        b3 = jnp.dot(tri, jnp.concatenate([la_hi, la_mid, la_lo], axis=1), preferred_element_type=F32)
        b_all = (b3[:, 0:GLA_DK] + b3[:, GLA_DK:2 * GLA_DK]) + b3[:, 2 * GLA_DK:3 * GLA_DK]
        qf_all = q_ref[pl.ds(g0, grp), :].astype(F32) * scale
        kf_all = k_ref[pl.ds(g0, grp), :].astype(F32)
        q_parts, k_parts = [], []
        for j in range(cpg):
            r0 = pl.multiple_of(g0 + j * c, c)
            b = b_all[j * c:(j + 1) * c, :]
            qf = qf_all[j * c:(j + 1) * c, :]
            kf = kf_all[j * c:(j + 1) * c, :]
            b_last = b[c - 1:c, :]
            b_mid = b[c // 2 - 1:c // 2, :]
            qi_scr[pl.ds(r0, c), :] = (qf * jnp.exp(b)).astype(BF16)
            ks_scr[pl.ds(r0, c), :] = (kf * jnp.exp(b_last - b)).astype(BF16)
            dec_scr[pl.ds(gi * cpg + j, 1), :] = jnp.exp(b_last)
            q_parts.append((qf * jnp.exp(b - b_mid)).astype(BF16))
            k_parts.append((kf * jnp.exp(b_mid - b)).astype(BF16))
        att = lax.dot_general(jnp.concatenate(q_parts, axis=0), jnp.concatenate(k_parts, axis=0),
                              (((1,), (1,)), ((), ())), preferred_element_type=F32)
        att = jnp.where(same_chunk_causal, att, 0.0).astype(BF16)
        intra_scr[pl.ds(g0, grp), :] = jnp.dot(att, v_ref[pl.ds(g0, grp), :], preferred_element_type=F32)
        return carry

    lax.fori_loop(0, seq // grp, group, 0, unroll=2)

    st_scr[...] = jnp.zeros_like(st_scr)

    def scan_group(gi, carry):
        rows = [pl.multiple_of((gi * cpg + j) * c, c) for j in range(cpg)]
        upds = [lax.dot_general(v_ref[pl.ds(r0, c), :], ks_scr[pl.ds(r0, c), :], (((0,), (0,)), ((), ())),
                                preferred_element_type=F32) for r0 in rows]
        st = st_scr[...]
        for j, r0 in enumerate(rows):
            inter = lax.dot_general(qi_scr[pl.ds(r0, c), :], st.astype(BF16), (((1,), (1,)), ((), ())),
                                    preferred_element_type=F32)
            st = st * dec_scr[pl.ds(gi * cpg + j, 1), :] + upds[j]
            o = inter + intra_scr[pl.ds(r0, c), :]
            ms = jnp.mean(o * o, axis=-1, keepdims=True)
            o_ref[pl.ds(r0, c), :] = (o * lax.rsqrt(ms + NORM_EPS) * gn_ref[...]).astype(BF16)
        st_scr[...] = st
        return carry

    lax.fori_loop(0, seq // grp, scan_group, 0)


def _gla(p_act, a_lr, w2_hi, w2_lo, b_alpha, gla_norm_g, batch, seq):
    n = batch * seq
    assert seq % GLA_GROUP == 0 and GLA_GROUP % GLA_CHUNK == 0
    return pl.pallas_call(
        _gla_kernel,
        out_shape=jax.ShapeDtypeStruct((n, GLA_HEADS * GLA_DV), BF16),
        grid=(batch, GLA_HEADS),
        in_specs=[
            pl.BlockSpec((seq, GLA_DK), lambda b, h: (b, COL_QB // GLA_DK + h)),
            pl.BlockSpec((seq, GLA_DK), lambda b, h: (b, COL_KB // GLA_DK + h)),
            pl.BlockSpec((seq, GLA_DV), lambda b, h: (b, COL_VB // GLA_DV + h)),
            pl.BlockSpec((seq, LANES), lambda b, h: (b, 0)),
            pl.BlockSpec((LANES, GLA_DK), lambda b, h: (0, h)),
            pl.BlockSpec((LANES, GLA_DK), lambda b, h: (0, h)),
            pl.BlockSpec((1, GLA_DK), lambda b, h: (0, h)),
            pl.BlockSpec((1, GLA_DV), lambda b, h: (0, 0)),
        ],
        out_specs=pl.BlockSpec((seq, GLA_DV), lambda b, h: (b, h)),
        scratch_shapes=[pltpu.VMEM((GLA_DV, GLA_DK), F32),
                        pltpu.VMEM((seq, GLA_DK), BF16),
                        pltpu.VMEM((seq, GLA_DK), BF16),
                        pltpu.VMEM((seq // GLA_CHUNK, GLA_DK), F32),
                        pltpu.VMEM((seq, GLA_DV), F32)],
        compiler_params=_cparams(("arbitrary", "arbitrary")),
        name="gla",
    )(p_act, p_act, p_act, a_lr, w2_hi, w2_lo, b_alpha, gla_norm_g)


def _merge_kernel(ga_ref, gb_ref, rb_ref, oa_ref, ob_ref, x_ref, wout_ref, gffn_ref, wrh_ref, wrl_ref, br_ref,
                  x2_ref, xs_ref, slot_ref, gate_ref, cnt_ref):
    tm = x_ref.shape[0]
    slots = xs_ref.shape[0] // PAIR
    ga = ga_ref[...].astype(F32)
    gb = gb_ref[...].astype(F32)
    rb = rb_ref[...].astype(F32)
    ob = ob_ref[...].astype(F32) * (rb * _sigmoid(rb))
    mix = _sigmoid(ga) * oa_ref[...].astype(F32) + _sigmoid(gb) * ob
    x2 = x_ref[...] + jnp.dot(mix.astype(BF16), wout_ref[...], preferred_element_type=F32)
    x2_ref[...] = x2
    ms = jnp.mean(x2 * x2, axis=-1, keepdims=True)
    h2 = x2 * lax.rsqrt(ms + NORM_EPS) * gffn_ref[...]

    logits = _dot_split(h2, wrh_ref[...], wrl_ref[...]) + br_ref[...]
    lane = lax.broadcasted_iota(I32, (tm, LANES), 1)
    neg_inf = jnp.asarray(-jnp.inf, F32)
    work = jnp.where(lane < N_EXPERTS, logits, neg_inf)
    vals, idxs = [], []
    for _ in range(TOP_K):
        m = jnp.max(work, axis=-1, keepdims=True)
        idx = jnp.min(jnp.where(work == m, lane, LANES), axis=-1, keepdims=True)
        vals.append(m)
        idxs.append(idx)
        work = jnp.where(lane == idx, neg_inf, work)
    exps = [jnp.exp(v - vals[0]) for v in vals]
    denom = exps[0] + exps[1] + exps[2] + exps[3]

    r_i = lax.broadcasted_iota(I32, (tm, tm), 0)
    c_i = lax.broadcasted_iota(I32, (tm, tm), 1)
    lower = jnp.where(r_i > c_i, 1.0, 0.0).astype(BF16)
    base = jnp.zeros((1, LANES), F32)
    onehots, ranks = [], []
    for k in range(TOP_K):
        onehot = lane == idxs[k]
        before = jnp.dot(lower, jnp.where(onehot, 1.0, 0.0).astype(BF16), preferred_element_type=F32)
        onehots.append(onehot)
        ranks.append(before + base)
        base = base + jnp.sum(jnp.where(onehot, 1.0, 0.0), axis=0, keepdims=True)
    cnt8 = jnp.floor((base + (SUBLANES - 1)) * (1.0 / SUBLANES))
    e_i = lax.broadcasted_iota(I32, (LANES, LANES), 0)
    e_j = lax.broadcasted_iota(I32, (LANES, LANES), 1)
    upper = jnp.where(e_i < e_j, 1.0, 0.0).astype(BF16)
    seg8 = jnp.dot(jnp.broadcast_to(cnt8, (SUBLANES, LANES)).astype(BF16), upper,
                   preferred_element_type=F32)[0:1, :]
    seg_row = seg8 * float(SUBLANES)
    slot_out = jnp.full((tm, LANES), -1.0, F32)
    gate_out = jnp.zeros((tm, LANES), F32)
    for k in range(TOP_K):
        slot_k = jnp.sum(jnp.where(onehots[k], ranks[k] + seg_row, 0.0), axis=-1, keepdims=True)
        slot_out = jnp.where(lane == k, slot_k, slot_out)
        gate_out = jnp.where(lane == k, exps[k] / denom, gate_out)
    slot_ref[...] = slot_out.astype(I32)
    gate_ref[...] = gate_out
    cnt_ref[...] = jnp.broadcast_to(base, (SUBLANES, LANES))

    slot_t = slot_out.T
    p_i = lax.broadcasted_iota(I32, (slots, tm), 0).astype(F32)
    perm = jnp.where(p_i == slot_t[0:1, :], 1.0, 0.0)
    for k in range(1, TOP_K):
        perm = perm + jnp.where(p_i == slot_t[k:k + 1, :], 1.0, 0.0)
    perm = perm.astype(BF16)
    h2b = h2.astype(BF16)
    cw = 2 * PACK_GROUP
    for c in range(D_MODEL // cw):
        y = jnp.dot(perm, h2b[:, c * cw:(c + 1) * cw], preferred_element_type=F32)
        lo, hi = _split_cols(y)
        xs_ref[:, c * (cw // 2):(c + 1) * (cw // 2)] = _pack_rows(lo, hi)


def _merge(p_act, o_a, o_b, x2d, w_out, g_ffn, w_router_hi, w_router_lo, b_router):
    n = x2d.shape[0]
    tm = MERGE_TM
    nt = n // tm
    row = lambda c: pl.BlockSpec((tm, D_MODEL), lambda i: (i, c))
    const = lambda shape: pl.BlockSpec(shape, lambda i: (0, 0))
    aux = pl.BlockSpec((tm, LANES), lambda i: (i, 0))
    return pl.pallas_call(
        _merge_kernel,
        out_shape=(
            jax.ShapeDtypeStruct((n, D_MODEL), F32),
            jax.ShapeDtypeStruct((nt * PAIR * MERGE_SLOTS, D_HALF), BF16),
            jax.ShapeDtypeStruct((n, LANES), I32),
            jax.ShapeDtypeStruct((n, LANES), F32),
            jax.ShapeDtypeStruct((nt * SUBLANES, LANES), F32),
        ),
        grid=(nt,),
        in_specs=[
            row(COL_GA // D_MODEL), row(COL_GB // D_MODEL), row(COL_RB // D_MODEL),
            row(0), row(0), row(0),
            pl.BlockSpec((D_MODEL, D_MODEL), lambda i: (0, 0), pipeline_mode=pl.Buffered(1)),
            const((1, D_MODEL)),
            pl.BlockSpec((D_MODEL, LANES), lambda i: (0, 0), pipeline_mode=pl.Buffered(1)),
            pl.BlockSpec((D_MODEL, LANES), lambda i: (0, 0), pipeline_mode=pl.Buffered(1)),
            const((1, LANES)),
        ],
        out_specs=(row(0), pl.BlockSpec((PAIR * MERGE_SLOTS, D_HALF), lambda i: (i, 0)), aux, aux,
                   pl.BlockSpec((SUBLANES, LANES), lambda i: (i, 0))),
        compiler_params=_cparams(("arbitrary",)),
        name="merge",
    )(p_act, p_act, p_act, o_a, o_b, x2d, w_out, g_ffn, w_router_hi, w_router_lo, b_router)


def _experts_kernel(nitems_ref, ie_ref, it0_ref, it1_ref, ibase_ref, irows_ref, c8_ref, src8_ref, cum8_ref,
                    xs_ref, wg_ref, wl_ref, bg_ref, bl_ref, wd_ref, bd_ref,
                    ys_ref, bufs, act_scr, sem_in, sem_out):
    w = pl.program_id(0)
    s = pl.program_id(1)
    nfa = D_FF // EXPERT_TF
    nfb = D_HALF // EXPERT_TN
    n_items = nitems_ref[0]
    valid = w < n_items
    cur = lax.rem(w, 2)
    unit = PAIR * SUBLANES

    def for_each_copy(item, slot, outbound, fn):
        e = ie_ref[item]
        base = ibase_ref[item]
        buf = bufs.at[slot]

        def tile_body(t, carry):
            seg = t * N_EXPERTS + e
            c = c8_ref[seg]
            hbm0 = src8_ref[seg] * unit
            buf0 = (cum8_ref[seg] - base) * unit

            def run(count, rows, off0):
                def one(u, carry):
                    off = off0 + u * rows
                    in_buf = buf.at[pl.ds(pl.multiple_of(buf0 + off, unit), rows), :]
                    if outbound:
                        fn(pltpu.make_async_copy(
                            in_buf, ys_ref.at[pl.ds(pl.multiple_of(hbm0 + off, unit), rows), :], sem_out))
                    else:
                        fn(pltpu.make_async_copy(
                            xs_ref.at[pl.ds(pl.multiple_of(hbm0 + off, unit), rows), :], in_buf, sem_in))
                    return carry

                lax.fori_loop(0, count, one, 0)

            quads = lax.div(c, SEG_QUAD)
            run(quads, SEG_QUAD * unit, jnp.zeros((), I32))
            run(c - quads * SEG_QUAD, unit, quads * (SEG_QUAD * unit))
            return carry

        lax.fori_loop(it0_ref[item], it1_ref[item], tile_body, 0)

    start = lambda cp: cp.start()
    wait = lambda cp: cp.wait()

    @pl.when((w == 0) & (s == 0))
    def _():
        bufs[...] = jnp.zeros_like(bufs)
        act_scr[...] = jnp.zeros_like(act_scr)
        for_each_copy(0, 0, False, start)

    @pl.when(valid & (s == 0))
    def _():
        for_each_copy(w, cur, False, wait)

    n_sub = lax.div(irows_ref[w] * SUBLANES + (EXPERT_SUB - 1), EXPERT_SUB)

    @pl.when(valid & (s < nfa))
    def _():
        def packed_order(w_ref):
            half = PACK_GROUP // 2
            groups = range(D_MODEL // PACK_GROUP)
            return jnp.concatenate(
                [w_ref[g * PACK_GROUP:g * PACK_GROUP + half, :].astype(BF16) for g in groups]
                + [w_ref[g * PACK_GROUP + half:(g + 1) * PACK_GROUP, :].astype(BF16) for g in groups], axis=0)

        wg = packed_order(wg_ref)
        wl = packed_order(wl_ref)
        bg = bg_ref[...]
        bl = bl_ref[...]
        tile = jnp.minimum(s, nfa - 1)

        def sub_body(r, carry):
            z = bufs[cur, pl.ds(pl.multiple_of(r * (PAIR * EXPERT_SUB), PAIR * EXPERT_SUB),
                                PAIR * EXPERT_SUB), :]
            lo, hi = _unpack_rows(z)
            lo = lo.astype(BF16)
            hi = hi.astype(BF16)
            glu = (jnp.dot(lo, wg[0:D_HALF], preferred_element_type=F32)
                   + jnp.dot(hi, wg[D_HALF:D_MODEL], preferred_element_type=F32) + bg)
            lin = (jnp.dot(lo, wl[0:D_HALF], preferred_element_type=F32)
                   + jnp.dot(hi, wl[D_HALF:D_MODEL], preferred_element_type=F32) + bl)
            glu = jnp.minimum(glu, SWIGLU_LIMIT)
            lin = jnp.clip(lin, -SWIGLU_LIMIT, SWIGLU_LIMIT)
            act = glu * _sigmoid(SWIGLU_ALPHA * glu) * (lin + 1.0)
            act_scr[tile, pl.ds(pl.multiple_of(r * EXPERT_SUB, EXPERT_SUB), EXPERT_SUB), :] = act.astype(BF16)
            return carry

        lax.fori_loop(0, n_sub, sub_body, 0)

    @pl.when(valid & (s == nfa))
    def _():
        @pl.when(w >= 1)
        def _():
            for_each_copy(w - 1, 1 - cur, True, wait)

        @pl.when(w + 1 < n_items)
        def _():
            for_each_copy(w + 1, 1 - cur, False, start)

    @pl.when(valid & (s >= nfa))
    def _():
        wd = wd_ref[...].astype(BF16)
        bd = bd_ref[...]
        col = pl.multiple_of(jnp.maximum(s - nfa, 0) * EXPERT_TN, EXPERT_TN)

        def sub_body(r, carry):
            rows = pl.ds(pl.multiple_of(r * EXPERT_SUB, EXPERT_SUB), EXPERT_SUB)
            acc = jnp.dot(act_scr[0, rows, :], wd[0:EXPERT_TF], preferred_element_type=F32)
            for t in range(1, nfa):
                acc += jnp.dot(act_scr[t, rows, :], wd[t * EXPERT_TF:(t + 1) * EXPERT_TF],
                               preferred_element_type=F32)
            acc = acc + bd
            bufs[cur, pl.ds(pl.multiple_of(r * (PAIR * EXPERT_SUB), PAIR * EXPERT_SUB), PAIR * EXPERT_SUB),
                 pl.ds(col, EXPERT_TN)] = _pack_rows(*_split_cols(acc))
            return carry

        lax.fori_loop(0, n_sub, sub_body, 0)

    @pl.when(valid & (s == nfa + nfb - 1))
    def _():
        for_each_copy(w, cur, True, start)

        @pl.when(w == n_items - 1)
        def _():
            for_each_copy(w, cur, True, wait)


def _experts(x_sorted, meta, n_items_max, w_gu, b_gu, w_down, b_down):
    nfa = D_FF // EXPERT_TF
    nfb = D_HALF // EXPERT_TN
    n_pre = len(meta)

    def fa(w, s, n_items):
        return jnp.where(w < n_items[0], jnp.minimum(s, nfa - 1), nfa - 1)

    def fb(w, s, n_items):
        return jnp.where(w < n_items[0], jnp.maximum(s - nfa, 0), nfb - 1)

    def wspec(shape, col):
        return pl.BlockSpec(shape, lambda w, s, *m: (m[1][w], 0, col(w, s, m[0])))

    any_spec = pl.BlockSpec(memory_space=pl.ANY)
    grid_spec = pltpu.PrefetchScalarGridSpec(
        num_scalar_prefetch=n_pre,
        grid=(n_items_max, nfa + nfb),
        in_specs=[
            any_spec,
            wspec((None, D_MODEL, EXPERT_TF), fa),
            wspec((None, D_MODEL, EXPERT_TF), lambda w, s, n: nfa + fa(w, s, n)),
            wspec((None, 1, EXPERT_TF), fa),
            wspec((None, 1, EXPERT_TF), lambda w, s, n: nfa + fa(w, s, n)),
            wspec((None, D_FF, 2 * EXPERT_TN), fb),
            wspec((None, 1, 2 * EXPERT_TN), fb),
        ],
        out_specs=any_spec,
        scratch_shapes=[pltpu.VMEM((2, PAIR * EXPERT_ROWS, D_HALF), BF16),
                        pltpu.VMEM((nfa, EXPERT_ROWS, EXPERT_TF), BF16),
                        pltpu.SemaphoreType.DMA, pltpu.SemaphoreType.DMA],
    )
    return pl.pallas_call(
        _experts_kernel,
        out_shape=jax.ShapeDtypeStruct(x_sorted.shape, x_sorted.dtype),
        grid_spec=grid_spec,
        input_output_aliases={n_pre: 0},
        compiler_params=_cparams(("arbitrary", "arbitrary")),
        name="experts",
    )(*meta, x_sorted, w_gu, w_gu, b_gu, b_gu, w_down, b_down)


def _combine_kernel(ys_ref, slot_ref, gate_ref, x2_ref, gfin_ref, o_ref):
    tm = x2_ref.shape[0]
    slots = ys_ref.shape[0] // PAIR
    slot = slot_ref[...]
    gates = gate_ref[...]
    p_j = lax.broadcasted_iota(I32, (tm, slots), 1)
    gmat = jnp.where(p_j == slot[:, 0:1], gates[:, 0:1], 0.0)
    for k in range(1, TOP_K):
        gmat = gmat + jnp.where(p_j == slot[:, k:k + 1], gates[:, k:k + 1], 0.0)
    gmat = gmat.astype(BF16)
    y_lo, y_hi = _unpack_rows(ys_ref[...])
    moe_lo = jnp.dot(gmat, y_lo.astype(BF16), preferred_element_type=F32)
    moe_hi = jnp.dot(gmat, y_hi.astype(BF16), preferred_element_type=F32)
    x = x2_ref[...] + _merge_cols(moe_lo, moe_hi)
    ms = jnp.mean(x * x, axis=-1, keepdims=True)
    o_ref[...] = x * lax.rsqrt(ms + NORM_EPS) * gfin_ref[...]


def _combine(y_sorted, slots, gates, x2, g_final):
    n = x2.shape[0]
    tm = MERGE_TM
    aux = pl.BlockSpec((tm, LANES), lambda i: (i, 0))
    return pl.pallas_call(
        _combine_kernel,
        out_shape=jax.ShapeDtypeStruct((n, D_MODEL), F32),
        grid=(n // tm,),
        in_specs=[
            pl.BlockSpec((PAIR * MERGE_SLOTS, D_HALF), lambda i: (i, 0)),
            aux, aux,
            pl.BlockSpec((tm, D_MODEL), lambda i: (i, 0)),
            pl.BlockSpec((1, D_MODEL), lambda i: (0, 0)),
        ],
        out_specs=pl.BlockSpec((tm, D_MODEL), lambda i: (i, 0)),
        compiler_params=_cparams(("arbitrary",)),
        name="combine",
    )(y_sorted, slots, gates, x2, g_final)


def _layer(x2d, batch, seq, attn_norm_g, w_in, swa_sinks, w_alpha2, b_alpha, gla_norm_g, w_out,
           ffn_norm_g, w_router, b_router, w_gate_up, b_gate_up, w_down, b_down, final_norm_g):
    n = x2d.shape[0]
    assert n % MERGE_TM == 0 and seq % SWA_WINDOW == 0 and seq % GLA_CHUNK == 0
    o_qa, o_ka, o_va = 0, 2048, 2304
    o_qb, o_kb, o_vb, o_rb, o_al, o_ga, o_gb = 2560, 3584, 4608, 6656, 8704, 8720, 10768
    seg = lambda s, w: w_in[:, s:s + w]
    w_main = jnp.concatenate([
        seg(o_qa, 2048), seg(o_ga, 2048), seg(o_gb, 2048), seg(o_rb, 2048), seg(o_vb, 2048),
        seg(o_qb, 1024), seg(o_kb, 1024), seg(o_ka, 256), seg(o_va, 256)], axis=1).astype(BF16)
    w_a = jnp.pad(seg(o_al, GLA_GATE_RANK), ((0, 0), (0, LANES - GLA_GATE_RANK))).astype(BF16)
    w2_hi, w2_lo = _split_bf16(jnp.pad(w_alpha2, ((0, LANES - GLA_GATE_RANK), (0, 0))))
    wr_hi, wr_lo = _split_bf16(jnp.pad(w_router, ((0, 0), (0, LANES - N_EXPERTS))))
    b_r = jnp.pad(b_router, (0, LANES - N_EXPERTS)).reshape(1, LANES)
    slopes = jnp.exp2(-8.0 * jnp.arange(1, SWA_HEADS + 1, dtype=F32) / SWA_HEADS)

    p_act, a_lr = _inproj(x2d, attn_norm_g.reshape(1, D_MODEL), w_main, w_a)
    o_a = _swa(p_act, slopes, swa_sinks.astype(F32), batch, seq)
    o_b = _gla(p_act, a_lr, w2_hi, w2_lo, b_alpha.reshape(1, -1), gla_norm_g.reshape(1, -1), batch, seq)
    x2, x_sorted, slots, gates, cnt = _merge(
        p_act, o_a, o_b, x2d, w_out.astype(BF16), ffn_norm_g.reshape(1, D_MODEL), wr_hi, wr_lo, b_r)

    nt = n // MERGE_TM
    chunk8 = EXPERT_CHUNK // SUBLANES
    cnt_t = cnt[::SUBLANES, :N_EXPERTS].astype(I32)
    c8 = (cnt_t + SUBLANES - 1) // SUBLANES
    tile8 = (jnp.arange(nt, dtype=I32) * (MERGE_SLOTS // SUBLANES))[:, None]
    src8 = tile8 + jnp.cumsum(c8, axis=1) - c8
    cum8 = jnp.cumsum(c8, axis=0) - c8
    tot8 = jnp.sum(c8, axis=0)
    per_e = (tot8 + chunk8 - 1) // chunk8
    item_end = jnp.cumsum(per_e)
    n_items = item_end[-1]
    n_items_max = -(-(n * TOP_K // SUBLANES + nt * N_EXPERTS) // chunk8) + N_EXPERTS
    w_idx = jnp.arange(n_items_max, dtype=I32)
    w_eff = jnp.minimum(w_idx, n_items - 1)
    item_e = jnp.minimum(jnp.sum(item_end[None, :] <= w_eff[:, None], axis=1), N_EXPERTS - 1).astype(I32)
    item_j = w_eff - (item_end - per_e)[item_e]
    cum_e = cum8[:, item_e]
    item_t0 = jnp.sum(cum_e < (item_j * chunk8)[None, :], axis=0).astype(I32)
    item_t1 = jnp.sum(cum_e < ((item_j + 1) * chunk8)[None, :], axis=0).astype(I32)
    t0c = jnp.minimum(item_t0, nt - 1)
    t1c = jnp.maximum(item_t1 - 1, 0)
    item_base = cum8[t0c, item_e]
    item_rows = cum8[t1c, item_e] + c8[t1c, item_e] - item_base
    used = w_idx < n_items
    item_t1 = jnp.where(used, item_t1, item_t0)
    item_rows = jnp.where(used & (item_t1 > item_t0), item_rows, 0).astype(I32)
    meta = (n_items.astype(I32).reshape(1), item_e, item_t0, item_t1, item_base.astype(I32), item_rows,
            c8.reshape(-1), src8.reshape(-1), cum8.reshape(-1))

    y_sorted = _experts(x_sorted, meta, n_items_max, w_gate_up,
                        b_gate_up.reshape(N_EXPERTS, 1, 2 * D_FF), w_down,
                        b_down.reshape(N_EXPERTS, 1, D_MODEL))
    return _combine(y_sorted, slots, gates, x2, final_norm_g.reshape(1, D_MODEL))


def kernel(x, attn_norm_g, w_in, swa_sinks, w_alpha2, b_alpha, gla_norm_g, w_out, ffn_norm_g,
           w_router, b_router, w_gate_up, b_gate_up, w_down, b_down, final_norm_g):
    batch, seq, _ = x.shape
    assert w_in.shape[0] == 1, "single-layer block: the final norm is fused into the MoE combine"
    out = _layer(
        x.reshape(batch * seq, D_MODEL), batch, seq, attn_norm_g[0], w_in[0], swa_sinks[0],
        w_alpha2[0], b_alpha[0], gla_norm_g[0], w_out[0], ffn_norm_g[0], w_router[0], b_router[0],
        w_gate_up[0], b_gate_up[0], w_down[0], b_down[0], final_norm_g)
    return out.reshape(batch, seq, D_MODEL)
```

```python
import functools

import jax
import jax.numpy as jnp
from jax import lax
from jax.experimental import pallas as pl
from jax.experimental.pallas import tpu as pltpu

F32 = jnp.float32
BF16 = jnp.bfloat16
I32 = jnp.int32
U32 = jnp.uint32

D_MODEL = 2048
D_HALF = D_MODEL // 2
NORM_EPS = 1e-5
SWA_HEADS = 32
SWA_KV_HEADS = 4
SWA_HEAD_DIM = 64
SWA_WINDOW = 128
SWA_GROUP = SWA_HEADS // SWA_KV_HEADS
GLA_HEADS = 4
GLA_DK = 256
GLA_DV = 512
GLA_GATE_RANK = 16
GLA_TAU = 16.0
GLA_CHUNK = 64
GLA_GROUP = 256
N_EXPERTS = 32
TOP_K = 4
D_FF = 2048
SWIGLU_LIMIT = 7.0
SWIGLU_ALPHA = 1.702

LANES = 128
SUBLANES = 8
MASK_NEG = -1e30

COL_QA = 0
COL_GA = 2048
COL_GB = 4096
COL_RB = 6144
COL_VB = 8192
COL_QB = 10240
COL_KB = 11264
COL_KA = 12288
COL_VA = 12544
P_COLS = 12800

VMEM_LIMIT = 56 * 1024 * 1024

INPROJ_TM = 1024
INPROJ_TN = 1280
MERGE_TM = 256
MERGE_SLOTS = MERGE_TM * TOP_K + N_EXPERTS * SUBLANES
SEG_BITS = (MERGE_TM // SUBLANES).bit_length()
PAIR = 2
PACK_GROUP = 256
EXPERT_SUB = 592
EXPERT_CHUNK = 4 * EXPERT_SUB
EXPERT_ROWS = EXPERT_CHUNK
EXPERT_TF = 256
EXPERT_TN = 256


def _cparams(sem):
    return pltpu.CompilerParams(dimension_semantics=sem, vmem_limit_bytes=VMEM_LIMIT)


def _sigmoid(x):
    return 1.0 / (1.0 + jnp.exp(-x))


def _split_bf16(x):
    hi = x.astype(BF16)
    return hi, (x - hi.astype(F32)).astype(BF16)


def _dot_split(a, b_hi, b_lo):
    a_hi, a_lo = _split_bf16(a)
    return ((jnp.dot(a_hi, b_hi, preferred_element_type=F32)
             + jnp.dot(a_lo, b_hi, preferred_element_type=F32))
            + jnp.dot(a_hi, b_lo, preferred_element_type=F32))


def _split_cols(x):
    half = PACK_GROUP // 2
    groups = range(x.shape[1] // PACK_GROUP)
    return (jnp.concatenate([x[:, g * PACK_GROUP:g * PACK_GROUP + half] for g in groups], axis=1),
            jnp.concatenate([x[:, g * PACK_GROUP + half:(g + 1) * PACK_GROUP] for g in groups], axis=1))


def _merge_cols(lo, hi):
    half = PACK_GROUP // 2
    parts = []
    for g in range(lo.shape[1] // half):
        parts += [lo[:, g * half:(g + 1) * half], hi[:, g * half:(g + 1) * half]]
    return jnp.concatenate(parts, axis=1)


def _pack_rows(lo, hi):
    return pltpu.bitcast(pltpu.pack_elementwise([lo, hi], packed_dtype=BF16), BF16)


def _unpack_rows(z):
    words = pltpu.bitcast(z, U32)
    return (pltpu.unpack_elementwise(words, index=0, packed_dtype=BF16, unpacked_dtype=F32),
            pltpu.unpack_elementwise(words, index=1, packed_dtype=BF16, unpacked_dtype=F32))


def _inproj_kernel(x_ref, g_ref, w_ref, wa_ref, p_ref, a_ref, h_scr):
    @pl.when(pl.program_id(1) == 0)
    def _():
        x = x_ref[...]
        ms = jnp.mean(x * x, axis=-1, keepdims=True)
        h = (x * lax.rsqrt(ms + NORM_EPS) * g_ref[...]).astype(BF16)
        h_scr[...] = h
        a_ref[...] = jnp.dot(h, wa_ref[...], preferred_element_type=F32)

    p_ref[...] = jnp.dot(h_scr[...], w_ref[...], preferred_element_type=F32).astype(BF16)


def _inproj(x2d, g, w_main, w_a):
    n = x2d.shape[0]
    tm = min(INPROJ_TM, n)
    return pl.pallas_call(
        _inproj_kernel,
        out_shape=(jax.ShapeDtypeStruct((n, P_COLS), BF16),
                   jax.ShapeDtypeStruct((n, LANES), F32)),
        grid=(n // tm, P_COLS // INPROJ_TN),
        in_specs=[
            pl.BlockSpec((tm, D_MODEL), lambda i, j: (i, 0)),
            pl.BlockSpec((1, D_MODEL), lambda i, j: (0, 0)),
            pl.BlockSpec((D_MODEL, INPROJ_TN), lambda i, j: (0, j)),
            pl.BlockSpec((D_MODEL, LANES), lambda i, j: (0, 0)),
        ],
        out_specs=(
            pl.BlockSpec((tm, INPROJ_TN), lambda i, j: (i, j)),
            pl.BlockSpec((tm, LANES), lambda i, j: (i, 0)),
        ),
        scratch_shapes=[pltpu.VMEM((tm, D_MODEL), BF16)],
        compiler_params=_cparams(("arbitrary", "arbitrary")),
        name="inproj",
    )(x2d, g, w_main, w_a)


def _swa_kernel(slopes_ref, sinks_ref, q_ref, kp_ref, kc_ref, vp_ref, vc_ref, o_ref):
    n = pl.program_id(1)
    w = SWA_WINDOW
    hd = SWA_HEAD_DIM
    pairs = SWA_GROUP // 2
    lane = lax.broadcasted_iota(I32, (2 * w, LANES), 1)
    low = lane < hd
    kband = jnp.concatenate([kp_ref[...], kc_ref[...]], axis=0).astype(F32)
    vband = jnp.concatenate([vp_ref[...], vc_ref[...]], axis=0).astype(F32)
    qry = lax.broadcasted_iota(I32, (w, 2 * w), 0)
    key = lax.broadcasted_iota(I32, (w, 2 * w), 1)
    dist = qry - key + w
    valid = (dist >= 0) & (dist < w) & ((key >= w) | (n > 0))
    negd = jnp.where(valid, -dist.astype(F32), MASK_NEG)
    scale = jnp.asarray(hd ** -0.5, BF16)
    ones = jnp.ones((2 * w, LANES), BF16)
    out_lane_low = lax.broadcasted_iota(I32, (w, LANES), 1) < hd
    for kv in range(SWA_KV_HEADS):
        grp = slice((kv // 2) * LANES, (kv // 2 + 1) * LANES)
        kg, vg = kband[:, grp], vband[:, grp]
        kg_sw, vg_sw = pltpu.roll(kg, hd, axis=1), pltpu.roll(vg, hd, axis=1)
        if kv % 2 == 0:
            k_even, k_odd = jnp.where(low, kg, 0.0), jnp.where(low, 0.0, kg_sw)
            v_dup = jnp.where(low, vg, vg_sw)
        else:
            k_even, k_odd = jnp.where(low, kg_sw, 0.0), jnp.where(low, 0.0, kg)
            v_dup = jnp.where(low, vg_sw, vg)
        v_ext = jnp.concatenate([v_dup.astype(BF16), ones], axis=1)
        h0 = kv * SWA_GROUP
        q_pairs = jnp.concatenate(
            [q_ref[:, (h0 + 2 * i) * hd:(h0 + 2 * i + 2) * hd] for i in range(pairs)], axis=0) * scale
        nt = (((1,), (1,)), ((), ()))
        s_par = (lax.dot_general(q_pairs, k_even.astype(BF16), nt, preferred_element_type=F32),
                 lax.dot_general(q_pairs, k_odd.astype(BF16), nt, preferred_element_type=F32))
        probs, sink_terms = [], []
        for par in range(2):
            for i in range(pairs):
                h = h0 + 2 * i + par
                s = s_par[par][i * w:(i + 1) * w, :] + slopes_ref[h] * negd
                sink = sinks_ref[h]
                m = jnp.maximum(jnp.max(s, axis=-1, keepdims=True), sink)
                probs.append(jnp.exp(s - m).astype(BF16))
                sink_terms.append(jnp.exp(sink - m))
        o_ext = jnp.dot(jnp.concatenate(probs, axis=0), v_ext, preferred_element_type=F32)
        for i in range(pairs):
            halves = []
            for par in range(2):
                r0 = (par * pairs + i) * w
                blk = o_ext[r0:r0 + w, :]
                halves.append(blk[:, 0:LANES] / (blk[:, LANES:2 * LANES] + sink_terms[par * pairs + i]))
            c0 = (h0 + 2 * i) * hd
            o_ref[:, c0:c0 + LANES] = jnp.where(out_lane_low, halves[0], halves[1]).astype(BF16)


def _swa(p_act, slopes, sinks, batch, seq):
    n = batch * seq
    w = SWA_WINDOW
    nb = seq // w
    kvw = SWA_KV_HEADS * SWA_HEAD_DIM

    def cur(col):
        return lambda b, i: (b * nb + i, col)

    def prev(col):
        return lambda b, i: (b * nb + jnp.maximum(i - 1, 0), col)

    smem = pl.BlockSpec(memory_space=pltpu.SMEM)
    return pl.pallas_call(
        _swa_kernel,
        out_shape=jax.ShapeDtypeStruct((n, D_MODEL), BF16),
        grid=(batch, nb),
        in_specs=[
            smem, smem,
            pl.BlockSpec((w, D_MODEL), cur(COL_QA // D_MODEL)),
            pl.BlockSpec((w, kvw), prev(COL_KA // kvw)),
            pl.BlockSpec((w, kvw), cur(COL_KA // kvw)),
            pl.BlockSpec((w, kvw), prev(COL_VA // kvw)),
            pl.BlockSpec((w, kvw), cur(COL_VA // kvw)),
        ],
        out_specs=pl.BlockSpec((w, D_MODEL), lambda b, i: (b * nb + i, 0)),
        compiler_params=_cparams(("arbitrary", "arbitrary")),
        name="swa",
    )(slopes, sinks, p_act, p_act, p_act, p_act, p_act)


def _gla_kernel(q_ref, k_ref, v_ref, a_ref, w2h_ref, w2l_ref, ba_ref, gn_ref, o_ref,
                st_scr, qi_scr, ks_scr, dec_scr, intra_scr):
    c = GLA_CHUNK
    grp = GLA_GROUP
    cpg = grp // c
    seq = q_ref.shape[0]
    row = lax.broadcasted_iota(I32, (grp, grp), 0)
    col = lax.broadcasted_iota(I32, (grp, grp), 1)
    tri = jnp.where((row >= col) & (row // c == col // c), 1.0, 0.0).astype(BF16)
    causal = lax.broadcasted_iota(I32, (c, c), 0) >= lax.broadcasted_iota(I32, (c, c), 1)
    scale = jnp.asarray(GLA_DK ** -0.5, F32)

    def group(gi, carry):
        g0 = pl.multiple_of(gi * grp, grp)
        z = _dot_split(a_ref[pl.ds(g0, grp), :], w2h_ref[...], w2l_ref[...]) + ba_ref[...]
        log_a = (jnp.minimum(z, 0.0) - jnp.log1p(jnp.exp(-jnp.abs(z)))) * (1.0 / GLA_TAU)
        la_hi, la_mid = _split_bf16(log_a)
        la_lo = ((log_a - la_hi.astype(F32)) - la_mid.astype(F32)).astype(BF16)
        b_all = ((jnp.dot(tri, la_hi, preferred_element_type=F32)
                  + jnp.dot(tri, la_mid, preferred_element_type=F32))
                 + jnp.dot(tri, la_lo, preferred_element_type=F32))
        qf_all = q_ref[pl.ds(g0, grp), :].astype(F32) * scale
        kf_all = k_ref[pl.ds(g0, grp), :].astype(F32)
        atts = []
        for j in range(cpg):
            r0 = pl.multiple_of(g0 + j * c, c)
            b = b_all[j * c:(j + 1) * c, :]
            qf = qf_all[j * c:(j + 1) * c, :]
            kf = kf_all[j * c:(j + 1) * c, :]
            b_last = b[c - 1:c, :]
            b_mid = b[c // 2 - 1:c // 2, :]
            qi_scr[pl.ds(r0, c), :] = (qf * jnp.exp(b)).astype(BF16)
            ks_scr[pl.ds(r0, c), :] = (kf * jnp.exp(b_last - b)).astype(BF16)
            dec_scr[pl.ds(gi * cpg + j, 1), :] = jnp.exp(b_last)
            q_intra = (qf * jnp.exp(b - b_mid)).astype(BF16)
            k_intra = (kf * jnp.exp(b_mid - b)).astype(BF16)
            atts.append(lax.dot_general(q_intra, k_intra, (((1,), (1,)), ((), ())),
                                        preferred_element_type=F32))
        for j in range(cpg):
            r0 = pl.multiple_of(g0 + j * c, c)
            att = jnp.where(causal, atts[j], 0.0).astype(BF16)
            intra_scr[pl.ds(r0, c), :] = jnp.dot(att, v_ref[pl.ds(r0, c), :], preferred_element_type=F32)
        return carry

    lax.fori_loop(0, seq // grp, group, 0, unroll=2)

    st_scr[...] = jnp.zeros_like(st_scr)

    def scan_group(gi, carry):
        rows = [pl.multiple_of((gi * cpg + j) * c, c) for j in range(cpg)]
        upds = [lax.dot_general(v_ref[pl.ds(r0, c), :], ks_scr[pl.ds(r0, c), :], (((0,), (0,)), ((), ())),
                                preferred_element_type=F32) for r0 in rows]
        st = st_scr[...]
        for j, r0 in enumerate(rows):
            inter = lax.dot_general(qi_scr[pl.ds(r0, c), :], st.astype(BF16), (((1,), (1,)), ((), ())),
                                    preferred_element_type=F32)
            st = st * dec_scr[pl.ds(gi * cpg + j, 1), :] + upds[j]
            o = inter + intra_scr[pl.ds(r0, c), :]
            ms = jnp.mean(o * o, axis=-1, keepdims=True)
            o_ref[pl.ds(r0, c), :] = (o * lax.rsqrt(ms + NORM_EPS) * gn_ref[...]).astype(BF16)
        st_scr[...] = st
        return carry

    lax.fori_loop(0, seq // grp, scan_group, 0)


def _gla(p_act, a_lr, w2_hi, w2_lo, b_alpha, gla_norm_g, batch, seq):
    n = batch * seq
    assert seq % GLA_GROUP == 0 and GLA_GROUP % GLA_CHUNK == 0
    return pl.pallas_call(
        _gla_kernel,
        out_shape=jax.ShapeDtypeStruct((n, GLA_HEADS * GLA_DV), BF16),
        grid=(batch, GLA_HEADS),
        in_specs=[
            pl.BlockSpec((seq, GLA_DK), lambda b, h: (b, COL_QB // GLA_DK + h)),
            pl.BlockSpec((seq, GLA_DK), lambda b, h: (b, COL_KB // GLA_DK + h)),
            pl.BlockSpec((seq, GLA_DV), lambda b, h: (b, COL_VB // GLA_DV + h)),
            pl.BlockSpec((seq, LANES), lambda b, h: (b, 0)),
            pl.BlockSpec((LANES, GLA_DK), lambda b, h: (0, h)),
            pl.BlockSpec((LANES, GLA_DK), lambda b, h: (0, h)),
            pl.BlockSpec((1, GLA_DK), lambda b, h: (0, h)),
            pl.BlockSpec((1, GLA_DV), lambda b, h: (0, 0)),
        ],
        out_specs=pl.BlockSpec((seq, GLA_DV), lambda b, h: (b, h)),
        scratch_shapes=[pltpu.VMEM((GLA_DV, GLA_DK), F32),
                        pltpu.VMEM((seq, GLA_DK), BF16),
                        pltpu.VMEM((seq, GLA_DK), BF16),
                        pltpu.VMEM((seq // GLA_CHUNK, GLA_DK), F32),
                        pltpu.VMEM((seq, GLA_DV), F32)],
        compiler_params=_cparams(("arbitrary", "arbitrary")),
        name="gla",
    )(p_act, p_act, p_act, a_lr, w2_hi, w2_lo, b_alpha, gla_norm_g)


def _merge_kernel(ga_ref, gb_ref, rb_ref, oa_ref, ob_ref, x_ref, wout_ref, gffn_ref, wrh_ref, wrl_ref, br_ref,
                  x2_ref, xs_ref, slot_ref, gate_ref, cnt_ref):
    tm = x_ref.shape[0]
    slots = xs_ref.shape[0] // PAIR
    ga = ga_ref[...].astype(F32)
    gb = gb_ref[...].astype(F32)
    rb = rb_ref[...].astype(F32)
    ob = ob_ref[...].astype(F32) * (rb * _sigmoid(rb))
    mix = _sigmoid(ga) * oa_ref[...].astype(F32) + _sigmoid(gb) * ob
    x2 = x_ref[...] + jnp.dot(mix.astype(BF16), wout_ref[...], preferred_element_type=F32)
    x2_ref[...] = x2
    ms = jnp.mean(x2 * x2, axis=-1, keepdims=True)
    h2 = x2 * lax.rsqrt(ms + NORM_EPS) * gffn_ref[...]

    logits = _dot_split(h2, wrh_ref[...], wrl_ref[...]) + br_ref[...]
    lane = lax.broadcasted_iota(I32, (tm, LANES), 1)
    neg_inf = jnp.asarray(-jnp.inf, F32)
    work = jnp.where(lane < N_EXPERTS, logits, neg_inf)
    vals, idxs = [], []
    for _ in range(TOP_K):
        m = jnp.max(work, axis=-1, keepdims=True)
        idx = jnp.min(jnp.where(work == m, lane, LANES), axis=-1, keepdims=True)
        vals.append(m)
        idxs.append(idx)
        work = jnp.where(lane == idx, neg_inf, work)
    exps = [jnp.exp(v - vals[0]) for v in vals]
    denom = exps[0] + exps[1] + exps[2] + exps[3]

    r_i = lax.broadcasted_iota(I32, (tm, tm), 0)
    c_i = lax.broadcasted_iota(I32, (tm, tm), 1)
    lower = jnp.where(r_i > c_i, 1.0, 0.0).astype(BF16)
    base = jnp.zeros((1, LANES), F32)
    onehots, ranks = [], []
    for k in range(TOP_K):
        onehot = lane == idxs[k]
        before = jnp.dot(lower, jnp.where(onehot, 1.0, 0.0).astype(BF16), preferred_element_type=F32)
        onehots.append(onehot)
        ranks.append(before + base)
        base = base + jnp.sum(jnp.where(onehot, 1.0, 0.0), axis=0, keepdims=True)
    cnt8 = jnp.floor((base + (SUBLANES - 1)) * (1.0 / SUBLANES))
    e_i = lax.broadcasted_iota(I32, (LANES, LANES), 0)
    e_j = lax.broadcasted_iota(I32, (LANES, LANES), 1)
    upper = jnp.where(e_i < e_j, 1.0, 0.0).astype(BF16)
    seg8 = jnp.dot(jnp.broadcast_to(cnt8, (SUBLANES, LANES)).astype(BF16), upper,
                   preferred_element_type=F32)[0:1, :]
    seg_row = seg8 * float(SUBLANES)
    slot_out = jnp.full((tm, LANES), -1.0, F32)
    gate_out = jnp.zeros((tm, LANES), F32)
    for k in range(TOP_K):
        slot_k = jnp.sum(jnp.where(onehots[k], ranks[k] + seg_row, 0.0), axis=-1, keepdims=True)
        slot_out = jnp.where(lane == k, slot_k, slot_out)
        gate_out = jnp.where(lane == k, exps[k] / denom, gate_out)
    slot_ref[...] = slot_out.astype(I32)
    gate_ref[...] = gate_out
    cnt_ref[...] = jnp.broadcast_to(base, (SUBLANES, LANES))

    slot_t = slot_out.T
    p_i = lax.broadcasted_iota(I32, (slots, tm), 0).astype(F32)
    perm = jnp.where(p_i == slot_t[0:1, :], 1.0, 0.0)
    for k in range(1, TOP_K):
        perm = perm + jnp.where(p_i == slot_t[k:k + 1, :], 1.0, 0.0)
    perm = perm.astype(BF16)
    h2b = h2.astype(BF16)
    cw = 2 * PACK_GROUP
    for c in range(D_MODEL // cw):
        y = jnp.dot(perm, h2b[:, c * cw:(c + 1) * cw], preferred_element_type=F32)
        lo, hi = _split_cols(y)
        xs_ref[:, c * (cw // 2):(c + 1) * (cw // 2)] = _pack_rows(lo, hi)


def _merge(p_act, o_a, o_b, x2d, w_out, g_ffn, w_router_hi, w_router_lo, b_router):
    n = x2d.shape[0]
    tm = MERGE_TM
    nt = n // tm
    row = lambda c: pl.BlockSpec((tm, D_MODEL), lambda i: (i, c))
    const = lambda shape: pl.BlockSpec(shape, lambda i: (0, 0))
    aux = pl.BlockSpec((tm, LANES), lambda i: (i, 0))
    return pl.pallas_call(
        _merge_kernel,
        out_shape=(
            jax.ShapeDtypeStruct((n, D_MODEL), F32),
            jax.ShapeDtypeStruct((nt * PAIR * MERGE_SLOTS, D_HALF), BF16),
            jax.ShapeDtypeStruct((n, LANES), I32),
            jax.ShapeDtypeStruct((n, LANES), F32),
            jax.ShapeDtypeStruct((nt * SUBLANES, LANES), F32),
        ),
        grid=(nt,),
        in_specs=[
            row(COL_GA // D_MODEL), row(COL_GB // D_MODEL), row(COL_RB // D_MODEL),
            row(0), row(0), row(0),
            pl.BlockSpec((D_MODEL, D_MODEL), lambda i: (0, 0), pipeline_mode=pl.Buffered(1)),
            const((1, D_MODEL)),
            pl.BlockSpec((D_MODEL, LANES), lambda i: (0, 0), pipeline_mode=pl.Buffered(1)),
            pl.BlockSpec((D_MODEL, LANES), lambda i: (0, 0), pipeline_mode=pl.Buffered(1)),
            const((1, LANES)),
        ],
        out_specs=(row(0), pl.BlockSpec((PAIR * MERGE_SLOTS, D_HALF), lambda i: (i, 0)), aux, aux,
                   pl.BlockSpec((SUBLANES, LANES), lambda i: (i, 0))),
        compiler_params=_cparams(("arbitrary",)),
        name="merge",
    )(p_act, p_act, p_act, o_a, o_b, x2d, w_out, g_ffn, w_router_hi, w_router_lo, b_router)


def _experts_kernel(nitems_ref, ie_ref, it0_ref, it1_ref, ilo_ref, ihi_ref, c8_ref, src8_ref, cum8_ref,
                    xs_ref, wg_ref, wl_ref, bg_ref, bl_ref, wd_ref, bd_ref,
                    ys_ref, bufs, act_scr, sem_in, sem_out):
    w = pl.program_id(0)
    s = pl.program_id(1)
    nfa = D_FF // EXPERT_TF
    nfb = D_HALF // EXPERT_TN
    n_items = nitems_ref[0]
    valid = w < n_items
    cur = lax.rem(w, 2)
    unit = PAIR * SUBLANES

    def for_each_copy(item, slot, outbound, fn):
        e = ie_ref[item]
        lo = ilo_ref[item]
        hi = ihi_ref[item]
        buf = bufs.at[slot]

        def tile_body(t, carry):
            seg = t * N_EXPERTS + e
            seg0 = cum8_ref[seg]
            first = jnp.maximum(seg0, lo)
            c = jnp.maximum(jnp.minimum(seg0 + c8_ref[seg], hi) - first, 0)
            hbm0 = (src8_ref[seg] + (first - seg0)) * unit
            buf0 = (first - lo) * unit

            def copies(bits, off):
                for bit in bits:
                    rows = unit << bit
                    flag = lax.bitwise_and(lax.shift_right_logical(c, bit), 1)

                    @pl.when(flag == 1)
                    def _(rows=rows, off=off):
                        in_buf = buf.at[pl.ds(pl.multiple_of(buf0 + off, unit), rows), :]
                        if outbound:
                            fn(pltpu.make_async_copy(
                                in_buf, ys_ref.at[pl.ds(pl.multiple_of(hbm0 + off, unit), rows), :], sem_out))
                        else:
                            fn(pltpu.make_async_copy(
                                xs_ref.at[pl.ds(pl.multiple_of(hbm0 + off, unit), rows), :], in_buf, sem_in))

                    off = off + flag * rows

            low_bits = 3
            high = lax.shift_right_logical(c, low_bits)

            @pl.when(high > 0)
            def _():
                copies(range(SEG_BITS - 1, low_bits - 1, -1), jnp.zeros((), I32))

            copies(range(low_bits - 1, -1, -1), lax.shift_left(high, low_bits) * unit)
            return carry

        lax.fori_loop(it0_ref[item], it1_ref[item], tile_body, 0)

    start = lambda cp: cp.start()
    wait = lambda cp: cp.wait()

    @pl.when((w == 0) & (s == 0))
    def _():
        bufs[...] = jnp.zeros_like(bufs)
        act_scr[...] = jnp.zeros_like(act_scr)
        for_each_copy(0, 0, False, start)

    @pl.when(valid & (s == 0))
    def _():
        for_each_copy(w, cur, False, wait)

    n_sub = lax.div((ihi_ref[w] - ilo_ref[w]) * SUBLANES + (EXPERT_SUB - 1), EXPERT_SUB)

    @pl.when(valid & (s < nfa))
    def _():
        def packed_order(w_ref):
            half = PACK_GROUP // 2
            groups = range(D_MODEL // PACK_GROUP)
            return jnp.concatenate(
                [w_ref[g * PACK_GROUP:g * PACK_GROUP + half, :].astype(BF16) for g in groups]
                + [w_ref[g * PACK_GROUP + half:(g + 1) * PACK_GROUP, :].astype(BF16) for g in groups], axis=0)

        wg = packed_order(wg_ref)
        wl = packed_order(wl_ref)
        bg = bg_ref[...]
        bl = bl_ref[...]
        tile = jnp.minimum(s, nfa - 1)

        def sub_body(r, carry):
            z = bufs[cur, pl.ds(pl.multiple_of(r * (PAIR * EXPERT_SUB), PAIR * EXPERT_SUB),
                                PAIR * EXPERT_SUB), :]
            lo, hi = _unpack_rows(z)
            lo = lo.astype(BF16)
            hi = hi.astype(BF16)
            glu = (jnp.dot(lo, wg[0:D_HALF], preferred_element_type=F32)
                   + jnp.dot(hi, wg[D_HALF:D_MODEL], preferred_element_type=F32) + bg)
            lin = (jnp.dot(lo, wl[0:D_HALF], preferred_element_type=F32)
                   + jnp.dot(hi, wl[D_HALF:D_MODEL], preferred_element_type=F32) + bl)
            glu = jnp.minimum(glu, SWIGLU_LIMIT)
            lin = jnp.clip(lin, -SWIGLU_LIMIT, SWIGLU_LIMIT)
            act = glu * _sigmoid(SWIGLU_ALPHA * glu) * (lin + 1.0)
            act_scr[tile, pl.ds(pl.multiple_of(r * EXPERT_SUB, EXPERT_SUB), EXPERT_SUB), :] = act.astype(BF16)
            return carry

        lax.fori_loop(0, n_sub, sub_body, 0)

    @pl.when(valid & (s == nfa))
    def _():
        @pl.when(w >= 1)
        def _():
            for_each_copy(w - 1, 1 - cur, True, wait)

        @pl.when(w + 1 < n_items)
        def _():
            for_each_copy(w + 1, 1 - cur, False, start)

    @pl.when(valid & (s >= nfa))
    def _():
        wd = wd_ref[...].astype(BF16)
        bd = bd_ref[...]
        col = pl.multiple_of(jnp.maximum(s - nfa, 0) * EXPERT_TN, EXPERT_TN)

        def sub_body(r, carry):
            rows = pl.ds(pl.multiple_of(r * EXPERT_SUB, EXPERT_SUB), EXPERT_SUB)
            acc = jnp.dot(act_scr[0, rows, :], wd[0:EXPERT_TF], preferred_element_type=F32)
            for t in range(1, nfa):
                acc += jnp.dot(act_scr[t, rows, :], wd[t * EXPERT_TF:(t + 1) * EXPERT_TF],
                               preferred_element_type=F32)
            acc = acc + bd
            bufs[cur, pl.ds(pl.multiple_of(r * (PAIR * EXPERT_SUB), PAIR * EXPERT_SUB), PAIR * EXPERT_SUB),
                 pl.ds(col, EXPERT_TN)] = _pack_rows(*_split_cols(acc))
            return carry

        lax.fori_loop(0, n_sub, sub_body, 0)

    @pl.when(valid & (s == nfa + nfb - 1))
    def _():
        for_each_copy(w, cur, True, start)

        @pl.when(w == n_items - 1)
        def _():
            for_each_copy(w, cur, True, wait)


def _experts(x_sorted, meta, n_items_max, w_gu, b_gu, w_down, b_down):
    nfa = D_FF // EXPERT_TF
    nfb = D_HALF // EXPERT_TN
    n_pre = len(meta)

    def fa(w, s, n_items):
        return jnp.where(w < n_items[0], jnp.minimum(s, nfa - 1), nfa - 1)

    def fb(w, s, n_items):
        return jnp.where(w < n_items[0], jnp.maximum(s - nfa, 0), nfb - 1)

    def wspec(shape, col):
        return pl.BlockSpec(shape, lambda w, s, *m: (m[1][w], 0, col(w, s, m[0])))

    any_spec = pl.BlockSpec(memory_space=pl.ANY)
    grid_spec = pltpu.PrefetchScalarGridSpec(
        num_scalar_prefetch=n_pre,
        grid=(n_items_max, nfa + nfb),
        in_specs=[
            any_spec,
            wspec((None, D_MODEL, EXPERT_TF), fa),
            wspec((None, D_MODEL, EXPERT_TF), lambda w, s, n: nfa + fa(w, s, n)),
            wspec((None, 1, EXPERT_TF), fa),
            wspec((None, 1, EXPERT_TF), lambda w, s, n: nfa + fa(w, s, n)),
            wspec((None, D_FF, 2 * EXPERT_TN), fb),
            wspec((None, 1, 2 * EXPERT_TN), fb),
        ],
        out_specs=any_spec,
        scratch_shapes=[pltpu.VMEM((2, PAIR * EXPERT_ROWS, D_HALF), BF16),
                        pltpu.VMEM((nfa, EXPERT_ROWS, EXPERT_TF), BF16),
                        pltpu.SemaphoreType.DMA, pltpu.SemaphoreType.DMA],
    )
    return pl.pallas_call(
        _experts_kernel,
        out_shape=jax.ShapeDtypeStruct(x_sorted.shape, x_sorted.dtype),
        grid_spec=grid_spec,
        input_output_aliases={n_pre: 0},
        compiler_params=_cparams(("arbitrary", "arbitrary")),
        name="experts",
    )(*meta, x_sorted, w_gu, w_gu, b_gu, b_gu, w_down, b_down)


def _combine_kernel(ys_ref, slot_ref, gate_ref, x2_ref, gfin_ref, o_ref):
    tm = x2_ref.shape[0]
    slots = ys_ref.shape[0] // PAIR
    slot = slot_ref[...]
    gates = gate_ref[...]
    p_j = lax.broadcasted_iota(I32, (tm, slots), 1)
    gmat = jnp.where(p_j == slot[:, 0:1], gates[:, 0:1], 0.0)
    for k in range(1, TOP_K):
        gmat = gmat + jnp.where(p_j == slot[:, k:k + 1], gates[:, k:k + 1], 0.0)
    gmat = gmat.astype(BF16)
    y_lo, y_hi = _unpack_rows(ys_ref[...])
    moe_lo = jnp.dot(gmat, y_lo.astype(BF16), preferred_element_type=F32)
    moe_hi = jnp.dot(gmat, y_hi.astype(BF16), preferred_element_type=F32)
    x = x2_ref[...] + _merge_cols(moe_lo, moe_hi)
    ms = jnp.mean(x * x, axis=-1, keepdims=True)
    o_ref[...] = x * lax.rsqrt(ms + NORM_EPS) * gfin_ref[...]


def _combine(y_sorted, slots, gates, x2, g_final):
    n = x2.shape[0]
    tm = MERGE_TM
    aux = pl.BlockSpec((tm, LANES), lambda i: (i, 0))
    return pl.pallas_call(
        _combine_kernel,
        out_shape=jax.ShapeDtypeStruct((n, D_MODEL), F32),
        grid=(n // tm,),
        in_specs=[
            pl.BlockSpec((PAIR * MERGE_SLOTS, D_HALF), lambda i: (i, 0)),
            aux, aux,
            pl.BlockSpec((tm, D_MODEL), lambda i: (i, 0)),
            pl.BlockSpec((1, D_MODEL), lambda i: (0, 0)),
        ],
        out_specs=pl.BlockSpec((tm, D_MODEL), lambda i: (i, 0)),
        compiler_params=_cparams(("arbitrary",)),
        name="combine",
    )(y_sorted, slots, gates, x2, g_final)


def _layer(x2d, batch, seq, attn_norm_g, w_in, swa_sinks, w_alpha2, b_alpha, gla_norm_g, w_out,
           ffn_norm_g, w_router, b_router, w_gate_up, b_gate_up, w_down, b_down, final_norm_g):
    n = x2d.shape[0]
    assert n % MERGE_TM == 0 and seq % SWA_WINDOW == 0 and seq % GLA_CHUNK == 0
    o_qa, o_ka, o_va = 0, 2048, 2304
    o_qb, o_kb, o_vb, o_rb, o_al, o_ga, o_gb = 2560, 3584, 4608, 6656, 8704, 8720, 10768
    seg = lambda s, w: w_in[:, s:s + w]
    w_main = jnp.concatenate([
        seg(o_qa, 2048), seg(o_ga, 2048), seg(o_gb, 2048), seg(o_rb, 2048), seg(o_vb, 2048),
        seg(o_qb, 1024), seg(o_kb, 1024), seg(o_ka, 256), seg(o_va, 256)], axis=1).astype(BF16)
    w_a = jnp.pad(seg(o_al, GLA_GATE_RANK), ((0, 0), (0, LANES - GLA_GATE_RANK))).astype(BF16)
    w2_hi, w2_lo = _split_bf16(jnp.pad(w_alpha2, ((0, LANES - GLA_GATE_RANK), (0, 0))))
    wr_hi, wr_lo = _split_bf16(jnp.pad(w_router, ((0, 0), (0, LANES - N_EXPERTS))))
    b_r = jnp.pad(b_router, (0, LANES - N_EXPERTS)).reshape(1, LANES)
    slopes = jnp.exp2(-8.0 * jnp.arange(1, SWA_HEADS + 1, dtype=F32) / SWA_HEADS)

    p_act, a_lr = _inproj(x2d, attn_norm_g.reshape(1, D_MODEL), w_main, w_a)
    o_a = _swa(p_act, slopes, swa_sinks.astype(F32), batch, seq)
    o_b = _gla(p_act, a_lr, w2_hi, w2_lo, b_alpha.reshape(1, -1), gla_norm_g.reshape(1, -1), batch, seq)
    x2, x_sorted, slots, gates, cnt = _merge(
        p_act, o_a, o_b, x2d, w_out.astype(BF16), ffn_norm_g.reshape(1, D_MODEL), wr_hi, wr_lo, b_r)

    nt = n // MERGE_TM
    chunk8 = EXPERT_CHUNK // SUBLANES
    cnt_t = cnt[::SUBLANES, :N_EXPERTS].astype(I32)
    c8 = (cnt_t + SUBLANES - 1) // SUBLANES
    tile8 = (jnp.arange(nt, dtype=I32) * (MERGE_SLOTS // SUBLANES))[:, None]
    src8 = tile8 + jnp.cumsum(c8, axis=1) - c8
    cum8 = jnp.cumsum(c8, axis=0) - c8
    tot8 = jnp.sum(c8, axis=0)
    per_e = (tot8 + chunk8 - 1) // chunk8
    item_end = jnp.cumsum(per_e)
    n_items = item_end[-1]
    n_items_max = -(-(n * TOP_K // SUBLANES + nt * N_EXPERTS) // chunk8) + N_EXPERTS
    w_idx = jnp.arange(n_items_max, dtype=I32)
    w_eff = jnp.minimum(w_idx, n_items - 1)
    item_e = jnp.minimum(jnp.sum(item_end[None, :] <= w_eff[:, None], axis=1), N_EXPERTS - 1).astype(I32)
    item_j = w_eff - (item_end - per_e)[item_e]
    used = w_idx < n_items
    item_lo = item_j * chunk8
    item_hi = jnp.where(used, jnp.minimum(item_lo + chunk8, tot8[item_e]), item_lo)
    cum_e = cum8[:, item_e]
    item_t0 = jnp.sum((cum_e + c8[:, item_e]) <= item_lo[None, :], axis=0).astype(I32)
    item_t1 = jnp.where(used, jnp.sum(cum_e < item_hi[None, :], axis=0), item_t0).astype(I32)
    meta = (n_items.astype(I32).reshape(1), item_e, item_t0, item_t1, item_lo.astype(I32),
            item_hi.astype(I32), c8.reshape(-1), src8.reshape(-1), cum8.reshape(-1))

    y_sorted = _experts(x_sorted, meta, n_items_max, w_gate_up,
                        b_gate_up.reshape(N_EXPERTS, 1, 2 * D_FF), w_down,
                        b_down.reshape(N_EXPERTS, 1, D_MODEL))
    return _combine(y_sorted, slots, gates, x2, final_norm_g.reshape(1, D_MODEL))


def kernel(x, attn_norm_g, w_in, swa_sinks, w_alpha2, b_alpha, gla_norm_g, w_out, ffn_norm_g,
           w_router, b_router, w_gate_up, b_gate_up, w_down, b_down, final_norm_g):
    batch, seq, _ = x.shape
    assert w_in.shape[0] == 1, "single-layer block: the final norm is fused into the MoE combine"
    out = _layer(
        x.reshape(batch * seq, D_MODEL), batch, seq, attn_norm_g[0], w_in[0], swa_sinks[0],
        w_alpha2[0], b_alpha[0], gla_norm_g[0], w_out[0], ffn_norm_g[0], w_router[0], b_router[0],
        w_gate_up[0], b_gate_up[0], w_down[0], b_down[0], final_norm_g)
    return out.reshape(batch, seq, D_MODEL)
```

```python
import functools

import jax
import jax.numpy as jnp
from jax import lax
from jax.experimental import pallas as pl
from jax.experimental.pallas import tpu as pltpu

F32 = jnp.float32
BF16 = jnp.bfloat16
I32 = jnp.int32
U32 = jnp.uint32

D_MODEL = 2048
D_HALF = D_MODEL // 2
NORM_EPS = 1e-5
SWA_HEADS = 32
SWA_KV_HEADS = 4
SWA_HEAD_DIM = 64
SWA_WINDOW = 128
SWA_GROUP = SWA_HEADS // SWA_KV_HEADS
GLA_HEADS = 4
GLA_DK = 256
GLA_DV = 512
GLA_GATE_RANK = 16
GLA_TAU = 16.0
GLA_CHUNK = 64
GLA_GROUP = 256
N_EXPERTS = 32
TOP_K = 4
D_FF = 2048
SWIGLU_LIMIT = 7.0
SWIGLU_ALPHA = 1.702

LANES = 128
SUBLANES = 8
MASK_NEG = -1e30

COL_QA = 0
COL_GA = 2048
COL_GB = 4096
COL_RB = 6144
COL_VB = 8192
COL_QB = 10240
COL_KB = 11264
COL_KA = 12288
COL_VA = 12544
P_COLS = 12800

VMEM_LIMIT = 56 * 1024 * 1024

INPROJ_TM = 1024
INPROJ_TN = 1280
MERGE_TM = 256
MERGE_SLOTS = MERGE_TM * TOP_K + N_EXPERTS * SUBLANES
SEG_BITS = (MERGE_TM // SUBLANES).bit_length()
PAIR = 2
PACK_GROUP = 256
EXPERT_SUB = 592
EXPERT_CHUNK = 4 * EXPERT_SUB
EXPERT_ROWS = EXPERT_CHUNK
EXPERT_TF = 256
EXPERT_TN = 256


def _cparams(sem):
    return pltpu.CompilerParams(dimension_semantics=sem, vmem_limit_bytes=VMEM_LIMIT)


def _sigmoid(x):
    return 1.0 / (1.0 + jnp.exp(-x))


def _split_bf16(x):
    hi = x.astype(BF16)
    return hi, (x - hi.astype(F32)).astype(BF16)


def _dot_split(a, b_hi, b_lo):
    a_hi, a_lo = _split_bf16(a)
    return ((jnp.dot(a_hi, b_hi, preferred_element_type=F32)
             + jnp.dot(a_lo, b_hi, preferred_element_type=F32))
            + jnp.dot(a_hi, b_lo, preferred_element_type=F32))


def _split_cols(x):
    half = PACK_GROUP // 2
    groups = range(x.shape[1] // PACK_GROUP)
    return (jnp.concatenate([x[:, g * PACK_GROUP:g * PACK_GROUP + half] for g in groups], axis=1),
            jnp.concatenate([x[:, g * PACK_GROUP + half:(g + 1) * PACK_GROUP] for g in groups], axis=1))


def _merge_cols(lo, hi):
    half = PACK_GROUP // 2
    parts = []
    for g in range(lo.shape[1] // half):
        parts += [lo[:, g * half:(g + 1) * half], hi[:, g * half:(g + 1) * half]]
    return jnp.concatenate(parts, axis=1)


def _pack_rows(lo, hi):
    return pltpu.bitcast(pltpu.pack_elementwise([lo, hi], packed_dtype=BF16), BF16)


def _unpack_rows(z):
    words = pltpu.bitcast(z, U32)
    return (pltpu.unpack_elementwise(words, index=0, packed_dtype=BF16, unpacked_dtype=F32),
            pltpu.unpack_elementwise(words, index=1, packed_dtype=BF16, unpacked_dtype=F32))


def _inproj_kernel(x_ref, g_ref, w_ref, wa_ref, p_ref, a_ref, h_scr):
    @pl.when(pl.program_id(1) == 0)
    def _():
        x = x_ref[...]
        ms = jnp.mean(x * x, axis=-1, keepdims=True)
        h = (x * lax.rsqrt(ms + NORM_EPS) * g_ref[...]).astype(BF16)
        h_scr[...] = h
        a_ref[...] = jnp.dot(h, wa_ref[...], preferred_element_type=F32)

    p_ref[...] = jnp.dot(h_scr[...], w_ref[...], preferred_element_type=F32).astype(BF16)


def _inproj(x2d, g, w_main, w_a):
    n = x2d.shape[0]
    tm = min(INPROJ_TM, n)
    return pl.pallas_call(
        _inproj_kernel,
        out_shape=(jax.ShapeDtypeStruct((n, P_COLS), BF16),
                   jax.ShapeDtypeStruct((n, LANES), F32)),
        grid=(n // tm, P_COLS // INPROJ_TN),
        in_specs=[
            pl.BlockSpec((tm, D_MODEL), lambda i, j: (i, 0)),
            pl.BlockSpec((1, D_MODEL), lambda i, j: (0, 0)),
            pl.BlockSpec((D_MODEL, INPROJ_TN), lambda i, j: (0, j)),
            pl.BlockSpec((D_MODEL, LANES), lambda i, j: (0, 0)),
        ],
        out_specs=(
            pl.BlockSpec((tm, INPROJ_TN), lambda i, j: (i, j)),
            pl.BlockSpec((tm, LANES), lambda i, j: (i, 0)),
        ),
        scratch_shapes=[pltpu.VMEM((tm, D_MODEL), BF16)],
        compiler_params=_cparams(("arbitrary", "arbitrary")),
        name="inproj",
    )(x2d, g, w_main, w_a)


def _swa_kernel(slopes_ref, sinks_ref, q_ref, kp_ref, kc_ref, vp_ref, vc_ref, o_ref):
    n = pl.program_id(1)
    w = SWA_WINDOW
    hd = SWA_HEAD_DIM
    pairs = SWA_GROUP // 2
    lane = lax.broadcasted_iota(I32, (2 * w, LANES), 1)
    low = lane < hd
    kband = jnp.concatenate([kp_ref[...], kc_ref[...]], axis=0).astype(F32)
    vband = jnp.concatenate([vp_ref[...], vc_ref[...]], axis=0).astype(F32)
    qry = lax.broadcasted_iota(I32, (w, 2 * w), 0)
    key = lax.broadcasted_iota(I32, (w, 2 * w), 1)
    dist = qry - key + w
    valid = (dist >= 0) & (dist < w) & ((key >= w) | (n > 0))
    negd = jnp.where(valid, -dist.astype(F32), MASK_NEG)
    scale = jnp.asarray(hd ** -0.5, BF16)
    ones = jnp.ones((2 * w, LANES), BF16)
    out_lane_low = lax.broadcasted_iota(I32, (w, LANES), 1) < hd
    for kv in range(SWA_KV_HEADS):
        grp = slice((kv // 2) * LANES, (kv // 2 + 1) * LANES)
        kg, vg = kband[:, grp], vband[:, grp]
        kg_sw, vg_sw = pltpu.roll(kg, hd, axis=1), pltpu.roll(vg, hd, axis=1)
        if kv % 2 == 0:
            k_even, k_odd = jnp.where(low, kg, 0.0), jnp.where(low, 0.0, kg_sw)
            v_dup = jnp.where(low, vg, vg_sw)
        else:
            k_even, k_odd = jnp.where(low, kg_sw, 0.0), jnp.where(low, 0.0, kg)
            v_dup = jnp.where(low, vg_sw, vg)
        v_ext = jnp.concatenate([v_dup.astype(BF16), ones], axis=1)
        h0 = kv * SWA_GROUP
        q_pairs = jnp.concatenate(
            [q_ref[:, (h0 + 2 * i) * hd:(h0 + 2 * i + 2) * hd] for i in range(pairs)], axis=0) * scale
        nt = (((1,), (1,)), ((), ()))
        s_par = (lax.dot_general(q_pairs, k_even.astype(BF16), nt, preferred_element_type=F32),
                 lax.dot_general(q_pairs, k_odd.astype(BF16), nt, preferred_element_type=F32))
        probs, sink_terms = [], []
        for par in range(2):
            for i in range(pairs):
                h = h0 + 2 * i + par
                s = s_par[par][i * w:(i + 1) * w, :] + slopes_ref[h] * negd
                sink = sinks_ref[h]
                m = jnp.maximum(jnp.max(s, axis=-1, keepdims=True), sink)
                probs.append(jnp.exp(s - m).astype(BF16))
                sink_terms.append(jnp.exp(sink - m))
        o_ext = jnp.dot(jnp.concatenate(probs, axis=0), v_ext, preferred_element_type=F32)
        for i in range(pairs):
            halves = []
            for par in range(2):
                r0 = (par * pairs + i) * w
                blk = o_ext[r0:r0 + w, :]
                halves.append(blk[:, 0:LANES] / (blk[:, LANES:2 * LANES] + sink_terms[par * pairs + i]))
            c0 = (h0 + 2 * i) * hd
            o_ref[:, c0:c0 + LANES] = jnp.where(out_lane_low, halves[0], halves[1]).astype(BF16)


def _swa(p_act, slopes, sinks, batch, seq):
    n = batch * seq
    w = SWA_WINDOW
    nb = seq // w
    kvw = SWA_KV_HEADS * SWA_HEAD_DIM

    def cur(col):
        return lambda b, i: (b * nb + i, col)

    def prev(col):
        return lambda b, i: (b * nb + jnp.maximum(i - 1, 0), col)

    smem = pl.BlockSpec(memory_space=pltpu.SMEM)
    return pl.pallas_call(
        _swa_kernel,
        out_shape=jax.ShapeDtypeStruct((n, D_MODEL), BF16),
        grid=(batch, nb),
        in_specs=[
            smem, smem,
            pl.BlockSpec((w, D_MODEL), cur(COL_QA // D_MODEL)),
            pl.BlockSpec((w, kvw), prev(COL_KA // kvw)),
            pl.BlockSpec((w, kvw), cur(COL_KA // kvw)),
            pl.BlockSpec((w, kvw), prev(COL_VA // kvw)),
            pl.BlockSpec((w, kvw), cur(COL_VA // kvw)),
        ],
        out_specs=pl.BlockSpec((w, D_MODEL), lambda b, i: (b * nb + i, 0)),
        compiler_params=_cparams(("arbitrary", "arbitrary")),
        name="swa",
    )(slopes, sinks, p_act, p_act, p_act, p_act, p_act)


def _gla_kernel(q_ref, k_ref, v_ref, a_ref, w2h_ref, w2l_ref, ba_ref, gn_ref, o_ref,
                st_scr, qi_scr, ks_scr, dec_scr, intra_scr):
    c = GLA_CHUNK
    grp = GLA_GROUP
    cpg = grp // c
    seq = q_ref.shape[0]
    row = lax.broadcasted_iota(I32, (grp, grp), 0)
    col = lax.broadcasted_iota(I32, (grp, grp), 1)
    tri = jnp.where((row >= col) & (row // c == col // c), 1.0, 0.0).astype(BF16)
    causal = lax.broadcasted_iota(I32, (c, c), 0) >= lax.broadcasted_iota(I32, (c, c), 1)
    scale = jnp.asarray(GLA_DK ** -0.5, F32)

    def group(gi, carry):
        g0 = pl.multiple_of(gi * grp, grp)
        z = _dot_split(a_ref[pl.ds(g0, grp), :], w2h_ref[...], w2l_ref[...]) + ba_ref[...]
        log_a = (jnp.minimum(z, 0.0) - jnp.log1p(jnp.exp(-jnp.abs(z)))) * (1.0 / GLA_TAU)
        la_hi, la_mid = _split_bf16(log_a)
        la_lo = ((log_a - la_hi.astype(F32)) - la_mid.astype(F32)).astype(BF16)
        b_all = ((jnp.dot(tri, la_hi, preferred_element_type=F32)
                  + jnp.dot(tri, la_mid, preferred_element_type=F32))
                 + jnp.dot(tri, la_lo, preferred_element_type=F32))
        qf_all = q_ref[pl.ds(g0, grp), :].astype(F32) * scale
        kf_all = k_ref[pl.ds(g0, grp), :].astype(F32)
        atts = []
        for j in range(cpg):
            r0 = pl.multiple_of(g0 + j * c, c)
            b = b_all[j * c:(j + 1) * c, :]
            qf = qf_all[j * c:(j + 1) * c, :]
            kf = kf_all[j * c:(j + 1) * c, :]
            b_last = b[c - 1:c, :]
            b_mid = b[c // 2 - 1:c // 2, :]
            qi_scr[pl.ds(r0, c), :] = (qf * jnp.exp(b)).astype(BF16)
            ks_scr[pl.ds(r0, c), :] = (kf * jnp.exp(b_last - b)).astype(BF16)
            dec_scr[pl.ds(gi * cpg + j, 1), :] = jnp.exp(b_last)
            q_intra = (qf * jnp.exp(b - b_mid)).astype(BF16)
            k_intra = (kf * jnp.exp(b_mid - b)).astype(BF16)
            atts.append(lax.dot_general(q_intra, k_intra, (((1,), (1,)), ((), ())),
                                        preferred_element_type=F32))
        for j in range(cpg):
            r0 = pl.multiple_of(g0 + j * c, c)
            att = jnp.where(causal, atts[j], 0.0).astype(BF16)
            intra_scr[pl.ds(r0, c), :] = jnp.dot(att, v_ref[pl.ds(r0, c), :], preferred_element_type=F32)
        return carry

    lax.fori_loop(0, seq // grp, group, 0, unroll=2)

    st_scr[...] = jnp.zeros_like(st_scr)

    def scan_group(gi, carry):
        rows = [pl.multiple_of((gi * cpg + j) * c, c) for j in range(cpg)]
        upds = [lax.dot_general(v_ref[pl.ds(r0, c), :], ks_scr[pl.ds(r0, c), :], (((0,), (0,)), ((), ())),
                                preferred_element_type=F32) for r0 in rows]
        st = st_scr[...]
        for j, r0 in enumerate(rows):
            inter = lax.dot_general(qi_scr[pl.ds(r0, c), :], st.astype(BF16), (((1,), (1,)), ((), ())),
                                    preferred_element_type=F32)
            st = st * dec_scr[pl.ds(gi * cpg + j, 1), :] + upds[j]
            o = inter + intra_scr[pl.ds(r0, c), :]
            ms = jnp.mean(o * o, axis=-1, keepdims=True)
            o_ref[pl.ds(r0, c), :] = (o * lax.rsqrt(ms + NORM_EPS) * gn_ref[...]).astype(BF16)
        st_scr[...] = st
        return carry

    lax.fori_loop(0, seq // grp, scan_group, 0)


def _gla(p_act, a_lr, w2_hi, w2_lo, b_alpha, gla_norm_g, batch, seq):
    n = batch * seq
    assert seq % GLA_GROUP == 0 and GLA_GROUP % GLA_CHUNK == 0
    return pl.pallas_call(
        _gla_kernel,
        out_shape=jax.ShapeDtypeStruct((n, GLA_HEADS * GLA_DV), BF16),
        grid=(batch, GLA_HEADS),
        in_specs=[
            pl.BlockSpec((seq, GLA_DK), lambda b, h: (b, COL_QB // GLA_DK + h)),
            pl.BlockSpec((seq, GLA_DK), lambda b, h: (b, COL_KB // GLA_DK + h)),
            pl.BlockSpec((seq, GLA_DV), lambda b, h: (b, COL_VB // GLA_DV + h)),
            pl.BlockSpec((seq, LANES), lambda b, h: (b, 0)),
            pl.BlockSpec((LANES, GLA_DK), lambda b, h: (0, h)),
            pl.BlockSpec((LANES, GLA_DK), lambda b, h: (0, h)),
            pl.BlockSpec((1, GLA_DK), lambda b, h: (0, h)),
            pl.BlockSpec((1, GLA_DV), lambda b, h: (0, 0)),
        ],
        out_specs=pl.BlockSpec((seq, GLA_DV), lambda b, h: (b, h)),
        scratch_shapes=[pltpu.VMEM((GLA_DV, GLA_DK), F32),
                        pltpu.VMEM((seq, GLA_DK), BF16),
                        pltpu.VMEM((seq, GLA_DK), BF16),
                        pltpu.VMEM((seq // GLA_CHUNK, GLA_DK), F32),
                        pltpu.VMEM((seq, GLA_DV), F32)],
        compiler_params=_cparams(("arbitrary", "arbitrary")),
        name="gla",
    )(p_act, p_act, p_act, a_lr, w2_hi, w2_lo, b_alpha, gla_norm_g)


def _merge_kernel(ga_ref, gb_ref, rb_ref, oa_ref, ob_ref, x_ref, wout_ref, gffn_ref, wrh_ref, wrl_ref, br_ref,
                  x2_ref, xs_ref, slot_ref, gate_ref, cnt_ref):
    tm = x_ref.shape[0]
    slots = xs_ref.shape[0] // PAIR
    ga = ga_ref[...].astype(F32)
    gb = gb_ref[...].astype(F32)
    rb = rb_ref[...].astype(F32)
    ob = ob_ref[...].astype(F32) * (rb * _sigmoid(rb))
    mix = _sigmoid(ga) * oa_ref[...].astype(F32) + _sigmoid(gb) * ob
    x2 = x_ref[...] + jnp.dot(mix.astype(BF16), wout_ref[...], preferred_element_type=F32)
    x2_ref[...] = x2
    ms = jnp.mean(x2 * x2, axis=-1, keepdims=True)
    h2 = x2 * lax.rsqrt(ms + NORM_EPS) * gffn_ref[...]

    logits = _dot_split(h2, wrh_ref[...], wrl_ref[...]) + br_ref[...]
    lane = lax.broadcasted_iota(I32, (tm, LANES), 1)
    neg_inf = jnp.asarray(-jnp.inf, F32)
    work = jnp.where(lane < N_EXPERTS, logits, neg_inf)
    vals, idxs = [], []
    for _ in range(TOP_K):
        m = jnp.max(work, axis=-1, keepdims=True)
        idx = jnp.min(jnp.where(work == m, lane, LANES), axis=-1, keepdims=True)
        vals.append(m)
        idxs.append(idx)
        work = jnp.where(lane == idx, neg_inf, work)
    exps = [jnp.exp(v - vals[0]) for v in vals]
    denom = exps[0] + exps[1] + exps[2] + exps[3]

    r_i = lax.broadcasted_iota(I32, (tm, tm), 0)
    c_i = lax.broadcasted_iota(I32, (tm, tm), 1)
    lower = jnp.where(r_i > c_i, 1.0, 0.0).astype(BF16)
    base = jnp.zeros((1, LANES), F32)
    onehots, ranks = [], []
    for k in range(TOP_K):
        onehot = lane == idxs[k]
        before = jnp.dot(lower, jnp.where(onehot, 1.0, 0.0).astype(BF16), preferred_element_type=F32)
        onehots.append(onehot)
        ranks.append(before + base)
        base = base + jnp.sum(jnp.where(onehot, 1.0, 0.0), axis=0, keepdims=True)
    cnt8 = jnp.floor((base + (SUBLANES - 1)) * (1.0 / SUBLANES))
    e_i = lax.broadcasted_iota(I32, (LANES, LANES), 0)
    e_j = lax.broadcasted_iota(I32, (LANES, LANES), 1)
    upper = jnp.where(e_i < e_j, 1.0, 0.0).astype(BF16)
    seg8 = jnp.dot(jnp.broadcast_to(cnt8, (SUBLANES, LANES)).astype(BF16), upper,
                   preferred_element_type=F32)[0:1, :]
    seg_row = seg8 * float(SUBLANES)
    slot_out = jnp.full((tm, LANES), -1.0, F32)
    gate_out = jnp.zeros((tm, LANES), F32)
    for k in range(TOP_K):
        slot_k = jnp.sum(jnp.where(onehots[k], ranks[k] + seg_row, 0.0), axis=-1, keepdims=True)
        slot_out = jnp.where(lane == k, slot_k, slot_out)
        gate_out = jnp.where(lane == k, exps[k] / denom, gate_out)
    slot_ref[...] = slot_out.astype(I32)
    gate_ref[...] = gate_out
    cnt_ref[...] = jnp.broadcast_to(base, (SUBLANES, LANES))

    slot_t = slot_out.T
    p_i = lax.broadcasted_iota(I32, (slots, tm), 0).astype(F32)
    perm = jnp.where(p_i == slot_t[0:1, :], 1.0, 0.0)
    for k in range(1, TOP_K):
        perm = perm + jnp.where(p_i == slot_t[k:k + 1, :], 1.0, 0.0)
    perm = perm.astype(BF16)
    h2b = h2.astype(BF16)
    cw = 2 * PACK_GROUP
    for c in range(D_MODEL // cw):
        y = jnp.dot(perm, h2b[:, c * cw:(c + 1) * cw], preferred_element_type=F32)
        lo, hi = _split_cols(y)
        xs_ref[:, c * (cw // 2):(c + 1) * (cw // 2)] = _pack_rows(lo, hi)


def _merge(p_act, o_a, o_b, x2d, w_out, g_ffn, w_router_hi, w_router_lo, b_router):
    n = x2d.shape[0]
    tm = MERGE_TM
    nt = n // tm
    row = lambda c: pl.BlockSpec((tm, D_MODEL), lambda i: (i, c))
    const = lambda shape: pl.BlockSpec(shape, lambda i: (0, 0))
    aux = pl.BlockSpec((tm, LANES), lambda i: (i, 0))
    return pl.pallas_call(
        _merge_kernel,
        out_shape=(
            jax.ShapeDtypeStruct((n, D_MODEL), F32),
            jax.ShapeDtypeStruct((nt * PAIR * MERGE_SLOTS, D_HALF), BF16),
            jax.ShapeDtypeStruct((n, LANES), I32),
            jax.ShapeDtypeStruct((n, LANES), F32),
            jax.ShapeDtypeStruct((nt * SUBLANES, LANES), F32),
        ),
        grid=(nt,),
        in_specs=[
            row(COL_GA // D_MODEL), row(COL_GB // D_MODEL), row(COL_RB // D_MODEL),
            row(0), row(0), row(0),
            pl.BlockSpec((D_MODEL, D_MODEL), lambda i: (0, 0), pipeline_mode=pl.Buffered(1)),
            const((1, D_MODEL)),
            pl.BlockSpec((D_MODEL, LANES), lambda i: (0, 0), pipeline_mode=pl.Buffered(1)),
            pl.BlockSpec((D_MODEL, LANES), lambda i: (0, 0), pipeline_mode=pl.Buffered(1)),
            const((1, LANES)),
        ],
        out_specs=(row(0), pl.BlockSpec((PAIR * MERGE_SLOTS, D_HALF), lambda i: (i, 0)), aux, aux,
                   pl.BlockSpec((SUBLANES, LANES), lambda i: (i, 0))),
        compiler_params=_cparams(("arbitrary",)),
        name="merge",
    )(p_act, p_act, p_act, o_a, o_b, x2d, w_out, g_ffn, w_router_hi, w_router_lo, b_router)


def _experts_kernel(nitems_ref, ie_ref, it0_ref, it1_ref, ilo_ref, ihi_ref, c8_ref, src8_ref, cum8_ref,
                    xs_ref, wg_ref, wl_ref, bg_ref, bl_ref, wd_ref, bd_ref,
                    ys_ref, bufs, act_scr, sem_in, sem_out):
    w = pl.program_id(0)
    s = pl.program_id(1)
    nfa = D_FF // EXPERT_TF
    nfb = D_HALF // EXPERT_TN
    n_items = nitems_ref[0]
    valid = w < n_items
    cur = lax.rem(w, 2)
    unit = PAIR * SUBLANES

    def for_each_copy(item, slot, outbound, fn):
        e = ie_ref[item]
        lo = ilo_ref[item]
        hi = ihi_ref[item]
        buf = bufs.at[slot]

        def tile_body(t, carry):
            seg = t * N_EXPERTS + e
            seg0 = cum8_ref[seg]
            first = jnp.maximum(seg0, lo)
            c = jnp.maximum(jnp.minimum(seg0 + c8_ref[seg], hi) - first, 0)
            hbm0 = (src8_ref[seg] + (first - seg0)) * unit
            buf0 = (first - lo) * unit

            def copies(bits, off):
                for bit in bits:
                    rows = unit << bit
                    flag = lax.bitwise_and(lax.shift_right_logical(c, bit), 1)

                    @pl.when(flag == 1)
                    def _(rows=rows, off=off):
                        in_buf = buf.at[pl.ds(pl.multiple_of(buf0 + off, unit), rows), :]
                        if outbound:
                            fn(pltpu.make_async_copy(
                                in_buf, ys_ref.at[pl.ds(pl.multiple_of(hbm0 + off, unit), rows), :], sem_out))
                        else:
                            fn(pltpu.make_async_copy(
                                xs_ref.at[pl.ds(pl.multiple_of(hbm0 + off, unit), rows), :], in_buf, sem_in))

                    off = off + flag * rows

            low_bits = 3
            high = lax.shift_right_logical(c, low_bits)

            @pl.when(high > 0)
            def _():
                copies(range(SEG_BITS - 1, low_bits - 1, -1), jnp.zeros((), I32))

            copies(range(low_bits - 1, -1, -1), lax.shift_left(high, low_bits) * unit)
            return carry

        lax.fori_loop(it0_ref[item], it1_ref[item], tile_body, 0)

    start = lambda cp: cp.start()
    wait = lambda cp: cp.wait()

    @pl.when((w == 0) & (s == 0))
    def _():
        bufs[...] = jnp.zeros_like(bufs)
        act_scr[...] = jnp.zeros_like(act_scr)
        for_each_copy(0, 0, False, start)

    @pl.when(valid & (s == 0))
    def _():
        for_each_copy(w, cur, False, wait)

    n_sub = lax.div((ihi_ref[w] - ilo_ref[w]) * SUBLANES + (EXPERT_SUB - 1), EXPERT_SUB)

    @pl.when(valid & (s < nfa))
    def _():
        def packed_order(w_ref):
            half = PACK_GROUP // 2
            groups = range(D_MODEL // PACK_GROUP)
            return jnp.concatenate(
                [w_ref[g * PACK_GROUP:g * PACK_GROUP + half, :].astype(BF16) for g in groups]
                + [w_ref[g * PACK_GROUP + half:(g + 1) * PACK_GROUP, :].astype(BF16) for g in groups], axis=0)

        wg = packed_order(wg_ref)
        wl = packed_order(wl_ref)
        bg = bg_ref[...]
        bl = bl_ref[...]
        tile = jnp.minimum(s, nfa - 1)

        def sub_body(r, carry):
            z = bufs[cur, pl.ds(pl.multiple_of(r * (PAIR * EXPERT_SUB), PAIR * EXPERT_SUB),
                                PAIR * EXPERT_SUB), :]
            lo, hi = _unpack_rows(z)
            lo = lo.astype(BF16)
            hi = hi.astype(BF16)
            glu = (jnp.dot(lo, wg[0:D_HALF], preferred_element_type=F32)
                   + jnp.dot(hi, wg[D_HALF:D_MODEL], preferred_element_type=F32) + bg)
            lin = (jnp.dot(lo, wl[0:D_HALF], preferred_element_type=F32)
                   + jnp.dot(hi, wl[D_HALF:D_MODEL], preferred_element_type=F32) + bl)
            glu = jnp.minimum(glu, SWIGLU_LIMIT)
            lin = jnp.clip(lin, -SWIGLU_LIMIT, SWIGLU_LIMIT)
            act = glu * _sigmoid(SWIGLU_ALPHA * glu) * (lin + 1.0)
            act_scr[tile, pl.ds(pl.multiple_of(r * EXPERT_SUB, EXPERT_SUB), EXPERT_SUB), :] = act.astype(BF16)
            return carry

        lax.fori_loop(0, n_sub, sub_body, 0)

    @pl.when(valid & (s == nfa))
    def _():
        @pl.when(w >= 1)
        def _():
            for_each_copy(w - 1, 1 - cur, True, wait)

        @pl.when(w + 1 < n_items)
        def _():
            for_each_copy(w + 1, 1 - cur, False, start)

    @pl.when(valid & (s >= nfa))
    def _():
        wd = wd_ref[...].astype(BF16)
        bd = bd_ref[...]
        col = pl.multiple_of(jnp.maximum(s - nfa, 0) * EXPERT_TN, EXPERT_TN)

        def sub_body(r, carry):
            rows = pl.ds(pl.multiple_of(r * EXPERT_SUB, EXPERT_SUB), EXPERT_SUB)
            acc = jnp.dot(act_scr[0, rows, :], wd[0:EXPERT_TF], preferred_element_type=F32)
            for t in range(1, nfa):
                acc += jnp.dot(act_scr[t, rows, :], wd[t * EXPERT_TF:(t + 1) * EXPERT_TF],
                               preferred_element_type=F32)
            acc = acc + bd
            bufs[cur, pl.ds(pl.multiple_of(r * (PAIR * EXPERT_SUB), PAIR * EXPERT_SUB), PAIR * EXPERT_SUB),
                 pl.ds(col, EXPERT_TN)] = _pack_rows(*_split_cols(acc))
            return carry

        lax.fori_loop(0, n_sub, sub_body, 0)

    @pl.when(valid & (s == nfa + nfb - 1))
    def _():
        for_each_copy(w, cur, True, start)

        @pl.when(w == n_items - 1)
        def _():
            for_each_copy(w, cur, True, wait)


def _experts(x_sorted, meta, n_items_max, w_gu, b_gu, w_down, b_down):
    nfa = D_FF // EXPERT_TF
    nfb = D_HALF // EXPERT_TN
    n_pre = len(meta)

    def fa(w, s, n_items):
        return jnp.where(w < n_items[0], jnp.minimum(s, nfa - 1), nfa - 1)

    def fb(w, s, n_items):
        return jnp.where(w < n_items[0], jnp.maximum(s - nfa, 0), nfb - 1)

    def wspec(shape, col):
        return pl.BlockSpec(shape, lambda w, s, *m: (m[1][w], 0, col(w, s, m[0])))

    any_spec = pl.BlockSpec(memory_space=pl.ANY)
    grid_spec = pltpu.PrefetchScalarGridSpec(
        num_scalar_prefetch=n_pre,
        grid=(n_items_max, nfa + nfb),
        in_specs=[
            any_spec,
            wspec((None, D_MODEL, EXPERT_TF), fa),
            wspec((None, D_MODEL, EXPERT_TF), lambda w, s, n: nfa + fa(w, s, n)),
            wspec((None, 1, EXPERT_TF), fa),
            wspec((None, 1, EXPERT_TF), lambda w, s, n: nfa + fa(w, s, n)),
            wspec((None, D_FF, 2 * EXPERT_TN), fb),
            wspec((None, 1, 2 * EXPERT_TN), fb),
        ],
        out_specs=any_spec,
        scratch_shapes=[pltpu.VMEM((2, PAIR * EXPERT_ROWS, D_HALF), BF16),
                        pltpu.VMEM((nfa, EXPERT_ROWS, EXPERT_TF), BF16),
                        pltpu.SemaphoreType.DMA, pltpu.SemaphoreType.DMA],
    )
    return pl.pallas_call(
        _experts_kernel,
        out_shape=jax.ShapeDtypeStruct(x_sorted.shape, x_sorted.dtype),
        grid_spec=grid_spec,
        input_output_aliases={n_pre: 0},
        compiler_params=_cparams(("arbitrary", "arbitrary")),
        name="experts",
    )(*meta, x_sorted, w_gu, w_gu, b_gu, b_gu, w_down, b_down)


def _combine_kernel(ys_ref, slot_ref, gate_ref, x2_ref, gfin_ref, o_ref):
    tm = x2_ref.shape[0]
    slots = ys_ref.shape[0] // PAIR
    slot = slot_ref[...]
    gates = gate_ref[...]
    p_j = lax.broadcasted_iota(I32, (tm, slots), 1)
    gmat = jnp.where(p_j == slot[:, 0:1], gates[:, 0:1], 0.0)
    for k in range(1, TOP_K):
        gmat = gmat + jnp.where(p_j == slot[:, k:k + 1], gates[:, k:k + 1], 0.0)
    gmat = gmat.astype(BF16)
    y_lo, y_hi = _unpack_rows(ys_ref[...])
    moe_lo = jnp.dot(gmat, y_lo.astype(BF16), preferred_element_type=F32)
    moe_hi = jnp.dot(gmat, y_hi.astype(BF16), preferred_element_type=F32)
    x = x2_ref[...] + _merge_cols(moe_lo, moe_hi)
    ms = jnp.mean(x * x, axis=-1, keepdims=True)
    o_ref[...] = x * lax.rsqrt(ms + NORM_EPS) * gfin_ref[...]


def _combine(y_sorted, slots, gates, x2, g_final):
    n = x2.shape[0]
    tm = MERGE_TM
    aux = pl.BlockSpec((tm, LANES), lambda i: (i, 0))
    return pl.pallas_call(
        _combine_kernel,
        out_shape=jax.ShapeDtypeStruct((n, D_MODEL), F32),
        grid=(n // tm,),
        in_specs=[
            pl.BlockSpec((PAIR * MERGE_SLOTS, D_HALF), lambda i: (i, 0)),
            aux, aux,
            pl.BlockSpec((tm, D_MODEL), lambda i: (i, 0)),
            pl.BlockSpec((1, D_MODEL), lambda i: (0, 0)),
        ],
        out_specs=pl.BlockSpec((tm, D_MODEL), lambda i: (i, 0)),
        compiler_params=_cparams(("arbitrary",)),
        name="combine",
    )(y_sorted, slots, gates, x2, g_final)


def _layer(x2d, batch, seq, attn_norm_g, w_in, swa_sinks, w_alpha2, b_alpha, gla_norm_g, w_out,
           ffn_norm_g, w_router, b_router, w_gate_up, b_gate_up, w_down, b_down, final_norm_g):
    n = x2d.shape[0]
    assert n % MERGE_TM == 0 and seq % SWA_WINDOW == 0 and seq % GLA_CHUNK == 0
    o_qa, o_ka, o_va = 0, 2048, 2304
    o_qb, o_kb, o_vb, o_rb, o_al, o_ga, o_gb = 2560, 3584, 4608, 6656, 8704, 8720, 10768
    seg = lambda s, w: w_in[:, s:s + w]
    w_main = jnp.concatenate([
        seg(o_qa, 2048), seg(o_ga, 2048), seg(o_gb, 2048), seg(o_rb, 2048), seg(o_vb, 2048),
        seg(o_qb, 1024), seg(o_kb, 1024), seg(o_ka, 256), seg(o_va, 256)], axis=1).astype(BF16)
    w_a = jnp.pad(seg(o_al, GLA_GATE_RANK), ((0, 0), (0, LANES - GLA_GATE_RANK))).astype(BF16)
    w2_hi, w2_lo = _split_bf16(jnp.pad(w_alpha2, ((0, LANES - GLA_GATE_RANK), (0, 0))))
    wr_hi, wr_lo = _split_bf16(jnp.pad(w_router, ((0, 0), (0, LANES - N_EXPERTS))))
    b_r = jnp.pad(b_router, (0, LANES - N_EXPERTS)).reshape(1, LANES)
    slopes = jnp.exp2(-8.0 * jnp.arange(1, SWA_HEADS + 1, dtype=F32) / SWA_HEADS)

    p_act, a_lr = _inproj(x2d, attn_norm_g.reshape(1, D_MODEL), w_main, w_a)
    o_a = _swa(p_act, slopes, swa_sinks.astype(F32), batch, seq)
    o_b = _gla(p_act, a_lr, w2_hi, w2_lo, b_alpha.reshape(1, -1), gla_norm_g.reshape(1, -1), batch, seq)
    x2, x_sorted, slots, gates, cnt = _merge(
        p_act, o_a, o_b, x2d, w_out.astype(BF16), ffn_norm_g.reshape(1, D_MODEL), wr_hi, wr_lo, b_r)

    nt = n // MERGE_TM
    chunk8 = EXPERT_CHUNK // SUBLANES
    cnt_t = cnt[::SUBLANES, :N_EXPERTS].astype(I32)
    c8 = (cnt_t + SUBLANES - 1) // SUBLANES
    tile8 = (jnp.arange(nt, dtype=I32) * (MERGE_SLOTS // SUBLANES))[:, None]
    src8 = tile8 + jnp.cumsum(c8, axis=1) - c8
    cum8 = jnp.cumsum(c8, axis=0) - c8
    tot8 = jnp.sum(c8, axis=0)
    per_e = (tot8 + chunk8 - 1) // chunk8
    item_end = jnp.cumsum(per_e)
    n_items = item_end[-1]
    n_items_max = -(-(n * TOP_K // SUBLANES + nt * N_EXPERTS) // chunk8) + N_EXPERTS
    w_idx = jnp.arange(n_items_max, dtype=I32)
    w_eff = jnp.minimum(w_idx, n_items - 1)
    item_e = jnp.minimum(jnp.sum(item_end[None, :] <= w_eff[:, None], axis=1), N_EXPERTS - 1).astype(I32)
    item_j = w_eff - (item_end - per_e)[item_e]
    used = w_idx < n_items
    item_lo = item_j * chunk8
    item_hi = jnp.where(used, jnp.minimum(item_lo + chunk8, tot8[item_e]), item_lo)
    cum_e = cum8[:, item_e]
    item_t0 = jnp.sum((cum_e + c8[:, item_e]) <= item_lo[None, :], axis=0).astype(I32)
    item_t1 = jnp.where(used, jnp.sum(cum_e < item_hi[None, :], axis=0), item_t0).astype(I32)
    meta = (n_items.astype(I32).reshape(1), item_e, item_t0, item_t1, item_lo.astype(I32),
            item_hi.astype(I32), c8.reshape(-1), src8.reshape(-1), cum8.reshape(-1))

    def run_experts(grid_items):
        return lambda xs: _experts(xs, meta, grid_items, w_gate_up,
                                   b_gate_up.reshape(N_EXPERTS, 1, 2 * D_FF), w_down,
                                   b_down.reshape(N_EXPERTS, 1, D_MODEL))

    few_items = min(N_EXPERTS + N_EXPERTS // 4, n_items_max)
    y_sorted = lax.cond(n_items <= few_items, run_experts(few_items), run_experts(n_items_max), x_sorted)
    return _combine(y_sorted, slots, gates, x2, final_norm_g.reshape(1, D_MODEL))


def kernel(x, attn_norm_g, w_in, swa_sinks, w_alpha2, b_alpha, gla_norm_g, w_out, ffn_norm_g,
           w_router, b_router, w_gate_up, b_gate_up, w_down, b_down, final_norm_g):
    batch, seq, _ = x.shape
    assert w_in.shape[0] == 1, "single-layer block: the final norm is fused into the MoE combine"
    out = _layer(
        x.reshape(batch * seq, D_MODEL), batch, seq, attn_norm_g[0], w_in[0], swa_sinks[0],
        w_alpha2[0], b_alpha[0], gla_norm_g[0], w_out[0], ffn_norm_g[0], w_router[0], b_router[0],
        w_gate_up[0], b_gate_up[0], w_down[0], b_down[0], final_norm_g)
    return out.reshape(batch, seq, D_MODEL)
```

```python
import functools

import jax
import jax.numpy as jnp
from jax import lax
from jax.experimental import pallas as pl
from jax.experimental.pallas import tpu as pltpu

F32 = jnp.float32
BF16 = jnp.bfloat16
I32 = jnp.int32
U32 = jnp.uint32

D_MODEL = 2048
D_HALF = D_MODEL // 2
NORM_EPS = 1e-5
SWA_HEADS = 32
SWA_KV_HEADS = 4
SWA_HEAD_DIM = 64
SWA_WINDOW = 128
SWA_GROUP = SWA_HEADS // SWA_KV_HEADS
GLA_HEADS = 4
GLA_DK = 256
GLA_DV = 512
GLA_GATE_RANK = 16
GLA_TAU = 16.0
GLA_CHUNK = 64
GLA_GROUP = 256
N_EXPERTS = 32
TOP_K = 4
D_FF = 2048
SWIGLU_LIMIT = 7.0
SWIGLU_ALPHA = 1.702

LANES = 128
SUBLANES = 8
MASK_NEG = -1e30

COL_QA = 0
COL_GA = 2048
COL_GB = 4096
COL_RB = 6144
COL_VB = 8192
COL_QB = 10240
COL_KB = 11264
COL_KA = 12288
COL_VA = 12544
P_COLS = 12800

VMEM_LIMIT = 56 * 1024 * 1024

INPROJ_TM = 1024
INPROJ_TN = 1280
MERGE_TM = 256
MERGE_SLOTS = MERGE_TM * TOP_K + N_EXPERTS * SUBLANES
SEG_BITS = (MERGE_TM // SUBLANES).bit_length()
PAIR = 2
PACK_GROUP = 256
EXPERT_SUB = 592
EXPERT_CHUNK = 4 * EXPERT_SUB
EXPERT_ROWS = EXPERT_CHUNK
EXPERT_TF = 256
EXPERT_TN = 256


def _cparams(sem):
    return pltpu.CompilerParams(dimension_semantics=sem, vmem_limit_bytes=VMEM_LIMIT)


def _sigmoid(x):
    return 1.0 / (1.0 + jnp.exp(-x))


def _split_bf16(x):
    hi = x.astype(BF16)
    return hi, (x - hi.astype(F32)).astype(BF16)


def _dot_split(a, b_hi, b_lo):
    a_hi, a_lo = _split_bf16(a)
    return ((jnp.dot(a_hi, b_hi, preferred_element_type=F32)
             + jnp.dot(a_lo, b_hi, preferred_element_type=F32))
            + jnp.dot(a_hi, b_lo, preferred_element_type=F32))


def _split_cols(x):
    half = PACK_GROUP // 2
    groups = range(x.shape[1] // PACK_GROUP)
    return (jnp.concatenate([x[:, g * PACK_GROUP:g * PACK_GROUP + half] for g in groups], axis=1),
            jnp.concatenate([x[:, g * PACK_GROUP + half:(g + 1) * PACK_GROUP] for g in groups], axis=1))


def _merge_cols(lo, hi):
    half = PACK_GROUP // 2
    parts = []
    for g in range(lo.shape[1] // half):
        parts += [lo[:, g * half:(g + 1) * half], hi[:, g * half:(g + 1) * half]]
    return jnp.concatenate(parts, axis=1)


def _pack_rows(lo, hi):
    return pltpu.bitcast(pltpu.pack_elementwise([lo, hi], packed_dtype=BF16), BF16)


def _unpack_rows(z):
    words = pltpu.bitcast(z, U32)
    return (pltpu.unpack_elementwise(words, index=0, packed_dtype=BF16, unpacked_dtype=F32),
            pltpu.unpack_elementwise(words, index=1, packed_dtype=BF16, unpacked_dtype=F32))


def _inproj_kernel(x_ref, g_ref, w_ref, wa_ref, p_ref, a_ref, h_scr):
    @pl.when(pl.program_id(1) == 0)
    def _():
        x = x_ref[...]
        ms = jnp.mean(x * x, axis=-1, keepdims=True)
        h = (x * lax.rsqrt(ms + NORM_EPS) * g_ref[...]).astype(BF16)
        h_scr[...] = h
        a_ref[...] = jnp.dot(h, wa_ref[...], preferred_element_type=F32)

    p_ref[...] = jnp.dot(h_scr[...], w_ref[...], preferred_element_type=F32).astype(BF16)


def _inproj(x2d, g, w_main, w_a):
    n = x2d.shape[0]
    tm = min(INPROJ_TM, n)
    return pl.pallas_call(
        _inproj_kernel,
        out_shape=(jax.ShapeDtypeStruct((n, P_COLS), BF16),
                   jax.ShapeDtypeStruct((n, LANES), F32)),
        grid=(n // tm, P_COLS // INPROJ_TN),
        in_specs=[
            pl.BlockSpec((tm, D_MODEL), lambda i, j: (i, 0)),
            pl.BlockSpec((1, D_MODEL), lambda i, j: (0, 0)),
            pl.BlockSpec((D_MODEL, INPROJ_TN), lambda i, j: (0, j)),
            pl.BlockSpec((D_MODEL, LANES), lambda i, j: (0, 0)),
        ],
        out_specs=(
            pl.BlockSpec((tm, INPROJ_TN), lambda i, j: (i, j)),
            pl.BlockSpec((tm, LANES), lambda i, j: (i, 0)),
        ),
        scratch_shapes=[pltpu.VMEM((tm, D_MODEL), BF16)],
        compiler_params=_cparams(("arbitrary", "arbitrary")),
        name="inproj",
    )(x2d, g, w_main, w_a)


def _swa_kernel(slopes_ref, sinks_ref, q_ref, kp_ref, kc_ref, vp_ref, vc_ref, o_ref):
    n = pl.program_id(1)
    w = SWA_WINDOW
    hd = SWA_HEAD_DIM
    pairs = SWA_GROUP // 2
    lane = lax.broadcasted_iota(I32, (2 * w, LANES), 1)
    low = lane < hd
    kband = jnp.concatenate([kp_ref[...], kc_ref[...]], axis=0).astype(F32)
    vband = jnp.concatenate([vp_ref[...], vc_ref[...]], axis=0).astype(F32)
    qry = lax.broadcasted_iota(I32, (w, 2 * w), 0)
    key = lax.broadcasted_iota(I32, (w, 2 * w), 1)
    dist = qry - key + w
    valid = (dist >= 0) & (dist < w) & ((key >= w) | (n > 0))
    negd = jnp.where(valid, -dist.astype(F32), MASK_NEG)
    scale = jnp.asarray(hd ** -0.5, BF16)
    ones = jnp.ones((2 * w, LANES), BF16)
    out_lane_low = lax.broadcasted_iota(I32, (w, LANES), 1) < hd
    for kv in range(SWA_KV_HEADS):
        grp = slice((kv // 2) * LANES, (kv // 2 + 1) * LANES)
        kg, vg = kband[:, grp], vband[:, grp]
        kg_sw, vg_sw = pltpu.roll(kg, hd, axis=1), pltpu.roll(vg, hd, axis=1)
        if kv % 2 == 0:
            k_even, k_odd = jnp.where(low, kg, 0.0), jnp.where(low, 0.0, kg_sw)
            v_dup = jnp.where(low, vg, vg_sw)
        else:
            k_even, k_odd = jnp.where(low, kg_sw, 0.0), jnp.where(low, 0.0, kg)
            v_dup = jnp.where(low, vg_sw, vg)
        v_ext = jnp.concatenate([v_dup.astype(BF16), ones], axis=1)
        h0 = kv * SWA_GROUP
        q_pairs = jnp.concatenate(
            [q_ref[:, (h0 + 2 * i) * hd:(h0 + 2 * i + 2) * hd] for i in range(pairs)], axis=0) * scale
        nt = (((1,), (1,)), ((), ()))
        s_par = (lax.dot_general(q_pairs, k_even.astype(BF16), nt, preferred_element_type=F32),
                 lax.dot_general(q_pairs, k_odd.astype(BF16), nt, preferred_element_type=F32))
        probs, sink_terms = [], []
        for par in range(2):
            for i in range(pairs):
                h = h0 + 2 * i + par
                s = s_par[par][i * w:(i + 1) * w, :] + slopes_ref[h] * negd
                sink = sinks_ref[h]
                m = jnp.maximum(jnp.max(s, axis=-1, keepdims=True), sink)
                probs.append(jnp.exp(s - m).astype(BF16))
                sink_terms.append(jnp.exp(sink - m))
        o_ext = jnp.dot(jnp.concatenate(probs, axis=0), v_ext, preferred_element_type=F32)
        for i in range(pairs):
            halves = []
            for par in range(2):
                r0 = (par * pairs + i) * w
                blk = o_ext[r0:r0 + w, :]
                halves.append(blk[:, 0:LANES] / (blk[:, LANES:2 * LANES] + sink_terms[par * pairs + i]))
            c0 = (h0 + 2 * i) * hd
            o_ref[:, c0:c0 + LANES] = jnp.where(out_lane_low, halves[0], halves[1]).astype(BF16)


def _swa(p_act, slopes, sinks, batch, seq):
    n = batch * seq
    w = SWA_WINDOW
    nb = seq // w
    kvw = SWA_KV_HEADS * SWA_HEAD_DIM

    def cur(col):
        return lambda b, i: (b * nb + i, col)

    def prev(col):
        return lambda b, i: (b * nb + jnp.maximum(i - 1, 0), col)

    smem = pl.BlockSpec(memory_space=pltpu.SMEM)
    return pl.pallas_call(
        _swa_kernel,
        out_shape=jax.ShapeDtypeStruct((n, D_MODEL), BF16),
        grid=(batch, nb),
        in_specs=[
            smem, smem,
            pl.BlockSpec((w, D_MODEL), cur(COL_QA // D_MODEL)),
            pl.BlockSpec((w, kvw), prev(COL_KA // kvw)),
            pl.BlockSpec((w, kvw), cur(COL_KA // kvw)),
            pl.BlockSpec((w, kvw), prev(COL_VA // kvw)),
            pl.BlockSpec((w, kvw), cur(COL_VA // kvw)),
        ],
        out_specs=pl.BlockSpec((w, D_MODEL), lambda b, i: (b * nb + i, 0)),
        compiler_params=_cparams(("arbitrary", "arbitrary")),
        name="swa",
    )(slopes, sinks, p_act, p_act, p_act, p_act, p_act)


def _gla_kernel(q_ref, k_ref, v_ref, a_ref, w2h_ref, w2l_ref, ba_ref, gn_ref, o_ref,
                st_scr, qi_scr, ks_scr, dec_scr, intra_scr):
    c = GLA_CHUNK
    grp = GLA_GROUP
    cpg = grp // c
    seq = q_ref.shape[0]
    row = lax.broadcasted_iota(I32, (grp, grp), 0)
    col = lax.broadcasted_iota(I32, (grp, grp), 1)
    tri = jnp.where((row >= col) & (row // c == col // c), 1.0, 0.0).astype(BF16)
    causal = lax.broadcasted_iota(I32, (c, c), 0) >= lax.broadcasted_iota(I32, (c, c), 1)
    scale = jnp.asarray(GLA_DK ** -0.5, F32)

    def group(gi, carry):
        g0 = pl.multiple_of(gi * grp, grp)
        z = _dot_split(a_ref[pl.ds(g0, grp), :], w2h_ref[...], w2l_ref[...]) + ba_ref[...]
        log_a = (jnp.minimum(z, 0.0) - jnp.log1p(jnp.exp(-jnp.abs(z)))) * (1.0 / GLA_TAU)
        la_hi, la_mid = _split_bf16(log_a)
        la_lo = ((log_a - la_hi.astype(F32)) - la_mid.astype(F32)).astype(BF16)
        b_all = ((jnp.dot(tri, la_hi, preferred_element_type=F32)
                  + jnp.dot(tri, la_mid, preferred_element_type=F32))
                 + jnp.dot(tri, la_lo, preferred_element_type=F32))
        qf_all = q_ref[pl.ds(g0, grp), :].astype(F32) * scale
        kf_all = k_ref[pl.ds(g0, grp), :].astype(F32)
        atts = []
        for j in range(cpg):
            r0 = pl.multiple_of(g0 + j * c, c)
            b = b_all[j * c:(j + 1) * c, :]
            qf = qf_all[j * c:(j + 1) * c, :]
            kf = kf_all[j * c:(j + 1) * c, :]
            b_last = b[c - 1:c, :]
            b_mid = b[c // 2 - 1:c // 2, :]
            qi_scr[pl.ds(r0, c), :] = (qf * jnp.exp(b)).astype(BF16)
            ks_scr[pl.ds(r0, c), :] = (kf * jnp.exp(b_last - b)).astype(BF16)
            dec_scr[pl.ds(gi * cpg + j, 1), :] = jnp.exp(b_last)
            q_intra = (qf * jnp.exp(b - b_mid)).astype(BF16)
            k_intra = (kf * jnp.exp(b_mid - b)).astype(BF16)
            atts.append(lax.dot_general(q_intra, k_intra, (((1,), (1,)), ((), ())),
                                        preferred_element_type=F32))
        for j in range(cpg):
            r0 = pl.multiple_of(g0 + j * c, c)
            att = jnp.where(causal, atts[j], 0.0).astype(BF16)
            intra_scr[pl.ds(r0, c), :] = jnp.dot(att, v_ref[pl.ds(r0, c), :], preferred_element_type=F32)
        return carry

    lax.fori_loop(0, seq // grp, group, 0, unroll=2)

    st_scr[...] = jnp.zeros_like(st_scr)

    def scan_group(gi, carry):
        rows = [pl.multiple_of((gi * cpg + j) * c, c) for j in range(cpg)]
        upds = [lax.dot_general(v_ref[pl.ds(r0, c), :], ks_scr[pl.ds(r0, c), :], (((0,), (0,)), ((), ())),
                                preferred_element_type=F32) for r0 in rows]
        st = st_scr[...]
        for j, r0 in enumerate(rows):
            inter = lax.dot_general(qi_scr[pl.ds(r0, c), :], st.astype(BF16), (((1,), (1,)), ((), ())),
                                    preferred_element_type=F32)
            st = st * dec_scr[pl.ds(gi * cpg + j, 1), :] + upds[j]
            o = inter + intra_scr[pl.ds(r0, c), :]
            ms = jnp.mean(o * o, axis=-1, keepdims=True)
            o_ref[pl.ds(r0, c), :] = (o * lax.rsqrt(ms + NORM_EPS) * gn_ref[...]).astype(BF16)
        st_scr[...] = st
        return carry

    lax.fori_loop(0, seq // grp, scan_group, 0)


def _gla(p_act, a_lr, w2_hi, w2_lo, b_alpha, gla_norm_g, batch, seq):
    n = batch * seq
    assert seq % GLA_GROUP == 0 and GLA_GROUP % GLA_CHUNK == 0
    return pl.pallas_call(
        _gla_kernel,
        out_shape=jax.ShapeDtypeStruct((n, GLA_HEADS * GLA_DV), BF16),
        grid=(batch, GLA_HEADS),
        in_specs=[
            pl.BlockSpec((seq, GLA_DK), lambda b, h: (b, COL_QB // GLA_DK + h)),
            pl.BlockSpec((seq, GLA_DK), lambda b, h: (b, COL_KB // GLA_DK + h)),
            pl.BlockSpec((seq, GLA_DV), lambda b, h: (b, COL_VB // GLA_DV + h)),
            pl.BlockSpec((seq, LANES), lambda b, h: (b, 0)),
            pl.BlockSpec((LANES, GLA_DK), lambda b, h: (0, h)),
            pl.BlockSpec((LANES, GLA_DK), lambda b, h: (0, h)),
            pl.BlockSpec((1, GLA_DK), lambda b, h: (0, h)),
            pl.BlockSpec((1, GLA_DV), lambda b, h: (0, 0)),
        ],
        out_specs=pl.BlockSpec((seq, GLA_DV), lambda b, h: (b, h)),
        scratch_shapes=[pltpu.VMEM((GLA_DV, GLA_DK), F32),
                        pltpu.VMEM((seq, GLA_DK), BF16),
                        pltpu.VMEM((seq, GLA_DK), BF16),
                        pltpu.VMEM((seq // GLA_CHUNK, GLA_DK), F32),
                        pltpu.VMEM((seq, GLA_DV), F32)],
        compiler_params=_cparams(("arbitrary", "arbitrary")),
        name="gla",
    )(p_act, p_act, p_act, a_lr, w2_hi, w2_lo, b_alpha, gla_norm_g)


def _merge_kernel(ga_ref, gb_ref, rb_ref, oa_ref, ob_ref, x_ref, wout_ref, gffn_ref, wr_ref, br_ref,
                  x2_ref, xs_ref, slot_ref, gate_ref, cnt_ref):
    tm = x_ref.shape[0]
    slots = xs_ref.shape[0] // PAIR
    ga = ga_ref[...].astype(F32)
    gb = gb_ref[...].astype(F32)
    rb = rb_ref[...].astype(F32)
    ob = ob_ref[...].astype(F32) * (rb * _sigmoid(rb))
    mix = _sigmoid(ga) * oa_ref[...].astype(F32) + _sigmoid(gb) * ob
    x2 = x_ref[...] + jnp.dot(mix.astype(BF16), wout_ref[...], preferred_element_type=F32)
    x2_ref[...] = x2
    ms = jnp.mean(x2 * x2, axis=-1, keepdims=True)
    h2 = x2 * lax.rsqrt(ms + NORM_EPS) * gffn_ref[...]

    h_hi, h_lo = _split_bf16(h2)
    w_hilo = wr_ref[...]
    hw = jnp.dot(h_hi, w_hilo, preferred_element_type=F32)
    logits = ((hw[:, 0:LANES] + jnp.dot(h_lo, w_hilo[:, 0:LANES], preferred_element_type=F32))
              + hw[:, LANES:2 * LANES]) + br_ref[...]
    lane = lax.broadcasted_iota(I32, (tm, LANES), 1)
    neg_inf = jnp.asarray(-jnp.inf, F32)
    work = jnp.where(lane < N_EXPERTS, logits, neg_inf)
    vals, idxs = [], []
    for _ in range(TOP_K):
        m = jnp.max(work, axis=-1, keepdims=True)
        idx = jnp.min(jnp.where(work == m, lane, LANES), axis=-1, keepdims=True)
        vals.append(m)
        idxs.append(idx)
        work = jnp.where(lane == idx, neg_inf, work)
    exps = [jnp.exp(v - vals[0]) for v in vals]
    denom = exps[0] + exps[1] + exps[2] + exps[3]

    r_i = lax.broadcasted_iota(I32, (tm, tm), 0)
    c_i = lax.broadcasted_iota(I32, (tm, tm), 1)
    lower = jnp.where(r_i > c_i, 1.0, 0.0).astype(BF16)
    base = jnp.zeros((1, LANES), F32)
    ranks = []
    onehots = [lane == idxs[k] for k in range(TOP_K)]
    before_all = jnp.dot(lower, jnp.concatenate([jnp.where(oh, 1.0, 0.0).astype(BF16) for oh in onehots], axis=1),
                         preferred_element_type=F32)
    for k in range(TOP_K):
        ranks.append(before_all[:, k * LANES:(k + 1) * LANES] + base)
        base = base + jnp.sum(jnp.where(onehots[k], 1.0, 0.0), axis=0, keepdims=True)
    cnt8 = jnp.floor((base + (SUBLANES - 1)) * (1.0 / SUBLANES))
    e_i = lax.broadcasted_iota(I32, (LANES, LANES), 0)
    e_j = lax.broadcasted_iota(I32, (LANES, LANES), 1)
    upper = jnp.where(e_i < e_j, 1.0, 0.0).astype(BF16)
    seg8 = jnp.dot(jnp.broadcast_to(cnt8, (SUBLANES, LANES)).astype(BF16), upper,
                   preferred_element_type=F32)[0:1, :]
    seg_row = seg8 * float(SUBLANES)
    slot_out = jnp.full((tm, LANES), -1.0, F32)
    gate_out = jnp.zeros((tm, LANES), F32)
    for k in range(TOP_K):
        slot_k = jnp.sum(jnp.where(onehots[k], ranks[k] + seg_row, 0.0), axis=-1, keepdims=True)
        slot_out = jnp.where(lane == k, slot_k, slot_out)
        gate_out = jnp.where(lane == k, exps[k] / denom, gate_out)
    slot_ref[...] = slot_out.astype(I32)
    gate_ref[...] = gate_out
    cnt_ref[...] = jnp.broadcast_to(base, (SUBLANES, LANES))

    slot_t = slot_out.T
    p_i = lax.broadcasted_iota(I32, (slots, tm), 0).astype(F32)
    perm = jnp.where(p_i == slot_t[0:1, :], 1.0, 0.0)
    for k in range(1, TOP_K):
        perm = perm + jnp.where(p_i == slot_t[k:k + 1, :], 1.0, 0.0)
    perm = perm.astype(BF16)
    h2b = h2.astype(BF16)
    cw = 2 * PACK_GROUP
    for c in range(D_MODEL // cw):
        y = jnp.dot(perm, h2b[:, c * cw:(c + 1) * cw], preferred_element_type=F32)
        lo, hi = _split_cols(y)
        xs_ref[:, c * (cw // 2):(c + 1) * (cw // 2)] = _pack_rows(lo, hi)


def _merge(p_act, o_a, o_b, x2d, w_out, g_ffn, w_router_hilo, b_router):
    n = x2d.shape[0]
    tm = MERGE_TM
    nt = n // tm
    row = lambda c: pl.BlockSpec((tm, D_MODEL), lambda i: (i, c))
    const = lambda shape: pl.BlockSpec(shape, lambda i: (0, 0))
    aux = pl.BlockSpec((tm, LANES), lambda i: (i, 0))
    return pl.pallas_call(
        _merge_kernel,
        out_shape=(
            jax.ShapeDtypeStruct((n, D_MODEL), F32),
            jax.ShapeDtypeStruct((nt * PAIR * MERGE_SLOTS, D_HALF), BF16),
            jax.ShapeDtypeStruct((n, LANES), I32),
            jax.ShapeDtypeStruct((n, LANES), F32),
            jax.ShapeDtypeStruct((nt * SUBLANES, LANES), F32),
        ),
        grid=(nt,),
        in_specs=[
            row(COL_GA // D_MODEL), row(COL_GB // D_MODEL), row(COL_RB // D_MODEL),
            row(0), row(0), row(0),
            pl.BlockSpec((D_MODEL, D_MODEL), lambda i: (0, 0), pipeline_mode=pl.Buffered(1)),
            const((1, D_MODEL)),
            pl.BlockSpec((D_MODEL, 2 * LANES), lambda i: (0, 0), pipeline_mode=pl.Buffered(1)),
            const((1, LANES)),
        ],
        out_specs=(row(0), pl.BlockSpec((PAIR * MERGE_SLOTS, D_HALF), lambda i: (i, 0)), aux, aux,
                   pl.BlockSpec((SUBLANES, LANES), lambda i: (i, 0))),
        compiler_params=_cparams(("arbitrary",)),
        name="merge",
    )(p_act, p_act, p_act, o_a, o_b, x2d, w_out, g_ffn, w_router_hilo, b_router)


def _experts_kernel(nitems_ref, ie_ref, it0_ref, it1_ref, ilo_ref, ihi_ref, c8_ref, src8_ref, cum8_ref,
                    xs_ref, wg_ref, wl_ref, bg_ref, bl_ref, wd_ref, bd_ref,
                    ys_ref, bufs, act_scr, sem_in, sem_out):
    w = pl.program_id(0)
    s = pl.program_id(1)
    nfa = D_FF // EXPERT_TF
    nfb = D_HALF // EXPERT_TN
    n_items = nitems_ref[0]
    valid = w < n_items
    cur = lax.rem(w, 2)
    unit = PAIR * SUBLANES

    def for_each_copy(item, slot, outbound, fn):
        e = ie_ref[item]
        lo = ilo_ref[item]
        hi = ihi_ref[item]
        buf = bufs.at[slot]

        def tile_body(t, carry):
            seg = t * N_EXPERTS + e
            seg0 = cum8_ref[seg]
            first = jnp.maximum(seg0, lo)
            c = jnp.maximum(jnp.minimum(seg0 + c8_ref[seg], hi) - first, 0)
            hbm0 = (src8_ref[seg] + (first - seg0)) * unit
            buf0 = (first - lo) * unit

            def copies(bits, off):
                for bit in bits:
                    rows = unit << bit
                    flag = lax.bitwise_and(lax.shift_right_logical(c, bit), 1)

                    @pl.when(flag == 1)
                    def _(rows=rows, off=off):
                        in_buf = buf.at[pl.ds(pl.multiple_of(buf0 + off, unit), rows), :]
                        if outbound:
                            fn(pltpu.make_async_copy(
                                in_buf, ys_ref.at[pl.ds(pl.multiple_of(hbm0 + off, unit), rows), :], sem_out))
                        else:
                            fn(pltpu.make_async_copy(
                                xs_ref.at[pl.ds(pl.multiple_of(hbm0 + off, unit), rows), :], in_buf, sem_in))

                    off = off + flag * rows

            low_bits = 3
            high = lax.shift_right_logical(c, low_bits)

            @pl.when(high > 0)
            def _():
                copies(range(SEG_BITS - 1, low_bits - 1, -1), jnp.zeros((), I32))

            copies(range(low_bits - 1, -1, -1), lax.shift_left(high, low_bits) * unit)
            return carry

        lax.fori_loop(it0_ref[item], it1_ref[item], tile_body, 0)

    start = lambda cp: cp.start()
    wait = lambda cp: cp.wait()

    @pl.when((w == 0) & (s == 0))
    def _():
        bufs[...] = jnp.zeros_like(bufs)
        act_scr[...] = jnp.zeros_like(act_scr)
        for_each_copy(0, 0, False, start)

    @pl.when(valid & (s == 0))
    def _():
        for_each_copy(w, cur, False, wait)

    n_sub = lax.div((ihi_ref[w] - ilo_ref[w]) * SUBLANES + (EXPERT_SUB - 1), EXPERT_SUB)

    @pl.when(valid & (s < nfa))
    def _():
        def packed_order(w_ref):
            half = PACK_GROUP // 2
            groups = range(D_MODEL // PACK_GROUP)
            return jnp.concatenate(
                [w_ref[g * PACK_GROUP:g * PACK_GROUP + half, :].astype(BF16) for g in groups]
                + [w_ref[g * PACK_GROUP + half:(g + 1) * PACK_GROUP, :].astype(BF16) for g in groups], axis=0)

        wg = packed_order(wg_ref)
        wl = packed_order(wl_ref)
        bg = bg_ref[...]
        bl = bl_ref[...]
        tile = jnp.minimum(s, nfa - 1)

        def sub_body(r, carry):
            z = bufs[cur, pl.ds(pl.multiple_of(r * (PAIR * EXPERT_SUB), PAIR * EXPERT_SUB),
                                PAIR * EXPERT_SUB), :]
            lo, hi = _unpack_rows(z)
            lo = lo.astype(BF16)
            hi = hi.astype(BF16)
            glu = (jnp.dot(lo, wg[0:D_HALF], preferred_element_type=F32)
                   + jnp.dot(hi, wg[D_HALF:D_MODEL], preferred_element_type=F32) + bg)
            lin = (jnp.dot(lo, wl[0:D_HALF], preferred_element_type=F32)
                   + jnp.dot(hi, wl[D_HALF:D_MODEL], preferred_element_type=F32) + bl)
            glu = jnp.minimum(glu, SWIGLU_LIMIT)
            lin = jnp.clip(lin, -SWIGLU_LIMIT, SWIGLU_LIMIT)
            act = glu * _sigmoid(SWIGLU_ALPHA * glu) * (lin + 1.0)
            act_scr[tile, pl.ds(pl.multiple_of(r * EXPERT_SUB, EXPERT_SUB), EXPERT_SUB), :] = act.astype(BF16)
            return carry

        lax.fori_loop(0, n_sub, sub_body, 0)

    @pl.when(valid & (s == nfa))
    def _():
        @pl.when(w >= 1)
        def _():
            for_each_copy(w - 1, 1 - cur, True, wait)

        @pl.when(w + 1 < n_items)
        def _():
            for_each_copy(w + 1, 1 - cur, False, start)

    @pl.when(valid & (s >= nfa))
    def _():
        wd = wd_ref[...].astype(BF16)
        bd = bd_ref[...]
        col = pl.multiple_of(jnp.maximum(s - nfa, 0) * EXPERT_TN, EXPERT_TN)

        def sub_body(r, carry):
            rows = pl.ds(pl.multiple_of(r * EXPERT_SUB, EXPERT_SUB), EXPERT_SUB)
            acc = jnp.dot(act_scr[0, rows, :], wd[0:EXPERT_TF], preferred_element_type=F32)
            for t in range(1, nfa):
                acc += jnp.dot(act_scr[t, rows, :], wd[t * EXPERT_TF:(t + 1) * EXPERT_TF],
                               preferred_element_type=F32)
            acc = acc + bd
            bufs[cur, pl.ds(pl.multiple_of(r * (PAIR * EXPERT_SUB), PAIR * EXPERT_SUB), PAIR * EXPERT_SUB),
                 pl.ds(col, EXPERT_TN)] = _pack_rows(*_split_cols(acc))
            return carry

        lax.fori_loop(0, n_sub, sub_body, 0)

    @pl.when(valid & (s == nfa + nfb - 1))
    def _():
        for_each_copy(w, cur, True, start)

        @pl.when(w == n_items - 1)
        def _():
            for_each_copy(w, cur, True, wait)


def _experts(x_sorted, meta, n_items_max, w_gu, b_gu, w_down, b_down):
    nfa = D_FF // EXPERT_TF
    nfb = D_HALF // EXPERT_TN
    n_pre = len(meta)

    def fa(w, s, n_items):
        return jnp.where(w < n_items[0], jnp.minimum(s, nfa - 1), nfa - 1)

    def fb(w, s, n_items):
        return jnp.where(w < n_items[0], jnp.maximum(s - nfa, 0), nfb - 1)

    def wspec(shape, col):
        return pl.BlockSpec(shape, lambda w, s, *m: (m[1][w], 0, col(w, s, m[0])))

    any_spec = pl.BlockSpec(memory_space=pl.ANY)
    grid_spec = pltpu.PrefetchScalarGridSpec(
        num_scalar_prefetch=n_pre,
        grid=(n_items_max, nfa + nfb),
        in_specs=[
            any_spec,
            wspec((None, D_MODEL, EXPERT_TF), fa),
            wspec((None, D_MODEL, EXPERT_TF), lambda w, s, n: nfa + fa(w, s, n)),
            wspec((None, 1, EXPERT_TF), fa),
            wspec((None, 1, EXPERT_TF), lambda w, s, n: nfa + fa(w, s, n)),
            wspec((None, D_FF, 2 * EXPERT_TN), fb),
            wspec((None, 1, 2 * EXPERT_TN), fb),
        ],
        out_specs=any_spec,
        scratch_shapes=[pltpu.VMEM((2, PAIR * EXPERT_ROWS, D_HALF), BF16),
                        pltpu.VMEM((nfa, EXPERT_ROWS, EXPERT_TF), BF16),
                        pltpu.SemaphoreType.DMA, pltpu.SemaphoreType.DMA],
    )
    return pl.pallas_call(
        _experts_kernel,
        out_shape=jax.ShapeDtypeStruct(x_sorted.shape, x_sorted.dtype),
        grid_spec=grid_spec,
        input_output_aliases={n_pre: 0},
        compiler_params=_cparams(("arbitrary", "arbitrary")),
        name="experts",
    )(*meta, x_sorted, w_gu, w_gu, b_gu, b_gu, w_down, b_down)


def _combine_kernel(ys_ref, slot_ref, gate_ref, x2_ref, gfin_ref, o_ref):
    tm = x2_ref.shape[0]
    slots = ys_ref.shape[0] // PAIR
    slot = slot_ref[...]
    gates = gate_ref[...]
    p_j = lax.broadcasted_iota(I32, (tm, slots), 1)
    gmat = jnp.where(p_j == slot[:, 0:1], gates[:, 0:1], 0.0)
    for k in range(1, TOP_K):
        gmat = gmat + jnp.where(p_j == slot[:, k:k + 1], gates[:, k:k + 1], 0.0)
    gmat = gmat.astype(BF16)
    y_lo, y_hi = _unpack_rows(ys_ref[...])
    moe_lo = jnp.dot(gmat, y_lo.astype(BF16), preferred_element_type=F32)
    moe_hi = jnp.dot(gmat, y_hi.astype(BF16), preferred_element_type=F32)
    x = x2_ref[...] + _merge_cols(moe_lo, moe_hi)
    ms = jnp.mean(x * x, axis=-1, keepdims=True)
    o_ref[...] = x * lax.rsqrt(ms + NORM_EPS) * gfin_ref[...]


def _combine(y_sorted, slots, gates, x2, g_final):
    n = x2.shape[0]
    tm = MERGE_TM
    aux = pl.BlockSpec((tm, LANES), lambda i: (i, 0))
    return pl.pallas_call(
        _combine_kernel,
        out_shape=jax.ShapeDtypeStruct((n, D_MODEL), F32),
        grid=(n // tm,),
        in_specs=[
            pl.BlockSpec((PAIR * MERGE_SLOTS, D_HALF), lambda i: (i, 0)),
            aux, aux,
            pl.BlockSpec((tm, D_MODEL), lambda i: (i, 0)),
            pl.BlockSpec((1, D_MODEL), lambda i: (0, 0)),
        ],
        out_specs=pl.BlockSpec((tm, D_MODEL), lambda i: (i, 0)),
        compiler_params=_cparams(("arbitrary",)),
        name="combine",
    )(y_sorted, slots, gates, x2, g_final)


def _layer(x2d, batch, seq, attn_norm_g, w_in, swa_sinks, w_alpha2, b_alpha, gla_norm_g, w_out,
           ffn_norm_g, w_router, b_router, w_gate_up, b_gate_up, w_down, b_down, final_norm_g):
    n = x2d.shape[0]
    assert n % MERGE_TM == 0 and seq % SWA_WINDOW == 0 and seq % GLA_CHUNK == 0
    o_qa, o_ka, o_va = 0, 2048, 2304
    o_qb, o_kb, o_vb, o_rb, o_al, o_ga, o_gb = 2560, 3584, 4608, 6656, 8704, 8720, 10768
    seg = lambda s, w: w_in[:, s:s + w]
    w_main = jnp.concatenate([
        seg(o_qa, 2048), seg(o_ga, 2048), seg(o_gb, 2048), seg(o_rb, 2048), seg(o_vb, 2048),
        seg(o_qb, 1024), seg(o_kb, 1024), seg(o_ka, 256), seg(o_va, 256)], axis=1).astype(BF16)
    w_a = jnp.pad(seg(o_al, GLA_GATE_RANK), ((0, 0), (0, LANES - GLA_GATE_RANK))).astype(BF16)
    w2_hi, w2_lo = _split_bf16(jnp.pad(w_alpha2, ((0, LANES - GLA_GATE_RANK), (0, 0))))
    wr_hi, wr_lo = _split_bf16(jnp.pad(w_router, ((0, 0), (0, LANES - N_EXPERTS))))
    b_r = jnp.pad(b_router, (0, LANES - N_EXPERTS)).reshape(1, LANES)
    slopes = jnp.exp2(-8.0 * jnp.arange(1, SWA_HEADS + 1, dtype=F32) / SWA_HEADS)

    p_act, a_lr = _inproj(x2d, attn_norm_g.reshape(1, D_MODEL), w_main, w_a)
    o_a = _swa(p_act, slopes, swa_sinks.astype(F32), batch, seq)
    o_b = _gla(p_act, a_lr, w2_hi, w2_lo, b_alpha.reshape(1, -1), gla_norm_g.reshape(1, -1), batch, seq)
    x2, x_sorted, slots, gates, cnt = _merge(
        p_act, o_a, o_b, x2d, w_out.astype(BF16), ffn_norm_g.reshape(1, D_MODEL),
        jnp.concatenate([wr_hi, wr_lo], axis=1), b_r)

    nt = n // MERGE_TM
    chunk8 = EXPERT_CHUNK // SUBLANES
    cnt_t = cnt[::SUBLANES, :N_EXPERTS].astype(I32)
    c8 = (cnt_t + SUBLANES - 1) // SUBLANES
    tile8 = (jnp.arange(nt, dtype=I32) * (MERGE_SLOTS // SUBLANES))[:, None]
    src8 = tile8 + jnp.cumsum(c8, axis=1) - c8
    cum8 = jnp.cumsum(c8, axis=0) - c8
    tot8 = jnp.sum(c8, axis=0)
    per_e = (tot8 + chunk8 - 1) // chunk8
    item_end = jnp.cumsum(per_e)
    n_items = item_end[-1]
    n_items_max = -(-(n * TOP_K // SUBLANES + nt * N_EXPERTS) // chunk8) + N_EXPERTS
    w_idx = jnp.arange(n_items_max, dtype=I32)
    w_eff = jnp.minimum(w_idx, n_items - 1)
    item_e = jnp.minimum(jnp.sum(item_end[None, :] <= w_eff[:, None], axis=1), N_EXPERTS - 1).astype(I32)
    item_j = w_eff - (item_end - per_e)[item_e]
    used = w_idx < n_items
    item_lo = item_j * chunk8
    item_hi = jnp.where(used, jnp.minimum(item_lo + chunk8, tot8[item_e]), item_lo)
    cum_e = cum8[:, item_e]
    item_t0 = jnp.sum((cum_e + c8[:, item_e]) <= item_lo[None, :], axis=0).astype(I32)
    item_t1 = jnp.where(used, jnp.sum(cum_e < item_hi[None, :], axis=0), item_t0).astype(I32)
    meta = (n_items.astype(I32).reshape(1), item_e, item_t0, item_t1, item_lo.astype(I32),
            item_hi.astype(I32), c8.reshape(-1), src8.reshape(-1), cum8.reshape(-1))

    def run_experts(grid_items):
        return lambda xs: _experts(xs, meta, grid_items, w_gate_up,
                                   b_gate_up.reshape(N_EXPERTS, 1, 2 * D_FF), w_down,
                                   b_down.reshape(N_EXPERTS, 1, D_MODEL))

    few_items = min(N_EXPERTS + N_EXPERTS // 4, n_items_max)
    y_sorted = lax.cond(n_items <= few_items, run_experts(few_items), run_experts(n_items_max), x_sorted)
    return _combine(y_sorted, slots, gates, x2, final_norm_g.reshape(1, D_MODEL))


def kernel(x, attn_norm_g, w_in, swa_sinks, w_alpha2, b_alpha, gla_norm_g, w_out, ffn_norm_g,
           w_router, b_router, w_gate_up, b_gate_up, w_down, b_down, final_norm_g):
    batch, seq, _ = x.shape
    assert w_in.shape[0] == 1, "single-layer block: the final norm is fused into the MoE combine"
    out = _layer(
        x.reshape(batch * seq, D_MODEL), batch, seq, attn_norm_g[0], w_in[0], swa_sinks[0],
        w_alpha2[0], b_alpha[0], gla_norm_g[0], w_out[0], ffn_norm_g[0], w_router[0], b_router[0],
        w_gate_up[0], b_gate_up[0], w_down[0], b_down[0], final_norm_g)
    return out.reshape(batch, seq, D_MODEL)
```

```python
import functools

import jax
import jax.numpy as jnp
from jax import lax
from jax.experimental import pallas as pl
from jax.experimental.pallas import tpu as pltpu

F32 = jnp.float32
BF16 = jnp.bfloat16
I32 = jnp.int32
U32 = jnp.uint32

D_MODEL = 2048
D_HALF = D_MODEL // 2
NORM_EPS = 1e-5
SWA_HEADS = 32
SWA_KV_HEADS = 4
SWA_HEAD_DIM = 64
SWA_WINDOW = 128
SWA_GROUP = SWA_HEADS // SWA_KV_HEADS
GLA_HEADS = 4
GLA_DK = 256
GLA_DV = 512
GLA_GATE_RANK = 16
GLA_TAU = 16.0
GLA_CHUNK = 64
GLA_GROUP = 256
N_EXPERTS = 32
TOP_K = 4
D_FF = 2048
SWIGLU_LIMIT = 7.0
SWIGLU_ALPHA = 1.702

LANES = 128
SUBLANES = 8
MASK_NEG = -1e30

COL_QA = 0
COL_GA = 2048
COL_GB = 4096
COL_RB = 6144
COL_VB = 8192
COL_QB = 10240
COL_KB = 11264
COL_KA = 12288
COL_VA = 12544
P_COLS = 12800

VMEM_LIMIT = 56 * 1024 * 1024

INPROJ_TM = 1024
INPROJ_TN = 1280
MERGE_TM = 256
MERGE_SLOTS = MERGE_TM * TOP_K + N_EXPERTS * SUBLANES
SEG_BITS = (MERGE_TM // SUBLANES).bit_length()
PAIR = 2
PACK_GROUP = 256
EXPERT_SUB = 592
EXPERT_CHUNK = 4 * EXPERT_SUB
EXPERT_ROWS = EXPERT_CHUNK
EXPERT_TF = 256
EXPERT_TN = 256


def _cparams(sem):
    return pltpu.CompilerParams(dimension_semantics=sem, vmem_limit_bytes=VMEM_LIMIT)


def _sigmoid(x):
    return 1.0 / (1.0 + jnp.exp(-x))


def _split_bf16(x):
    hi = x.astype(BF16)
    return hi, (x - hi.astype(F32)).astype(BF16)


def _dot_split(a, b_hi, b_lo):
    a_hi, a_lo = _split_bf16(a)
    return ((jnp.dot(a_hi, b_hi, preferred_element_type=F32)
             + jnp.dot(a_lo, b_hi, preferred_element_type=F32))
            + jnp.dot(a_hi, b_lo, preferred_element_type=F32))


def _split_cols(x):
    half = PACK_GROUP // 2
    groups = range(x.shape[1] // PACK_GROUP)
    return (jnp.concatenate([x[:, g * PACK_GROUP:g * PACK_GROUP + half] for g in groups], axis=1),
            jnp.concatenate([x[:, g * PACK_GROUP + half:(g + 1) * PACK_GROUP] for g in groups], axis=1))


def _merge_cols(lo, hi):
    half = PACK_GROUP // 2
    parts = []
    for g in range(lo.shape[1] // half):
        parts += [lo[:, g * half:(g + 1) * half], hi[:, g * half:(g + 1) * half]]
    return jnp.concatenate(parts, axis=1)


def _pack_rows(lo, hi):
    return pltpu.bitcast(pltpu.pack_elementwise([lo, hi], packed_dtype=BF16), BF16)


def _unpack_rows(z):
    words = pltpu.bitcast(z, U32)
    return (pltpu.unpack_elementwise(words, index=0, packed_dtype=BF16, unpacked_dtype=F32),
            pltpu.unpack_elementwise(words, index=1, packed_dtype=BF16, unpacked_dtype=F32))


def _inproj_kernel(x_ref, g_ref, w_ref, wa_ref, p_ref, a_ref, h_scr):
    @pl.when(pl.program_id(1) == 0)
    def _():
        x = x_ref[...]
        ms = jnp.mean(x * x, axis=-1, keepdims=True)
        h = (x * lax.rsqrt(ms + NORM_EPS) * g_ref[...]).astype(BF16)
        h_scr[...] = h
        a_ref[...] = jnp.dot(h, wa_ref[...], preferred_element_type=F32)

    p_ref[...] = jnp.dot(h_scr[...], w_ref[...], preferred_element_type=F32).astype(BF16)


def _inproj(x2d, g, w_main, w_a):
    n = x2d.shape[0]
    tm = min(INPROJ_TM, n)
    return pl.pallas_call(
        _inproj_kernel,
        out_shape=(jax.ShapeDtypeStruct((n, P_COLS), BF16),
                   jax.ShapeDtypeStruct((n, LANES), F32)),
        grid=(n // tm, P_COLS // INPROJ_TN),
        in_specs=[
            pl.BlockSpec((tm, D_MODEL), lambda i, j: (i, 0)),
            pl.BlockSpec((1, D_MODEL), lambda i, j: (0, 0)),
            pl.BlockSpec((D_MODEL, INPROJ_TN), lambda i, j: (0, j)),
            pl.BlockSpec((D_MODEL, LANES), lambda i, j: (0, 0)),
        ],
        out_specs=(
            pl.BlockSpec((tm, INPROJ_TN), lambda i, j: (i, j)),
            pl.BlockSpec((tm, LANES), lambda i, j: (i, 0)),
        ),
        scratch_shapes=[pltpu.VMEM((tm, D_MODEL), BF16)],
        compiler_params=_cparams(("arbitrary", "arbitrary")),
        name="inproj",
    )(x2d, g, w_main, w_a)


def _swa_kernel(slopes_ref, sinks_ref, q_ref, kp_ref, kc_ref, vp_ref, vc_ref, o_ref):
    n = pl.program_id(1)
    w = SWA_WINDOW
    hd = SWA_HEAD_DIM
    pairs = SWA_GROUP // 2
    lane = lax.broadcasted_iota(I32, (2 * w, LANES), 1)
    low = lane < hd
    kband = jnp.concatenate([kp_ref[...], kc_ref[...]], axis=0).astype(F32)
    vband = jnp.concatenate([vp_ref[...], vc_ref[...]], axis=0).astype(F32)
    qry = lax.broadcasted_iota(I32, (w, 2 * w), 0)
    key = lax.broadcasted_iota(I32, (w, 2 * w), 1)
    dist = qry - key + w
    valid = (dist >= 0) & (dist < w) & ((key >= w) | (n > 0))
    negd = jnp.where(valid, -dist.astype(F32), MASK_NEG)
    scale = jnp.asarray(hd ** -0.5, BF16)
    ones = jnp.ones((2 * w, LANES), BF16)
    out_lane_low = lax.broadcasted_iota(I32, (w, LANES), 1) < hd
    for kv in range(SWA_KV_HEADS):
        grp = slice((kv // 2) * LANES, (kv // 2 + 1) * LANES)
        kg, vg = kband[:, grp], vband[:, grp]
        kg_sw, vg_sw = pltpu.roll(kg, hd, axis=1), pltpu.roll(vg, hd, axis=1)
        if kv % 2 == 0:
            k_even, k_odd = jnp.where(low, kg, 0.0), jnp.where(low, 0.0, kg_sw)
            v_dup = jnp.where(low, vg, vg_sw)
        else:
            k_even, k_odd = jnp.where(low, kg_sw, 0.0), jnp.where(low, 0.0, kg)
            v_dup = jnp.where(low, vg_sw, vg)
        v_ext = jnp.concatenate([v_dup.astype(BF16), ones], axis=1)
        h0 = kv * SWA_GROUP
        q_pairs = jnp.concatenate(
            [q_ref[:, (h0 + 2 * i) * hd:(h0 + 2 * i + 2) * hd] for i in range(pairs)], axis=0) * scale
        nt = (((1,), (1,)), ((), ()))
        s_par = (lax.dot_general(q_pairs, k_even.astype(BF16), nt, preferred_element_type=F32),
                 lax.dot_general(q_pairs, k_odd.astype(BF16), nt, preferred_element_type=F32))
        probs, sink_terms = [], []
        for par in range(2):
            for i in range(pairs):
                h = h0 + 2 * i + par
                s = s_par[par][i * w:(i + 1) * w, :] + slopes_ref[h] * negd
                sink = sinks_ref[h]
                m = jnp.maximum(jnp.max(s, axis=-1, keepdims=True), sink)
                probs.append(jnp.exp(s - m).astype(BF16))
                sink_terms.append(jnp.exp(sink - m))
        o_ext = jnp.dot(jnp.concatenate(probs, axis=0), v_ext, preferred_element_type=F32)
        for i in range(pairs):
            halves = []
            for par in range(2):
                r0 = (par * pairs + i) * w
                blk = o_ext[r0:r0 + w, :]
                halves.append(blk[:, 0:LANES] / (blk[:, LANES:2 * LANES] + sink_terms[par * pairs + i]))
            c0 = (h0 + 2 * i) * hd
            o_ref[:, c0:c0 + LANES] = jnp.where(out_lane_low, halves[0], halves[1]).astype(BF16)


def _swa(p_act, slopes, sinks, batch, seq):
    n = batch * seq
    w = SWA_WINDOW
    nb = seq // w
    kvw = SWA_KV_HEADS * SWA_HEAD_DIM

    def cur(col):
        return lambda b, i: (b * nb + i, col)

    def prev(col):
        return lambda b, i: (b * nb + jnp.maximum(i - 1, 0), col)

    smem = pl.BlockSpec(memory_space=pltpu.SMEM)
    return pl.pallas_call(
        _swa_kernel,
        out_shape=jax.ShapeDtypeStruct((n, D_MODEL), BF16),
        grid=(batch, nb),
        in_specs=[
            smem, smem,
            pl.BlockSpec((w, D_MODEL), cur(COL_QA // D_MODEL)),
            pl.BlockSpec((w, kvw), prev(COL_KA // kvw)),
            pl.BlockSpec((w, kvw), cur(COL_KA // kvw)),
            pl.BlockSpec((w, kvw), prev(COL_VA // kvw)),
            pl.BlockSpec((w, kvw), cur(COL_VA // kvw)),
        ],
        out_specs=pl.BlockSpec((w, D_MODEL), lambda b, i: (b * nb + i, 0)),
        compiler_params=_cparams(("arbitrary", "arbitrary")),
        name="swa",
    )(slopes, sinks, p_act, p_act, p_act, p_act, p_act)


def _gla_kernel(q_ref, k_ref, v_ref, a_ref, w2h_ref, w2l_ref, ba_ref, gn_ref, o_ref,
                st_scr, qi_scr, ks_scr, dec_scr, intra_scr):
    c = GLA_CHUNK
    grp = GLA_GROUP
    cpg = grp // c
    seq = q_ref.shape[0]
    row = lax.broadcasted_iota(I32, (grp, grp), 0)
    col = lax.broadcasted_iota(I32, (grp, grp), 1)
    tri = jnp.where((row >= col) & (row // c == col // c), 1.0, 0.0).astype(BF16)
    causal = lax.broadcasted_iota(I32, (c, c), 0) >= lax.broadcasted_iota(I32, (c, c), 1)
    scale = jnp.asarray(GLA_DK ** -0.5, F32)

    def group(gi, carry):
        g0 = pl.multiple_of(gi * grp, grp)
        z = _dot_split(a_ref[pl.ds(g0, grp), :], w2h_ref[...], w2l_ref[...]) + ba_ref[...]
        log_a = (jnp.minimum(z, 0.0) - jnp.log1p(jnp.exp(-jnp.abs(z)))) * (1.0 / GLA_TAU)
        la_hi, la_mid = _split_bf16(log_a)
        la_lo = ((log_a - la_hi.astype(F32)) - la_mid.astype(F32)).astype(BF16)
        b_all = ((jnp.dot(tri, la_hi, preferred_element_type=F32)
                  + jnp.dot(tri, la_mid, preferred_element_type=F32))
                 + jnp.dot(tri, la_lo, preferred_element_type=F32))
        qf_all = q_ref[pl.ds(g0, grp), :].astype(F32) * scale
        kf_all = k_ref[pl.ds(g0, grp), :].astype(F32)
        atts = []
        for j in range(cpg):
            r0 = pl.multiple_of(g0 + j * c, c)
            b = b_all[j * c:(j + 1) * c, :]
            qf = qf_all[j * c:(j + 1) * c, :]
            kf = kf_all[j * c:(j + 1) * c, :]
            b_last = b[c - 1:c, :]
            b_mid = b[c // 2 - 1:c // 2, :]
            qi_scr[pl.ds(r0, c), :] = (qf * jnp.exp(b)).astype(BF16)
            ks_scr[pl.ds(r0, c), :] = (kf * jnp.exp(b_last - b)).astype(BF16)
            dec_scr[pl.ds(gi * cpg + j, 1), :] = jnp.exp(b_last)
            q_intra = (qf * jnp.exp(b - b_mid)).astype(BF16)
            k_intra = (kf * jnp.exp(b_mid - b)).astype(BF16)
            atts.append(lax.dot_general(q_intra, k_intra, (((1,), (1,)), ((), ())),
                                        preferred_element_type=F32))
        for j in range(cpg):
            r0 = pl.multiple_of(g0 + j * c, c)
            att = jnp.where(causal, atts[j], 0.0).astype(BF16)
            intra_scr[pl.ds(r0, c), :] = jnp.dot(att, v_ref[pl.ds(r0, c), :], preferred_element_type=F32)
        return carry

    lax.fori_loop(0, seq // grp, group, 0, unroll=2)

    st_scr[...] = jnp.zeros_like(st_scr)

    def scan_group(gi, carry):
        rows = [pl.multiple_of((gi * cpg + j) * c, c) for j in range(cpg)]
        upds = [lax.dot_general(v_ref[pl.ds(r0, c), :], ks_scr[pl.ds(r0, c), :], (((0,), (0,)), ((), ())),
                                preferred_element_type=F32) for r0 in rows]
        st = st_scr[...]
        for j, r0 in enumerate(rows):
            inter = lax.dot_general(qi_scr[pl.ds(r0, c), :], st.astype(BF16), (((1,), (1,)), ((), ())),
                                    preferred_element_type=F32)
            st = st * dec_scr[pl.ds(gi * cpg + j, 1), :] + upds[j]
            o = inter + intra_scr[pl.ds(r0, c), :]
            ms = jnp.mean(o * o, axis=-1, keepdims=True)
            o_ref[pl.ds(r0, c), :] = (o * lax.rsqrt(ms + NORM_EPS) * gn_ref[...]).astype(BF16)
        st_scr[...] = st
        return carry

    lax.fori_loop(0, seq // grp, scan_group, 0, unroll=4)


def _gla(p_act, a_lr, w2_hi, w2_lo, b_alpha, gla_norm_g, batch, seq):
    n = batch * seq
    assert seq % GLA_GROUP == 0 and GLA_GROUP % GLA_CHUNK == 0
    return pl.pallas_call(
        _gla_kernel,
        out_shape=jax.ShapeDtypeStruct((n, GLA_HEADS * GLA_DV), BF16),
        grid=(batch, GLA_HEADS),
        in_specs=[
            pl.BlockSpec((seq, GLA_DK), lambda b, h: (b, COL_QB // GLA_DK + h)),
            pl.BlockSpec((seq, GLA_DK), lambda b, h: (b, COL_KB // GLA_DK + h)),
            pl.BlockSpec((seq, GLA_DV), lambda b, h: (b, COL_VB // GLA_DV + h)),
            pl.BlockSpec((seq, LANES), lambda b, h: (b, 0)),
            pl.BlockSpec((LANES, GLA_DK), lambda b, h: (0, h)),
            pl.BlockSpec((LANES, GLA_DK), lambda b, h: (0, h)),
            pl.BlockSpec((1, GLA_DK), lambda b, h: (0, h)),
            pl.BlockSpec((1, GLA_DV), lambda b, h: (0, 0)),
        ],
        out_specs=pl.BlockSpec((seq, GLA_DV), lambda b, h: (b, h)),
        scratch_shapes=[pltpu.VMEM((GLA_DV, GLA_DK), F32),
                        pltpu.VMEM((seq, GLA_DK), BF16),
                        pltpu.VMEM((seq, GLA_DK), BF16),
                        pltpu.VMEM((seq // GLA_CHUNK, GLA_DK), F32),
                        pltpu.VMEM((seq, GLA_DV), F32)],
        compiler_params=_cparams(("arbitrary", "arbitrary")),
        name="gla",
    )(p_act, p_act, p_act, a_lr, w2_hi, w2_lo, b_alpha, gla_norm_g)


def _merge_kernel(ga_ref, gb_ref, rb_ref, oa_ref, ob_ref, x_ref, wout_ref, gffn_ref, wr_ref, br_ref,
                  x2_ref, xs_ref, slot_ref, gate_ref, cnt_ref):
    tm = x_ref.shape[0]
    slots = xs_ref.shape[0] // PAIR
    ga = ga_ref[...].astype(F32)
    gb = gb_ref[...].astype(F32)
    rb = rb_ref[...].astype(F32)
    ob = ob_ref[...].astype(F32) * (rb * _sigmoid(rb))
    mix = _sigmoid(ga) * oa_ref[...].astype(F32) + _sigmoid(gb) * ob
    x2 = x_ref[...] + jnp.dot(mix.astype(BF16), wout_ref[...], preferred_element_type=F32)
    x2_ref[...] = x2
    ms = jnp.mean(x2 * x2, axis=-1, keepdims=True)
    h2 = x2 * lax.rsqrt(ms + NORM_EPS) * gffn_ref[...]

    h_hi, h_lo = _split_bf16(h2)
    w_hilo = wr_ref[...]
    hw = jnp.dot(h_hi, w_hilo, preferred_element_type=F32)
    logits = ((hw[:, 0:LANES] + jnp.dot(h_lo, w_hilo[:, 0:LANES], preferred_element_type=F32))
              + hw[:, LANES:2 * LANES]) + br_ref[...]
    lane = lax.broadcasted_iota(I32, (tm, LANES), 1)
    neg_inf = jnp.asarray(-jnp.inf, F32)
    work = jnp.where(lane < N_EXPERTS, logits, neg_inf)
    vals, idxs = [], []
    for _ in range(TOP_K):
        m = jnp.max(work, axis=-1, keepdims=True)
        idx = jnp.min(jnp.where(work == m, lane, LANES), axis=-1, keepdims=True)
        vals.append(m)
        idxs.append(idx)
        work = jnp.where(lane == idx, neg_inf, work)
    exps = [jnp.exp(v - vals[0]) for v in vals]
    denom = exps[0] + exps[1] + exps[2] + exps[3]

    r_i = lax.broadcasted_iota(I32, (tm, tm), 0)
    c_i = lax.broadcasted_iota(I32, (tm, tm), 1)
    lower = jnp.where(r_i > c_i, 1.0, 0.0).astype(BF16)
    base = jnp.zeros((1, LANES), F32)
    ranks = []
    onehots = [lane == idxs[k] for k in range(TOP_K)]
    before_all = jnp.dot(lower, jnp.concatenate([jnp.where(oh, 1.0, 0.0).astype(BF16) for oh in onehots], axis=1),
                         preferred_element_type=F32)
    for k in range(TOP_K):
        ranks.append(before_all[:, k * LANES:(k + 1) * LANES] + base)
        base = base + jnp.sum(jnp.where(onehots[k], 1.0, 0.0), axis=0, keepdims=True)
    cnt8 = jnp.floor((base + (SUBLANES - 1)) * (1.0 / SUBLANES))
    e_i = lax.broadcasted_iota(I32, (LANES, LANES), 0)
    e_j = lax.broadcasted_iota(I32, (LANES, LANES), 1)
    upper = jnp.where(e_i < e_j, 1.0, 0.0).astype(BF16)
    seg8 = jnp.dot(jnp.broadcast_to(cnt8, (SUBLANES, LANES)).astype(BF16), upper,
                   preferred_element_type=F32)[0:1, :]
    seg_row = seg8 * float(SUBLANES)
    slot_out = jnp.full((tm, LANES), -1.0, F32)
    gate_out = jnp.zeros((tm, LANES), F32)
    for k in range(TOP_K):
        slot_k = jnp.sum(jnp.where(onehots[k], ranks[k] + seg_row, 0.0), axis=-1, keepdims=True)
        slot_out = jnp.where(lane == k, slot_k, slot_out)
        gate_out = jnp.where(lane == k, exps[k] / denom, gate_out)
    slot_ref[...] = slot_out.astype(I32)
    gate_ref[...] = gate_out
    cnt_ref[...] = jnp.broadcast_to(base, (SUBLANES, LANES))

    slot_t = slot_out.T
    p_i = lax.broadcasted_iota(I32, (slots, tm), 0).astype(F32)
    perm = jnp.where(p_i == slot_t[0:1, :], 1.0, 0.0)
    for k in range(1, TOP_K):
        perm = perm + jnp.where(p_i == slot_t[k:k + 1, :], 1.0, 0.0)
    perm = perm.astype(BF16)
    h2b = h2.astype(BF16)
    cw = 2 * PACK_GROUP
    for c in range(D_MODEL // cw):
        y = jnp.dot(perm, h2b[:, c * cw:(c + 1) * cw], preferred_element_type=F32)
        lo, hi = _split_cols(y)
        xs_ref[:, c * (cw // 2):(c + 1) * (cw // 2)] = _pack_rows(lo, hi)


def _merge(p_act, o_a, o_b, x2d, w_out, g_ffn, w_router_hilo, b_router):
    n = x2d.shape[0]
    tm = MERGE_TM
    nt = n // tm
    row = lambda c: pl.BlockSpec((tm, D_MODEL), lambda i: (i, c))
    const = lambda shape: pl.BlockSpec(shape, lambda i: (0, 0))
    aux = pl.BlockSpec((tm, LANES), lambda i: (i, 0))
    return pl.pallas_call(
        _merge_kernel,
        out_shape=(
            jax.ShapeDtypeStruct((n, D_MODEL), F32),
            jax.ShapeDtypeStruct((nt * PAIR * MERGE_SLOTS, D_HALF), BF16),
            jax.ShapeDtypeStruct((n, LANES), I32),
            jax.ShapeDtypeStruct((n, LANES), F32),
            jax.ShapeDtypeStruct((nt * SUBLANES, LANES), F32),
        ),
        grid=(nt,),
        in_specs=[
            row(COL_GA // D_MODEL), row(COL_GB // D_MODEL), row(COL_RB // D_MODEL),
            row(0), row(0), row(0),
            pl.BlockSpec((D_MODEL, D_MODEL), lambda i: (0, 0), pipeline_mode=pl.Buffered(1)),
            const((1, D_MODEL)),
            pl.BlockSpec((D_MODEL, 2 * LANES), lambda i: (0, 0), pipeline_mode=pl.Buffered(1)),
            const((1, LANES)),
        ],
        out_specs=(row(0), pl.BlockSpec((PAIR * MERGE_SLOTS, D_HALF), lambda i: (i, 0)), aux, aux,
                   pl.BlockSpec((SUBLANES, LANES), lambda i: (i, 0))),
        compiler_params=_cparams(("arbitrary",)),
        name="merge",
    )(p_act, p_act, p_act, o_a, o_b, x2d, w_out, g_ffn, w_router_hilo, b_router)


def _experts_kernel(nitems_ref, ie_ref, it0_ref, it1_ref, ilo_ref, ihi_ref, c8_ref, src8_ref, cum8_ref,
                    xs_ref, wg_ref, wl_ref, bg_ref, bl_ref, wd_ref, bd_ref,
                    ys_ref, bufs, act_scr, sem_in, sem_out):
    w = pl.program_id(0)
    s = pl.program_id(1)
    nfa = D_FF // EXPERT_TF
    nfb = D_HALF // EXPERT_TN
    n_items = nitems_ref[0]
    valid = w < n_items
    cur = lax.rem(w, 2)
    unit = PAIR * SUBLANES

    def for_each_copy(item, slot, outbound, fn):
        e = ie_ref[item]
        lo = ilo_ref[item]
        hi = ihi_ref[item]
        buf = bufs.at[slot]

        def tile_body(t, carry):
            seg = t * N_EXPERTS + e
            seg0 = cum8_ref[seg]
            first = jnp.maximum(seg0, lo)
            c = jnp.maximum(jnp.minimum(seg0 + c8_ref[seg], hi) - first, 0)
            hbm0 = (src8_ref[seg] + (first - seg0)) * unit
            buf0 = (first - lo) * unit

            def copies(bits, off):
                for bit in bits:
                    rows = unit << bit
                    flag = lax.bitwise_and(lax.shift_right_logical(c, bit), 1)

                    @pl.when(flag == 1)
                    def _(rows=rows, off=off):
                        in_buf = buf.at[pl.ds(pl.multiple_of(buf0 + off, unit), rows), :]
                        if outbound:
                            fn(pltpu.make_async_copy(
                                in_buf, ys_ref.at[pl.ds(pl.multiple_of(hbm0 + off, unit), rows), :], sem_out))
                        else:
                            fn(pltpu.make_async_copy(
                                xs_ref.at[pl.ds(pl.multiple_of(hbm0 + off, unit), rows), :], in_buf, sem_in))

                    off = off + flag * rows

            low_bits = 3
            high = lax.shift_right_logical(c, low_bits)

            @pl.when(high > 0)
            def _():
                copies(range(SEG_BITS - 1, low_bits - 1, -1), jnp.zeros((), I32))

            copies(range(low_bits - 1, -1, -1), lax.shift_left(high, low_bits) * unit)
            return carry

        lax.fori_loop(it0_ref[item], it1_ref[item], tile_body, 0)

    start = lambda cp: cp.start()
    wait = lambda cp: cp.wait()

    @pl.when((w == 0) & (s == 0))
    def _():
        bufs[...] = jnp.zeros_like(bufs)
        act_scr[...] = jnp.zeros_like(act_scr)
        for_each_copy(0, 0, False, start)

    @pl.when(valid & (s == 0))
    def _():
        for_each_copy(w, cur, False, wait)

    n_sub = lax.div((ihi_ref[w] - ilo_ref[w]) * SUBLANES + (EXPERT_SUB - 1), EXPERT_SUB)

    @pl.when(valid & (s < nfa))
    def _():
        def packed_order(w_ref):
            half = PACK_GROUP // 2
            groups = range(D_MODEL // PACK_GROUP)
            return jnp.concatenate(
                [w_ref[g * PACK_GROUP:g * PACK_GROUP + half, :].astype(BF16) for g in groups]
                + [w_ref[g * PACK_GROUP + half:(g + 1) * PACK_GROUP, :].astype(BF16) for g in groups], axis=0)

        wg = packed_order(wg_ref)
        wl = packed_order(wl_ref)
        bg = bg_ref[...]
        bl = bl_ref[...]
        tile = jnp.minimum(s, nfa - 1)

        def sub_body(r, carry):
            z = bufs[cur, pl.ds(pl.multiple_of(r * (PAIR * EXPERT_SUB), PAIR * EXPERT_SUB),
                                PAIR * EXPERT_SUB), :]
            lo, hi = _unpack_rows(z)
            lo = lo.astype(BF16)
            hi = hi.astype(BF16)
            glu = (jnp.dot(lo, wg[0:D_HALF], preferred_element_type=F32)
                   + jnp.dot(hi, wg[D_HALF:D_MODEL], preferred_element_type=F32) + bg)
            lin = (jnp.dot(lo, wl[0:D_HALF], preferred_element_type=F32)
                   + jnp.dot(hi, wl[D_HALF:D_MODEL], preferred_element_type=F32) + bl)
            glu = jnp.minimum(glu, SWIGLU_LIMIT)
            lin = jnp.clip(lin, -SWIGLU_LIMIT, SWIGLU_LIMIT)
            act = glu * _sigmoid(SWIGLU_ALPHA * glu) * (lin + 1.0)
            act_scr[tile, pl.ds(pl.multiple_of(r * EXPERT_SUB, EXPERT_SUB), EXPERT_SUB), :] = act.astype(BF16)
            return carry

        lax.fori_loop(0, n_sub, sub_body, 0)

    @pl.when(valid & (s == nfa))
    def _():
        @pl.when(w >= 1)
        def _():
            for_each_copy(w - 1, 1 - cur, True, wait)

        @pl.when(w + 1 < n_items)
        def _():
            for_each_copy(w + 1, 1 - cur, False, start)

    @pl.when(valid & (s >= nfa))
    def _():
        wd = wd_ref[...].astype(BF16)
        bd = bd_ref[...]
        col = pl.multiple_of(jnp.maximum(s - nfa, 0) * EXPERT_TN, EXPERT_TN)

        def sub_body(r, carry):
            rows = pl.ds(pl.multiple_of(r * EXPERT_SUB, EXPERT_SUB), EXPERT_SUB)
            acc = jnp.dot(act_scr[0, rows, :], wd[0:EXPERT_TF], preferred_element_type=F32)
            for t in range(1, nfa):
                acc += jnp.dot(act_scr[t, rows, :], wd[t * EXPERT_TF:(t + 1) * EXPERT_TF],
                               preferred_element_type=F32)
            acc = acc + bd
            bufs[cur, pl.ds(pl.multiple_of(r * (PAIR * EXPERT_SUB), PAIR * EXPERT_SUB), PAIR * EXPERT_SUB),
                 pl.ds(col, EXPERT_TN)] = _pack_rows(*_split_cols(acc))
            return carry

        lax.fori_loop(0, n_sub, sub_body, 0)

    @pl.when(valid & (s == nfa + nfb - 1))
    def _():
        for_each_copy(w, cur, True, start)

        @pl.when(w == n_items - 1)
        def _():
            for_each_copy(w, cur, True, wait)


def _experts(x_sorted, meta, n_items_max, w_gu, b_gu, w_down, b_down):
    nfa = D_FF // EXPERT_TF
    nfb = D_HALF // EXPERT_TN
    n_pre = len(meta)

    def fa(w, s, n_items):
        return jnp.where(w < n_items[0], jnp.minimum(s, nfa - 1), nfa - 1)

    def fb(w, s, n_items):
        return jnp.where(w < n_items[0], jnp.maximum(s - nfa, 0), nfb - 1)

    def wspec(shape, col):
        return pl.BlockSpec(shape, lambda w, s, *m: (m[1][w], 0, col(w, s, m[0])))

    any_spec = pl.BlockSpec(memory_space=pl.ANY)
    grid_spec = pltpu.PrefetchScalarGridSpec(
        num_scalar_prefetch=n_pre,
        grid=(n_items_max, nfa + nfb),
        in_specs=[
            any_spec,
            wspec((None, D_MODEL, EXPERT_TF), fa),
            wspec((None, D_MODEL, EXPERT_TF), lambda w, s, n: nfa + fa(w, s, n)),
            wspec((None, 1, EXPERT_TF), fa),
            wspec((None, 1, EXPERT_TF), lambda w, s, n: nfa + fa(w, s, n)),
            wspec((None, D_FF, 2 * EXPERT_TN), fb),
            wspec((None, 1, 2 * EXPERT_TN), fb),
        ],
        out_specs=any_spec,
        scratch_shapes=[pltpu.VMEM((2, PAIR * EXPERT_ROWS, D_HALF), BF16),
                        pltpu.VMEM((nfa, EXPERT_ROWS, EXPERT_TF), BF16),
                        pltpu.SemaphoreType.DMA, pltpu.SemaphoreType.DMA],
    )
    return pl.pallas_call(
        _experts_kernel,
        out_shape=jax.ShapeDtypeStruct(x_sorted.shape, x_sorted.dtype),
        grid_spec=grid_spec,
        input_output_aliases={n_pre: 0},
        compiler_params=_cparams(("arbitrary", "arbitrary")),
        name="experts",
    )(*meta, x_sorted, w_gu, w_gu, b_gu, b_gu, w_down, b_down)


def _combine_kernel(ys_ref, slot_ref, gate_ref, x2_ref, gfin_ref, o_ref):
    tm = x2_ref.shape[0]
    slots = ys_ref.shape[0] // PAIR
    slot = slot_ref[...]
    gates = gate_ref[...]
    p_j = lax.broadcasted_iota(I32, (tm, slots), 1)
    gmat = jnp.where(p_j == slot[:, 0:1], gates[:, 0:1], 0.0)
    for k in range(1, TOP_K):
        gmat = gmat + jnp.where(p_j == slot[:, k:k + 1], gates[:, k:k + 1], 0.0)
    gmat = gmat.astype(BF16)
    y_lo, y_hi = _unpack_rows(ys_ref[...])
    moe_lo = jnp.dot(gmat, y_lo.astype(BF16), preferred_element_type=F32)
    moe_hi = jnp.dot(gmat, y_hi.astype(BF16), preferred_element_type=F32)
    x = x2_ref[...] + _merge_cols(moe_lo, moe_hi)
    ms = jnp.mean(x * x, axis=-1, keepdims=True)
    o_ref[...] = x * lax.rsqrt(ms + NORM_EPS) * gfin_ref[...]


def _combine(y_sorted, slots, gates, x2, g_final):
    n = x2.shape[0]
    tm = MERGE_TM
    aux = pl.BlockSpec((tm, LANES), lambda i: (i, 0))
    return pl.pallas_call(
        _combine_kernel,
        out_shape=jax.ShapeDtypeStruct((n, D_MODEL), F32),
        grid=(n // tm,),
        in_specs=[
            pl.BlockSpec((PAIR * MERGE_SLOTS, D_HALF), lambda i: (i, 0)),
            aux, aux,
            pl.BlockSpec((tm, D_MODEL), lambda i: (i, 0)),
            pl.BlockSpec((1, D_MODEL), lambda i: (0, 0)),
        ],
        out_specs=pl.BlockSpec((tm, D_MODEL), lambda i: (i, 0)),
        compiler_params=_cparams(("arbitrary",)),
        name="combine",
    )(y_sorted, slots, gates, x2, g_final)


def _layer(x2d, batch, seq, attn_norm_g, w_in, swa_sinks, w_alpha2, b_alpha, gla_norm_g, w_out,
           ffn_norm_g, w_router, b_router, w_gate_up, b_gate_up, w_down, b_down, final_norm_g):
    n = x2d.shape[0]
    assert n % MERGE_TM == 0 and seq % SWA_WINDOW == 0 and seq % GLA_CHUNK == 0
    o_qa, o_ka, o_va = 0, 2048, 2304
    o_qb, o_kb, o_vb, o_rb, o_al, o_ga, o_gb = 2560, 3584, 4608, 6656, 8704, 8720, 10768
    seg = lambda s, w: w_in[:, s:s + w]
    w_main = jnp.concatenate([
        seg(o_qa, 2048), seg(o_ga, 2048), seg(o_gb, 2048), seg(o_rb, 2048), seg(o_vb, 2048),
        seg(o_qb, 1024), seg(o_kb, 1024), seg(o_ka, 256), seg(o_va, 256)], axis=1).astype(BF16)
    w_a = jnp.pad(seg(o_al, GLA_GATE_RANK), ((0, 0), (0, LANES - GLA_GATE_RANK))).astype(BF16)
    w2_hi, w2_lo = _split_bf16(jnp.pad(w_alpha2, ((0, LANES - GLA_GATE_RANK), (0, 0))))
    wr_hi, wr_lo = _split_bf16(jnp.pad(w_router, ((0, 0), (0, LANES - N_EXPERTS))))
    b_r = jnp.pad(b_router, (0, LANES - N_EXPERTS)).reshape(1, LANES)
    slopes = jnp.exp2(-8.0 * jnp.arange(1, SWA_HEADS + 1, dtype=F32) / SWA_HEADS)

    p_act, a_lr = _inproj(x2d, attn_norm_g.reshape(1, D_MODEL), w_main, w_a)
    o_a = _swa(p_act, slopes, swa_sinks.astype(F32), batch, seq)
    o_b = _gla(p_act, a_lr, w2_hi, w2_lo, b_alpha.reshape(1, -1), gla_norm_g.reshape(1, -1), batch, seq)
    x2, x_sorted, slots, gates, cnt = _merge(
        p_act, o_a, o_b, x2d, w_out.astype(BF16), ffn_norm_g.reshape(1, D_MODEL),
        jnp.concatenate([wr_hi, wr_lo], axis=1), b_r)

    nt = n // MERGE_TM
    chunk8 = EXPERT_CHUNK // SUBLANES
    cnt_t = cnt[::SUBLANES, :N_EXPERTS].astype(I32)
    c8 = (cnt_t + SUBLANES - 1) // SUBLANES
    tile8 = (jnp.arange(nt, dtype=I32) * (MERGE_SLOTS // SUBLANES))[:, None]
    src8 = tile8 + jnp.cumsum(c8, axis=1) - c8
    cum8 = jnp.cumsum(c8, axis=0) - c8
    tot8 = jnp.sum(c8, axis=0)
    per_e = (tot8 + chunk8 - 1) // chunk8
    item_end = jnp.cumsum(per_e)
    n_items = item_end[-1]
    n_items_max = -(-(n * TOP_K // SUBLANES + nt * N_EXPERTS) // chunk8) + N_EXPERTS
    w_idx = jnp.arange(n_items_max, dtype=I32)
    w_eff = jnp.minimum(w_idx, n_items - 1)
    item_e = jnp.minimum(jnp.sum(item_end[None, :] <= w_eff[:, None], axis=1), N_EXPERTS - 1).astype(I32)
    item_j = w_eff - (item_end - per_e)[item_e]
    used = w_idx < n_items
    item_lo = item_j * chunk8
    item_hi = jnp.where(used, jnp.minimum(item_lo + chunk8, tot8[item_e]), item_lo)
    cum_e = cum8[:, item_e]
    item_t0 = jnp.sum((cum_e + c8[:, item_e]) <= item_lo[None, :], axis=0).astype(I32)
    item_t1 = jnp.where(used, jnp.sum(cum_e < item_hi[None, :], axis=0), item_t0).astype(I32)
    meta = (n_items.astype(I32).reshape(1), item_e, item_t0, item_t1, item_lo.astype(I32),
            item_hi.astype(I32), c8.reshape(-1), src8.reshape(-1), cum8.reshape(-1))

    def run_experts(grid_items):
        return lambda xs: _experts(xs, meta, grid_items, w_gate_up,
                                   b_gate_up.reshape(N_EXPERTS, 1, 2 * D_FF), w_down,
                                   b_down.reshape(N_EXPERTS, 1, D_MODEL))

    few_items = min(N_EXPERTS + N_EXPERTS // 4, n_items_max)
    y_sorted = lax.cond(n_items <= few_items, run_experts(few_items), run_experts(n_items_max), x_sorted)
    return _combine(y_sorted, slots, gates, x2, final_norm_g.reshape(1, D_MODEL))


def kernel(x, attn_norm_g, w_in, swa_sinks, w_alpha2, b_alpha, gla_norm_g, w_out, ffn_norm_g,
           w_router, b_router, w_gate_up, b_gate_up, w_down, b_down, final_norm_g):
    batch, seq, _ = x.shape
    assert w_in.shape[0] == 1, "single-layer block: the final norm is fused into the MoE combine"
    out = _layer(
        x.reshape(batch * seq, D_MODEL), batch, seq, attn_norm_g[0], w_in[0], swa_sinks[0],
        w_alpha2[0], b_alpha[0], gla_norm_g[0], w_out[0], ffn_norm_g[0], w_router[0], b_router[0],
        w_gate_up[0], b_gate_up[0], w_down[0], b_down[0], final_norm_g)
    return out.reshape(batch, seq, D_MODEL)
```
